```python
import math
import jax
import jax.numpy as jnp
from jax import lax
import numpy as np

D_MODEL = 1024
BATCH = 8
SEQ = 4096
DEPTH = 1

MIX_WIDTH = D_MODEL
HYENA_WIDTH = MIX_WIDTH // 2
HYENA_ORDER = 2
HYENA_GROUPS = 8
SHORT_CONV_WIDTH = 3
FILTER_EMB_DIM = 33
FILTER_BANDS = (FILTER_EMB_DIM - 1) // 2
FILTER_HIDDEN = 64
DECAY_TARGET = 1e-2
FAST_DECAY_PCT = 0.3
SLOW_DECAY_PCT = 1.5
ATTN_WIDTH = MIX_WIDTH - HYENA_WIDTH
ATTN_HEADS = 4
ATTN_HEAD_DIM = ATTN_WIDTH // (2 * ATTN_HEADS)
Q_BLOCK = 128
HYENA_COLS = (HYENA_ORDER + 1) * HYENA_WIDTH
IN_COLS = HYENA_COLS + 3 * ATTN_WIDTH
N_EXPERTS = 16
EC_CAPACITY_FACTOR = 2
EXPERT_FF = 2816
NORM_EPS = 1e-6

kernel_name = 'hymba_hyena_diffattn_ec_block'


def _rmsnorm(x, g):
    xf = x.astype(jnp.float32)
    y = xf * lax.rsqrt(jnp.mean(xf * xf, axis=-1, keepdims=True) + NORM_EPS)
    return (y * g.astype(jnp.float32)).astype(x.dtype)


def _alibi_slopes(n):
    return jnp.asarray([2.0 ** (-8.0 * (i + 1) / n) for i in range(n)], dtype=jnp.float32)


def _hyena_filters(L, w1, b1, w2, b2, w3, freq):
    f32 = jnp.float32
    t = jnp.linspace(0.0, 1.0, L, dtype=f32)[:, None]
    w = 2.0 * math.pi * jnp.arange(L, dtype=f32)[:, None] / L
    bands = jnp.linspace(1e-4, FILTER_BANDS - 1, FILTER_BANDS, dtype=f32)[None, :]
    z = jnp.concatenate([t, jnp.cos(bands * w), -jnp.sin(bands * w)], axis=-1)
    fr = freq.astype(f32)
    hid = jnp.sin(fr * (z @ w1.astype(f32) + b1.astype(f32)))
    hid = jnp.sin(fr * (hid @ w2.astype(f32) + b2.astype(f32)))
    h = (hid @ w3.astype(f32)).reshape(L, 2, HYENA_ORDER, HYENA_WIDTH)
    min_decay = math.log(DECAY_TARGET) / SLOW_DECAY_PCT
    max_decay = math.log(DECAY_TARGET) / FAST_DECAY_PCT
    deltas = jnp.abs(jnp.linspace(min_decay, max_decay, HYENA_WIDTH, dtype=f32))
    h = h * jnp.exp(-t[:, :, None, None] * deltas)
    fwd, bwd = h[:, 0], h[:, 1]
    k = jnp.concatenate([fwd, jnp.zeros((1, HYENA_ORDER, HYENA_WIDTH), f32), bwd[:0:-1]], axis=0)
    return k / jnp.sum(jnp.abs(k), axis=0, keepdims=True)


def _fft_conv(u, k_freq, L):
    uf = jnp.fft.rfft(u, n=2 * L, axis=1)
    return jnp.fft.irfft(uf * k_freq[None], n=2 * L, axis=1)[:, :L]


def _hyena_mixer(proj, conv_w, conv_b, w1, b1, w2, b2, w3, freq, skip, norm_g):
    B, L, _ = proj.shape
    pad = SHORT_CONV_WIDTH // 2
    up = jnp.pad(proj, ((0, 0), (pad, pad), (0, 0)))
    u = conv_b
    for j in range(SHORT_CONV_WIDTH):
        u = u + up[:, j:j + L] * conv_w[j]
    x1, x2, v = jnp.split(u, HYENA_ORDER + 1, axis=-1)
    k_freq = jnp.fft.rfft(_hyena_filters(L, w1, b1, w2, b2, w3, freq), axis=0)
    skip = skip.astype(jnp.float32)
    z = v.astype(jnp.float32)
    for n, gate in enumerate((x1, x2)):
        z = gate.astype(jnp.float32) * (_fft_conv(z, k_freq[:, n], L) + skip[n] * z)
    z = z.reshape(B, L, HYENA_GROUPS, HYENA_WIDTH // HYENA_GROUPS)
    z = _rmsnorm(z, norm_g.reshape(HYENA_GROUPS, HYENA_WIDTH // HYENA_GROUPS))
    return z.reshape(B, L, HYENA_WIDTH).astype(proj.dtype)


def _diff_attention(q, k, v, lam, lam_init, subln_g):
    B, L, _ = q.shape
    H, Dh = ATTN_HEADS, ATTN_HEAD_DIM
    q = q.reshape(B, L, H, 2, Dh)
    k = k.reshape(B, L, H, 2, Dh)
    v = v.reshape(B, L, H, 2 * Dh)
    scale = Dh ** -0.5
    slopes = _alibi_slopes(H)
    nb = L // Q_BLOCK
    qb = q.reshape(B, nb, Q_BLOCK, H, 2, Dh).transpose(1, 0, 2, 3, 4, 5)
    starts = jnp.arange(nb, dtype=jnp.int32) * Q_BLOCK
    kpos = jnp.arange(L, dtype=jnp.float32)

    def block(args):
        qblk, s0 = args
        qpos = (s0 + jnp.arange(Q_BLOCK, dtype=jnp.int32)).astype(jnp.float32)
        dist = jnp.abs(qpos[:, None] - kpos[None, :])
        s = jnp.einsum('bqhcd,bkhcd->bhcqk', qblk, k).astype(jnp.float32) * scale
        s = s - slopes[None, :, None, None, None] * dist
        p = jax.nn.softmax(s, axis=-1)
        a = p[:, :, 0] - lam * p[:, :, 1]
        return jnp.einsum('bhqk,bkhe->bqhe', a.astype(v.dtype), v)

    o = lax.map(block, (qb, starts))
    o = o.transpose(1, 0, 2, 3, 4).reshape(B, L, H, 2 * Dh)
    o = _rmsnorm(o, subln_g) * (1.0 - lam_init)
    return o.reshape(B, L, H * 2 * Dh).astype(q.dtype)


def _expert_choice_ffn(h, w_router, w_gate, w_up, w_down):
    B, L, _ = h.shape
    cap = EC_CAPACITY_FACTOR * L // N_EXPERTS
    logits = jnp.einsum('bsd,de->bse', h, w_router).astype(jnp.float32)
    aff = jax.nn.softmax(logits, axis=-1)
    g, idx = lax.top_k(aff.transpose(2, 0, 1), cap)
    bi = jnp.arange(B, dtype=jnp.int32)[None, :, None]
    xin = h[bi, idx]

    def expert(args):
        xe, wg, wu, wd = args
        a = jnp.einsum('bcd,df->bcf', xe, wg)
        u = jnp.einsum('bcd,df->bcf', xe, wu)
        return jnp.einsum('bcf,fd->bcd', jax.nn.silu(a) * u, wd)

    out = lax.map(expert, (xin, w_gate, w_up, w_down))
    out = out * g[..., None].astype(out.dtype)
    return jnp.zeros_like(h).at[bi, idx].add(out)


def setup_inputs(seed: int = 0) -> dict:
    key = jax.random.key(seed)
    ks = jax.random.split(key, 25)
    f32 = jnp.float32

    def nrm(k, shape, scale):
        return jax.random.normal(k, shape, f32) * scale

    D, E, F = D_MODEL, N_EXPERTS, EXPERT_FF
    return {
        'x': nrm(ks[0], (BATCH, SEQ, D), 1.0),
        'attn_norm_g': 1.0 + nrm(ks[1], (DEPTH, D), 0.02),
        'w_in': nrm(ks[2], (DEPTH, D, IN_COLS), D ** -0.5),
        'conv_w': nrm(ks[3], (DEPTH, SHORT_CONV_WIDTH, HYENA_COLS), SHORT_CONV_WIDTH ** -0.5),
        'conv_b': nrm(ks[4], (DEPTH, HYENA_COLS), 0.02),
        'filt_w1': nrm(ks[5], (DEPTH, FILTER_EMB_DIM, FILTER_HIDDEN), FILTER_EMB_DIM ** -0.5),
        'filt_b1': nrm(ks[6], (DEPTH, FILTER_HIDDEN), 0.1),
        'filt_w2': nrm(ks[7], (DEPTH, FILTER_HIDDEN, FILTER_HIDDEN), FILTER_HIDDEN ** -0.5),
        'filt_b2': nrm(ks[8], (DEPTH, FILTER_HIDDEN), 0.1),
        'filt_w3': nrm(ks[9], (DEPTH, FILTER_HIDDEN, 2 * HYENA_ORDER * HYENA_WIDTH), FILTER_HIDDEN ** -0.5),
        'filt_freq': 1.0 + nrm(ks[10], (DEPTH, FILTER_HIDDEN), 0.1),
        'hyena_skip': nrm(ks[11], (DEPTH, HYENA_ORDER, HYENA_WIDTH), 1.0),
        'hyena_norm_g': 1.0 + nrm(ks[12], (DEPTH, HYENA_WIDTH), 0.02),
        'lambda_q1': nrm(ks[13], (DEPTH, ATTN_HEAD_DIM), 0.1),
        'lambda_k1': nrm(ks[14], (DEPTH, ATTN_HEAD_DIM), 0.1),
        'lambda_q2': nrm(ks[15], (DEPTH, ATTN_HEAD_DIM), 0.1),
        'lambda_k2': nrm(ks[16], (DEPTH, ATTN_HEAD_DIM), 0.1),
        'subln_g': 1.0 + nrm(ks[17], (DEPTH, 2 * ATTN_HEAD_DIM), 0.02),
        'w_out': nrm(ks[18], (DEPTH, MIX_WIDTH, D), MIX_WIDTH ** -0.5),
        'ffn_norm_g': 1.0 + nrm(ks[19], (DEPTH, D), 0.02),
        'w_router': nrm(ks[20], (DEPTH, D, E), D ** -0.5),
        'w_gate': nrm(ks[21], (DEPTH, E, D, F), D ** -0.5),
        'w_up': nrm(ks[22], (DEPTH, E, D, F), D ** -0.5),
        'w_down': nrm(ks[23], (DEPTH, E, F, D), F ** -0.5),
        'final_norm_g': 1.0 + nrm(ks[24], (D,), 0.02),
    }


def reference(x, attn_norm_g, w_in, conv_w, conv_b, filt_w1, filt_b1, filt_w2, filt_b2,
              filt_w3, filt_freq, hyena_skip, hyena_norm_g, lambda_q1, lambda_k1,
              lambda_q2, lambda_k2, subln_g, w_out, ffn_norm_g, w_router, w_gate, w_up,
              w_down, final_norm_g):
    f32 = jnp.float32
    for l in range(DEPTH):
        hn = _rmsnorm(x, attn_norm_g[l])
        proj = jnp.einsum('bsd,dn->bsn', hn, w_in[l])
        hy = _hyena_mixer(proj[..., :HYENA_COLS], conv_w[l], conv_b[l], filt_w1[l], filt_b1[l],
                          filt_w2[l], filt_b2[l], filt_w3[l], filt_freq[l], hyena_skip[l],
                          hyena_norm_g[l])
        q, k, v = jnp.split(proj[..., HYENA_COLS:], 3, axis=-1)
        lam_init = 0.8 - 0.6 * math.exp(-0.3 * l)
        lam = (jnp.exp(jnp.sum(lambda_q1[l].astype(f32) * lambda_k1[l].astype(f32)))
               - jnp.exp(jnp.sum(lambda_q2[l].astype(f32) * lambda_k2[l].astype(f32)))
               + lam_init)
        at = _diff_attention(q, k, v, lam, lam_init, subln_g[l])
        mix = jnp.concatenate([hy, at], axis=-1)
        x = x + jnp.einsum('bsm,md->bsd', mix, w_out[l])
        x = x + _expert_choice_ffn(_rmsnorm(x, ffn_norm_g[l]), w_router[l], w_gate[l],
                                   w_up[l], w_down[l])
    return _rmsnorm(x, final_norm_g)
```

```python
import functools
import math

import jax
import jax.numpy as jnp
import numpy as np
from jax import lax
from jax.experimental import pallas as pl
from jax.experimental.pallas import tpu as pltpu

F32 = jnp.float32
BF16 = jnp.bfloat16
HIGHEST = lax.Precision.HIGHEST

NORM_EPS = 1e-6
HYENA_ORDER = 2
HYENA_GROUPS = 8
ATTN_HEADS = 4
N_EXPERTS = 16
EC_CAPACITY_FACTOR = 2
FILTER_EMB_DIM = 33
FILTER_EMB_PAD = 40
DECAY_TARGET = 1e-2
FAST_DECAY_PCT = 0.3
SLOW_DECAY_PCT = 1.5

LANES = 128
FFT_FAST = 128
V7X_VMEM_LIMIT = 56 * 1024 * 1024


def _cparams(sem, vmem=V7X_VMEM_LIMIT):
    return pltpu.CompilerParams(dimension_semantics=sem, vmem_limit_bytes=vmem)


def _dft_constants(L):
    n2 = FFT_FAST
    n = 2 * L
    n1 = n // n2
    s = n1 // 2
    f = np.arange(n2)[:, None, None]
    k1 = np.arange(n1)[None, :, None]
    sv = np.arange(s)[None, None, :]
    th = 2.0 * np.pi * (f * k1 / n + sv * k1 / n1)
    fwd1 = np.empty((n2, 2 * n1, s), np.float64)
    fwd1[:, 0::2, :] = np.cos(th)
    fwd1[:, 1::2, :] = -np.sin(th)
    inv1 = np.empty((n2, s, 2 * n1), np.float64)
    tht = np.transpose(th, (0, 2, 1))
    inv1[:, :, 0::2] = np.cos(tht)
    inv1[:, :, 1::2] = -np.sin(tht)
    a = np.arange(n2)
    ph = 2.0 * np.pi * np.outer(a, a) / n2
    fr, fi = np.cos(ph), -np.sin(ph)
    fwd2 = np.block([[fr, -fi], [fi, fr]])
    inv2 = np.block([[fr, fi], [-fi, fr]]) / n
    return (fwd1.astype(np.float32), fwd2.astype(np.float32),
            inv2.astype(np.float32), inv1.astype(np.float32))


def _filter_kernel(zf_ref, w1_ref, b1_ref, w2_ref, b2_ref, fq_ref, w3f_ref, w3b_ref, dl_ref,
                   f1_ref, f2_ref, o_ref, hf_ref, hb_ref, sf_ref, sb_ref, *, L):
    n1 = 2 * L // FFT_FAST
    s = n1 // 2
    fq = fq_ref[...]
    hid = jnp.sin(fq * (jnp.dot(zf_ref[...], w1_ref[...], precision=HIGHEST,
                                preferred_element_type=F32) + b1_ref[...]))
    hid = jnp.sin(fq * (jnp.dot(hid, w2_ref[...], precision=HIGHEST,
                                preferred_element_type=F32) + b2_ref[...]))
    row = lax.broadcasted_iota(jnp.int32, (L, 1), 0)
    t = row.astype(F32) * (1.0 / (L - 1))
    win = jnp.exp(-t * dl_ref[...])
    hf = jnp.dot(hid, w3f_ref[...], precision=HIGHEST, preferred_element_type=F32) * win
    hb = jnp.dot(hid, w3b_ref[...], precision=HIGHEST, preferred_element_type=F32) * win
    hb = jnp.where(row == 0, 0.0, hb)
    norm = (jnp.sum(jnp.abs(hf), axis=0, keepdims=True)
            + jnp.sum(jnp.abs(hb), axis=0, keepdims=True))
    hf_ref[...] = hf
    hb_ref[...] = hb

    def stage1(f, c):
        m = f1_ref[f]
        sf_ref[pl.ds(f, 2 * n1, stride=FFT_FAST), :] = jnp.dot(
            m, hf_ref[pl.ds(f, s, stride=FFT_FAST), :], precision=HIGHEST, preferred_element_type=F32)
        sb_ref[pl.ds(f, 2 * n1, stride=FFT_FAST), :] = jnp.dot(
            m, hb_ref[pl.ds(f, s, stride=FFT_FAST), :], precision=HIGHEST, preferred_element_type=F32)
        return c

    lax.fori_loop(0, FFT_FAST, stage1, 0)
    inv_norm = 1.0 / norm

    def stage2(k1, c):
        r0 = pl.multiple_of(k1 * 2 * FFT_FAST, 2 * FFT_FAST)
        xf = jnp.dot(f2_ref[...], sf_ref[pl.ds(r0, 2 * FFT_FAST), :], precision=HIGHEST,
                     preferred_element_type=F32)
        xb = jnp.dot(f2_ref[...], sb_ref[pl.ds(r0, 2 * FFT_FAST), :], precision=HIGHEST,
                     preferred_element_type=F32)
        re = (xf[:FFT_FAST] + xb[:FFT_FAST]) * inv_norm
        im = (xf[FFT_FAST:] - xb[FFT_FAST:]) * inv_norm
        o_ref[0, pl.ds(r0, 2 * FFT_FAST), :] = jnp.concatenate([re, im], axis=0).astype(o_ref.dtype)
        return c

    lax.fori_loop(0, n1, stage2, 0)


def _hyena_filter_spectra(L, width, w1, b1, w2, b2, w3, freq, consts):
    fwd1, fwd2, _, _ = consts
    n1 = 2 * L // FFT_FAST
    cb = LANES
    ncb = width // cb
    hidden = w1.shape[1]
    tt = np.linspace(0.0, 1.0, L)[:, None]
    ww = 2.0 * np.pi * np.arange(L)[:, None] / L
    nb = (FILTER_EMB_DIM - 1) // 2
    bands = np.linspace(1e-4, nb - 1, nb)[None, :]
    feats = np.concatenate([tt, np.cos(bands * ww), -np.sin(bands * ww)], axis=-1)
    feats = np.pad(feats, ((0, 0), (0, FILTER_EMB_PAD - FILTER_EMB_DIM))).astype(np.float32)
    w1p = jnp.pad(w1.astype(F32), ((0, FILTER_EMB_PAD - FILTER_EMB_DIM), (0, 0)))
    min_decay = math.log(DECAY_TARGET) / SLOW_DECAY_PCT
    max_decay = math.log(DECAY_TARGET) / FAST_DECAY_PCT
    deltas = np.abs(np.linspace(min_decay, max_decay, width)).astype(np.float32)[None, :]

    full = lambda shape: pl.BlockSpec(shape, lambda n, c: (0,) * len(shape))
    return pl.pallas_call(
        functools.partial(_filter_kernel, L=L),
        grid=(HYENA_ORDER, ncb),
        in_specs=[
            full((L, FILTER_EMB_PAD)), full((FILTER_EMB_PAD, hidden)), full((1, hidden)),
            full((hidden, hidden)), full((1, hidden)), full((1, hidden)),
            pl.BlockSpec((hidden, cb), lambda n, c: (0, n * ncb + c)),
            pl.BlockSpec((hidden, cb), lambda n, c: (0, HYENA_ORDER * ncb + n * ncb + c)),
            pl.BlockSpec((1, cb), lambda n, c: (0, c)),
            full(fwd1.shape), full(fwd2.shape),
        ],
        out_specs=pl.BlockSpec((1, n1 * 2 * FFT_FAST, cb), lambda n, c: (n, 0, c)),
        out_shape=jax.ShapeDtypeStruct((HYENA_ORDER, n1 * 2 * FFT_FAST, width), BF16),
        scratch_shapes=[pltpu.VMEM((L, cb), F32), pltpu.VMEM((L, cb), F32),
                        pltpu.VMEM((2 * n1 * FFT_FAST, cb), F32), pltpu.VMEM((2 * n1 * FFT_FAST, cb), F32)],
        compiler_params=_cparams(("arbitrary", "arbitrary")),
        name="hyena_filter_spectra",
    )(jnp.asarray(feats), w1p, b1.astype(F32)[None, :], w2.astype(F32), b2.astype(F32)[None, :],
      freq.astype(F32)[None, :], w3.astype(F32), w3.astype(F32), jnp.asarray(deltas),
      jnp.asarray(fwd1), jnp.asarray(fwd2))


def _short_conv(p, w, b, L):
    row = lax.broadcasted_iota(jnp.int32, p.shape, 0)
    prev = jnp.where(row == 0, 0.0, pltpu.roll(p, 1, 0))
    nxt = jnp.where(row == L - 1, 0.0, pltpu.roll(p, L - 1, 0))
    return b + prev * w[0:1] + p * w[1:2] + nxt * w[2:3]


def _hyena_kernel(x1_ref, x2_ref, v_ref, w1_ref, w2_ref, wv_ref, b1_ref, b2_ref, bv_ref,
                  skip_ref, ng_ref, kf_ref, f1_ref, f2_ref, f2i_ref, g1_ref, gm_ref,
                  o_ref, z_ref, ga_ref, gb_ref, s_ref, *, L):
    n1 = 2 * L // FFT_FAST
    s = n1 // 2
    z_ref[...] = _short_conv(v_ref[0], wv_ref[...], bv_ref[...], L)
    ga_ref[...] = _short_conv(x1_ref[0], w1_ref[...], b1_ref[...], L)
    gb_ref[...] = _short_conv(x2_ref[0], w2_ref[...], b2_ref[...], L)

    for n, gate_ref in enumerate((ga_ref, gb_ref)):
        def fwd1(f, c):
            xf = z_ref[pl.ds(f, s, stride=FFT_FAST), :].astype(BF16)
            s_ref[pl.ds(f, 2 * n1, stride=FFT_FAST), :] = jnp.dot(
                f1_ref[f], xf, preferred_element_type=F32)
            return c

        lax.fori_loop(0, FFT_FAST, fwd1, 0)

        def mid(k1, c):
            r0 = pl.multiple_of(k1 * 2 * FFT_FAST, 2 * FFT_FAST)
            x = jnp.dot(f2_ref[...], s_ref[pl.ds(r0, 2 * FFT_FAST), :].astype(BF16),
                        preferred_element_type=F32)
            kk = kf_ref[n, pl.ds(r0, 2 * FFT_FAST), :].astype(F32)
            xr, xi = x[:FFT_FAST], x[FFT_FAST:]
            kr, ki = kk[:FFT_FAST], kk[FFT_FAST:]
            y = jnp.concatenate([xr * kr - xi * ki, xr * ki + xi * kr], axis=0).astype(BF16)
            s_ref[pl.ds(r0, 2 * FFT_FAST), :] = jnp.dot(f2i_ref[...], y, preferred_element_type=F32)
            return c

        lax.fori_loop(0, n1, mid, 0)

        def inv1(f, c):
            bf = s_ref[pl.ds(f, 2 * n1, stride=FFT_FAST), :].astype(BF16)
            y = jnp.dot(g1_ref[f], bf, preferred_element_type=F32)
            zf = z_ref[pl.ds(f, s, stride=FFT_FAST), :]
            gf = gate_ref[pl.ds(f, s, stride=FFT_FAST), :]
            z_ref[pl.ds(f, s, stride=FFT_FAST), :] = gf * (y + skip_ref[n:n + 1, :] * zf)
            return c

        lax.fori_loop(0, FFT_FAST, inv1, 0)

    z = z_ref[...]
    zz = z * z
    hi = zz.astype(BF16)
    lo = (zz - hi.astype(F32)).astype(BF16)
    ms = (jnp.dot(hi, gm_ref[...], preferred_element_type=F32)
          + jnp.dot(lo, gm_ref[...], preferred_element_type=F32))
    o_ref[0] = (z * lax.rsqrt(ms + NORM_EPS) * ng_ref[...]).astype(o_ref.dtype)


def _hyena_mixer(proj_h, conv_w, conv_b, kf, skip, norm_g, consts):
    B, L, cols = proj_h.shape
    width = cols // (HYENA_ORDER + 1)
    cb = LANES
    ncb = width // cb
    gsz = width // HYENA_GROUPS
    n1 = 2 * L // FFT_FAST
    fwd1, fwd2, inv2, inv1 = consts
    gm = np.kron(np.eye(cb // gsz), np.full((gsz, gsz), 1.0 / gsz)).astype(np.float32)

    def stream(i):
        return pl.BlockSpec((1, L, cb), lambda c, b: (b, 0, i * ncb + c))

    def cvec(rows, i):
        return pl.BlockSpec((rows, cb), lambda c, b: (0, i * ncb + c))

    def const(shape):
        return pl.BlockSpec(shape, lambda c, b: (0,) * len(shape), pipeline_mode=pl.Buffered(1))

    cw = conv_w.astype(F32)
    cbias = conv_b.astype(F32)[None, :]
    return pl.pallas_call(
        functools.partial(_hyena_kernel, L=L),
        grid=(ncb, B),
        in_specs=[
            stream(0), stream(1), stream(2),
            cvec(3, 0), cvec(3, 1), cvec(3, 2),
            cvec(1, 0), cvec(1, 1), cvec(1, 2),
            pl.BlockSpec((HYENA_ORDER, cb), lambda c, b: (0, c)),
            pl.BlockSpec((1, cb), lambda c, b: (0, c)),
            pl.BlockSpec((HYENA_ORDER, n1 * 2 * FFT_FAST, cb), lambda c, b: (0, 0, c),
                         pipeline_mode=pl.Buffered(1)),
            const(fwd1.shape), const(fwd2.shape), const(inv2.shape), const(inv1.shape),
            const(gm.shape),
        ],
        out_specs=pl.BlockSpec((1, L, cb), lambda c, b: (b, 0, c)),
        out_shape=jax.ShapeDtypeStruct((B, L, width), BF16),
        scratch_shapes=[pltpu.VMEM((L, cb), F32), pltpu.VMEM((L, cb), F32), pltpu.VMEM((L, cb), F32),
                        pltpu.VMEM((2 * n1 * FFT_FAST, cb), F32)],
        compiler_params=_cparams(("arbitrary", "arbitrary")),
        name="hyena_mixer",
    )(proj_h, proj_h, proj_h, cw, cw, cw, cbias, cbias, cbias,
      skip.astype(F32), norm_g.astype(F32)[None, :], kf,
      jnp.asarray(fwd1).astype(BF16), jnp.asarray(fwd2).astype(BF16),
      jnp.asarray(inv2).astype(BF16), jnp.asarray(inv1).astype(BF16), jnp.asarray(gm).astype(BF16))


def _rms(x, g):
    return x * lax.rsqrt(jnp.mean(x * x, axis=-1, keepdims=True) + NORM_EPS) * g


def _in_proj_kernel(x_ref, g_ref, wh_ref, wa_ref, oh_ref, oa_ref):
    hn = _rms(x_ref[...], g_ref[...]).astype(BF16)
    oh_ref[...] = jnp.dot(hn, wh_ref[...], preferred_element_type=F32).astype(oh_ref.dtype)
    oa_ref[...] = jnp.dot(hn, wa_ref[...], preferred_element_type=F32).astype(oa_ref.dtype)


def _in_proj(x2d, g, w_in, hyena_cols, tm):
    T, D = x2d.shape
    attn_cols = w_in.shape[1] - hyena_cols
    wh = w_in[:, :hyena_cols].astype(BF16)
    wa = w_in[:, hyena_cols:].astype(BF16)
    return pl.pallas_call(
        _in_proj_kernel,
        grid=(T // tm,),
        in_specs=[
            pl.BlockSpec((tm, D), lambda i: (i, 0)),
            pl.BlockSpec((1, D), lambda i: (0, 0)),
            pl.BlockSpec((D, hyena_cols), lambda i: (0, 0)),
            pl.BlockSpec((D, attn_cols), lambda i: (0, 0)),
        ],
        out_specs=[pl.BlockSpec((tm, hyena_cols), lambda i: (i, 0)),
                   pl.BlockSpec((tm, attn_cols), lambda i: (i, 0))],
        out_shape=[jax.ShapeDtypeStruct((T, hyena_cols), F32),
                   jax.ShapeDtypeStruct((T, attn_cols), BF16)],
        compiler_params=_cparams(("arbitrary",)),
        name="in_proj",
    )(x2d, g.astype(F32)[None, :], wh, wa)


def _attn_kernel(slope_ref, q_ref, k_ref, v_ref, lq1_ref, lk1_ref, lq2_ref, lk2_ref, sg_ref, o_ref,
                 *, L, tq, tk, dh, lam_init):
    h = pl.program_id(1)
    qi = pl.program_id(2)
    slope = slope_ref[h]
    q = q_ref[0]
    lane = lax.broadcasted_iota(jnp.int32, q.shape, 1)
    qs = q * jnp.asarray(dh ** -0.5, q.dtype)
    q1 = jnp.where(lane < dh, qs, jnp.zeros_like(qs))
    q2 = jnp.where(lane >= dh, qs, jnp.zeros_like(qs))
    rel = (lax.broadcasted_iota(jnp.int32, (tq, tk), 0)
           - lax.broadcasted_iota(jnp.int32, (tq, tk), 1)).astype(F32)
    q0 = (qi * tq).astype(F32)

    def update(s, m, l, acc, vb):
        m_new = jnp.maximum(m, jnp.max(s, axis=-1, keepdims=True))
        alpha = jnp.exp(m - m_new)
        p = jnp.exp(s - m_new)
        l = alpha * l + jnp.sum(p, axis=-1, keepdims=True)
        acc = alpha * acc + jnp.dot(p.astype(BF16), vb, preferred_element_type=F32)
        return m_new, l, acc

    def body(j, carry):
        m1, l1, a1, m2, l2, a2 = carry
        k0 = pl.multiple_of(j * tk, tk)
        kb = k_ref[0, pl.ds(k0, tk), :]
        vb = v_ref[0, pl.ds(k0, tk), :]
        pen = slope * jnp.abs(rel + (q0 - k0.astype(F32)))
        nt = (((1,), (1,)), ((), ()))
        s1 = lax.dot_general(q1, kb, nt, preferred_element_type=F32) - pen
        s2 = lax.dot_general(q2, kb, nt, preferred_element_type=F32) - pen
        m1, l1, a1 = update(s1, m1, l1, a1, vb)
        m2, l2, a2 = update(s2, m2, l2, a2, vb)
        return m1, l1, a1, m2, l2, a2

    neg = jnp.full((tq, 1), -1e30, F32)
    zero = jnp.zeros((tq, 1), F32)
    zacc = jnp.zeros((tq, 2 * dh), F32)
    m1, l1, a1, m2, l2, a2 = lax.fori_loop(0, L // tk, body, (neg, zero, zacc, neg, zero, zacc))
    lam = (jnp.exp(jnp.sum(lq1_ref[...] * lk1_ref[...], axis=-1, keepdims=True))
           - jnp.exp(jnp.sum(lq2_ref[...] * lk2_ref[...], axis=-1, keepdims=True)) + lam_init)
    o = a1 / l1 - lam * (a2 / l2)
    o = _rms(o, sg_ref[...]) * (1.0 - lam_init)
    o_ref[0] = o.astype(o_ref.dtype)


def _diff_attention(qkv, lq1, lk1, lq2, lk2, subln_g, lam_init, tq, tk):
    B, L, cols = qkv.shape
    W = cols // 3
    H = ATTN_HEADS
    hw = W // H
    dh = hw // 2
    slopes = jnp.asarray([2.0 ** (-8.0 * (i + 1) / H) for i in range(H)], F32)
    vec = lambda a: a.astype(F32)[None, :]
    small = lambda n: pl.BlockSpec((1, n), lambda b, h, i: (0, 0))
    return pl.pallas_call(
        functools.partial(_attn_kernel, L=L, tq=tq, tk=tk, dh=dh, lam_init=lam_init),
        grid=(B, H, L // tq),
        in_specs=[
            pl.BlockSpec(memory_space=pltpu.SMEM),
            pl.BlockSpec((1, tq, hw), lambda b, h, i: (b, i, h)),
            pl.BlockSpec((1, L, hw), lambda b, h, i: (b, 0, H + h)),
            pl.BlockSpec((1, L, hw), lambda b, h, i: (b, 0, 2 * H + h)),
            small(dh), small(dh), small(dh), small(dh), small(hw),
        ],
        out_specs=pl.BlockSpec((1, tq, hw), lambda b, h, i: (b, i, h)),
        out_shape=jax.ShapeDtypeStruct((B, L, W), BF16),
        compiler_params=_cparams(("arbitrary", "arbitrary", "arbitrary")),
        name="diff_attention",
    )(slopes, qkv, qkv, qkv, vec(lq1), vec(lk1), vec(lq2), vec(lk2), vec(subln_g))


def _out_proj_kernel(x_ref, hy_ref, at_ref, wh_ref, wa_ref, g_ref, wr_ref, x2_ref, h2_ref, aff_ref):
    x2 = (x_ref[...] + jnp.dot(hy_ref[...], wh_ref[...], preferred_element_type=F32)
          + jnp.dot(at_ref[...], wa_ref[...], preferred_element_type=F32))
    x2_ref[...] = x2
    h2 = _rms(x2, g_ref[...])
    h2_ref[...] = h2
    logits = jnp.dot(h2, wr_ref[...], precision=HIGHEST, preferred_element_type=F32)
    e = jnp.exp(logits - jnp.max(logits, axis=-1, keepdims=True))
    aff_ref[...] = e / jnp.sum(e, axis=-1, keepdims=True)


def _out_proj(x2d, hy2d, at2d, w_out, g, w_router, tm):
    T, D = x2d.shape
    wdt = hy2d.shape[1]
    E = w_router.shape[1]
    wh = w_out[:wdt].astype(BF16)
    wa = w_out[wdt:].astype(BF16)
    row = lambda n: pl.BlockSpec((tm, n), lambda i: (i, 0))
    return pl.pallas_call(
        _out_proj_kernel,
        grid=(T // tm,),
        in_specs=[row(D), row(wdt), row(at2d.shape[1]),
                  pl.BlockSpec(wh.shape, lambda i: (0, 0)), pl.BlockSpec(wa.shape, lambda i: (0, 0)),
                  pl.BlockSpec((1, D), lambda i: (0, 0)), pl.BlockSpec((D, E), lambda i: (0, 0))],
        out_specs=[row(D), row(D), row(E)],
        out_shape=[jax.ShapeDtypeStruct((T, D), F32), jax.ShapeDtypeStruct((T, D), F32),
                   jax.ShapeDtypeStruct((T, E), F32)],
        compiler_params=_cparams(("arbitrary",)),
        name="out_proj_router",
    )(x2d, hy2d, at2d, wh, wa, g.astype(F32)[None, :], w_router.astype(F32))


SEL_CHUNK = 256


def _prefix_count(mask, tri_ref, L):
    parts = []
    carry = jnp.zeros((1, mask.shape[1]), F32)
    for blk in range(L // SEL_CHUNK):
        m = mask[blk * SEL_CHUNK:(blk + 1) * SEL_CHUNK]
        parts.append(jnp.dot(tri_ref[...], m.astype(BF16), preferred_element_type=F32) + carry)
        carry = carry + jnp.sum(m, axis=0, keepdims=True)
    return jnp.concatenate(parts, axis=0)


def _select_kernel(aff_ref, tri_ref, idx_ref, gate_ref, *, L, E, cap):
    aff = aff_ref[0]

    def step(i, prefix):
        cand = prefix | jnp.left_shift(jnp.int32(1), 30 - i)
        cnt = jnp.sum((aff >= pltpu.bitcast(cand, F32)).astype(jnp.int32), axis=0, keepdims=True)
        return jnp.where(cnt >= cap, cand, prefix)

    tau = pltpu.bitcast(lax.fori_loop(0, 31, step, jnp.zeros((1, E), jnp.int32)), F32)
    gt = aff > tau
    eq = aff == tau
    need = cap - jnp.sum(gt.astype(jnp.int32), axis=0, keepdims=True)
    tie_rank = _prefix_count(eq.astype(F32), tri_ref, L) - eq.astype(F32)
    sel = gt | (eq & (tie_rank < need.astype(F32)))
    rank = _prefix_count(sel.astype(F32), tri_ref, L)
    slot = jnp.where(sel, rank - 1.0, -1.0)
    tok = lax.broadcasted_iota(jnp.int32, (L, cap), 0).astype(F32)
    jj = lax.broadcasted_iota(jnp.int32, (L, cap), 1).astype(F32)
    for e in range(E):
        hit = slot[:, e:e + 1] == jj
        idx_ref[0, e:e + 1, :] = jnp.sum(jnp.where(hit, tok, 0.0), axis=0, keepdims=True).astype(jnp.int32)
        gate_ref[0, e:e + 1, :] = jnp.sum(jnp.where(hit, aff[:, e:e + 1], 0.0), axis=0, keepdims=True)


def _select(aff, cap):
    B, L, E = aff.shape
    tri = np.tril(np.ones((SEL_CHUNK, SEL_CHUNK), np.float32))
    return pl.pallas_call(
        functools.partial(_select_kernel, L=L, E=E, cap=cap),
        grid=(B,),
        in_specs=[pl.BlockSpec((1, L, E), lambda b: (b, 0, 0)),
                  pl.BlockSpec(tri.shape, lambda b: (0, 0))],
        out_specs=[pl.BlockSpec((1, E, cap), lambda b: (b, 0, 0)),
                   pl.BlockSpec((1, E, cap), lambda b: (b, 0, 0))],
        out_shape=[jax.ShapeDtypeStruct((B, E, cap), jnp.int32),
                   jax.ShapeDtypeStruct((B, E, cap), F32)],
        compiler_params=_cparams(("arbitrary",)),
        name="expert_choice_select",
    )(aff, jnp.asarray(tri).astype(BF16))


ROW_UNROLL = 8


def _gather_kernel(idx_ref, h_ref, o_ref, buf_ref, *, E, cap):
    b = pl.program_id(0)
    e = pl.program_id(1)
    base = (b * E + e) * cap

    def body(j, c):
        t = idx_ref[base + j]
        buf_ref[pl.ds(j, 1), :] = h_ref[0, pl.ds(t, 1), :]
        return c

    lax.fori_loop(0, cap, body, 0, unroll=ROW_UNROLL)
    o_ref[0, 0] = buf_ref[...].astype(o_ref.dtype)


def _gather(idx_flat, h2, E, cap):
    B, L, D = h2.shape
    return pl.pallas_call(
        functools.partial(_gather_kernel, E=E, cap=cap),
        grid_spec=pltpu.PrefetchScalarGridSpec(
            num_scalar_prefetch=1,
            grid=(B, E),
            in_specs=[pl.BlockSpec((1, L, D), lambda b, e, idx: (b, 0, 0))],
            out_specs=pl.BlockSpec((1, 1, cap, D), lambda b, e, idx: (e, b, 0, 0)),
            scratch_shapes=[pltpu.VMEM((cap, D), F32)],
        ),
        out_shape=jax.ShapeDtypeStruct((E, B, cap, D), BF16),
        compiler_params=_cparams(("arbitrary", "arbitrary")),
        name="expert_gather",
    )(idx_flat, h2)


def _ffn_kernel(x_ref, wg_ref, wu_ref, wd_ref, o_ref):
    f = pl.program_id(2)
    x = x_ref[0]
    a = jnp.dot(x, wg_ref[0].astype(BF16), preferred_element_type=F32)
    u = jnp.dot(x, wu_ref[0].astype(BF16), preferred_element_type=F32)
    hmid = (a * jax.nn.sigmoid(a) * u).astype(BF16)
    y = jnp.dot(hmid, wd_ref[0].astype(BF16), preferred_element_type=F32)

    @pl.when(f == 0)
    def _():
        o_ref[0] = y

    @pl.when(f > 0)
    def _():
        o_ref[0] += y


def _expert_ffn(xe, w_gate, w_up, w_down, tt, tf):
    E, T, D = xe.shape
    F = w_gate.shape[2]
    return pl.pallas_call(
        _ffn_kernel,
        grid=(E, T // tt, F // tf),
        in_specs=[
            pl.BlockSpec((1, tt, D), lambda e, i, f: (e, i, 0)),
            pl.BlockSpec((1, D, tf), lambda e, i, f: (e, 0, f)),
            pl.BlockSpec((1, D, tf), lambda e, i, f: (e, 0, f)),
            pl.BlockSpec((1, tf, D), lambda e, i, f: (e, f, 0)),
        ],
        out_specs=pl.BlockSpec((1, tt, D), lambda e, i, f: (e, i, 0)),
        out_shape=jax.ShapeDtypeStruct((E, T, D), F32),
        compiler_params=_cparams(("arbitrary", "arbitrary", "arbitrary")),
        name="expert_ffn",
    )(xe, w_gate, w_up, w_down)


def _scatter_kernel(idx_ref, gate_ref, y_ref, o_ref, *, E, cap):
    b = pl.program_id(0)
    e = pl.program_id(2)
    base = (b * E + e) * cap

    @pl.when(e == 0)
    def _():
        o_ref[...] = jnp.zeros_like(o_ref)

    def body(j, c):
        t = idx_ref[base + j]
        g = gate_ref[base + j]
        o_ref[0, pl.ds(t, 1), :] += g * y_ref[0, 0, pl.ds(j, 1), :]
        return c

    lax.fori_loop(0, cap, body, 0, unroll=ROW_UNROLL)


def _scatter(idx_flat, gate_flat, ye, L, dsplit):
    E, B, cap, D = ye.shape
    dc = D // dsplit
    return pl.pallas_call(
        functools.partial(_scatter_kernel, E=E, cap=cap),
        grid_spec=pltpu.PrefetchScalarGridSpec(
            num_scalar_prefetch=2,
            grid=(B, dsplit, E),
            in_specs=[pl.BlockSpec((1, 1, cap, dc), lambda b, d, e, idx, gt: (e, b, 0, d))],
            out_specs=pl.BlockSpec((1, L, dc), lambda b, d, e, idx, gt: (b, 0, d)),
        ),
        out_shape=jax.ShapeDtypeStruct((B, L, D), F32),
        compiler_params=_cparams(("arbitrary", "arbitrary", "arbitrary")),
        name="expert_scatter_add",
    )(idx_flat, gate_flat, ye)


def _final_kernel(x_ref, m_ref, g_ref, o_ref):
    o_ref[...] = _rms(x_ref[...] + m_ref[...], g_ref[...])


def _final_norm(x2, moe, g, tm):
    T, D = x2.shape
    row = pl.BlockSpec((tm, D), lambda i: (i, 0))
    return pl.pallas_call(
        _final_kernel,
        grid=(T // tm,),
        in_specs=[row, row, pl.BlockSpec((1, D), lambda i: (0, 0))],
        out_specs=row,
        out_shape=jax.ShapeDtypeStruct((T, D), F32),
        compiler_params=_cparams(("arbitrary",)),
        name="final_norm",
    )(x2, moe, g.astype(F32)[None, :])


def _moe(h2, aff, w_gate, w_up, w_down, tt, tf):
    B, L, D = h2.shape
    E = aff.shape[-1]
    cap = EC_CAPACITY_FACTOR * L // E
    idx, gate = _select(aff, cap)
    idx_flat = idx.reshape(-1)
    xe = _gather(idx_flat, h2, E, cap)
    ye = _expert_ffn(xe.reshape(E, B * cap, D), w_gate, w_up, w_down, tt, tf)
    return _scatter(idx_flat, gate.reshape(-1), ye.reshape(E, B, cap, D), L, dsplit=2)


def kernel(x, attn_norm_g, w_in, conv_w, conv_b, filt_w1, filt_b1, filt_w2, filt_b2, filt_w3, filt_freq, hyena_skip, hyena_norm_g, lambda_q1, lambda_k1, lambda_q2, lambda_k2, subln_g, w_out, ffn_norm_g, w_router, w_gate, w_up, w_down, final_norm_g):
    B, L, D = x.shape
    depth = w_in.shape[0]
    width = hyena_skip.shape[-1]
    hyena_cols = (HYENA_ORDER + 1) * width
    E = w_router.shape[-1]
    tm = min(512, B * L)
    tq = min(512, L)
    consts = _dft_constants(L)
    x2d = x.reshape(B * L, D)
    for l in range(depth):
        lam_init = 0.8 - 0.6 * math.exp(-0.3 * l)
        ph, qkv = _in_proj(x2d, attn_norm_g[l], w_in[l], hyena_cols, tm)
        kf = _hyena_filter_spectra(L, width, filt_w1[l], filt_b1[l], filt_w2[l], filt_b2[l],
                                   filt_w3[l], filt_freq[l], consts)
        hy = _hyena_mixer(ph.reshape(B, L, hyena_cols), conv_w[l], conv_b[l], kf, hyena_skip[l],
                          hyena_norm_g[l], consts)
        at = _diff_attention(qkv.reshape(B, L, -1), lambda_q1[l], lambda_k1[l], lambda_q2[l],
                             lambda_k2[l], subln_g[l], lam_init, tq, tq)
        x2, h2, aff = _out_proj(x2d, hy.reshape(B * L, width), at.reshape(B * L, -1), w_out[l],
                                ffn_norm_g[l], w_router[l], tm)
        cap = EC_CAPACITY_FACTOR * L // E
        moe = _moe(h2.reshape(B, L, D), aff.reshape(B, L, E), w_gate[l], w_up[l], w_down[l],
                   tt=min(2048, B * cap), tf=256)
        if l + 1 < depth:
            x2d = x2 + moe.reshape(B * L, D)
    return _final_norm(x2, moe.reshape(B * L, D), final_norm_g, tm).reshape(B, L, D)
```

```python
import functools
import math

import jax
import jax.numpy as jnp
import numpy as np
from jax import lax
from jax.experimental import pallas as pl
from jax.experimental.pallas import tpu as pltpu

F32 = jnp.float32
BF16 = jnp.bfloat16
HIGHEST = lax.Precision.HIGHEST

NORM_EPS = 1e-6
HYENA_ORDER = 2
HYENA_GROUPS = 8
ATTN_HEADS = 4
N_EXPERTS = 16
EC_CAPACITY_FACTOR = 2
FILTER_EMB_DIM = 33
FILTER_EMB_PAD = 40
DECAY_TARGET = 1e-2
FAST_DECAY_PCT = 0.3
SLOW_DECAY_PCT = 1.5

LANES = 128
FFT_FAST = 128
V7X_VMEM_LIMIT = 56 * 1024 * 1024


def _cparams(sem, vmem=V7X_VMEM_LIMIT):
    return pltpu.CompilerParams(dimension_semantics=sem, vmem_limit_bytes=vmem)


SUBLANES = 8


def _dft_constants(L):
    n2 = FFT_FAST
    n = 2 * L
    n1 = n // n2
    s = n1 // 2
    k1 = np.arange(n1)
    th1 = 2.0 * np.pi * np.outer(k1, np.arange(s)) / n1
    f1 = np.empty((2 * n1, s), np.float64)
    f1[0::2], f1[1::2] = np.cos(th1), -np.sin(th1)
    g1 = np.empty((s, 2 * n1), np.float64)
    g1[:, 0::2], g1[:, 1::2] = np.cos(th1.T), -np.sin(th1.T)
    eye = np.eye(SUBLANES)
    fwd1 = np.kron(f1, eye)
    inv1 = np.kron(g1, eye)
    a = np.arange(n2)
    ph = 2.0 * np.pi * (np.outer(a, a)[None] / n2 + (k1[:, None, None] * a[None, None, :]) / n)
    tr, ti = np.cos(ph), -np.sin(ph)
    fwd2 = np.concatenate([np.concatenate([tr, -ti], 2), np.concatenate([ti, tr], 2)], 1)
    pht = np.transpose(ph, (0, 2, 1))
    ur, ui = np.cos(pht) / n, np.sin(pht) / n
    inv2 = np.concatenate([np.concatenate([ur, -ui], 2), np.concatenate([ui, ur], 2)], 1)
    return tuple(jnp.asarray(m.astype(np.float32)).astype(BF16) for m in (fwd1, fwd2, inv2, inv1))


def _filter_hidden_kernel(zf_ref, w1_ref, b1_ref, w2_ref, b2_ref, fq_ref, o_ref):
    fq = fq_ref[...]
    hid = jnp.sin(fq * (jnp.dot(zf_ref[...], w1_ref[...], precision=HIGHEST,
                                preferred_element_type=F32) + b1_ref[...]))
    o_ref[...] = jnp.sin(fq * (jnp.dot(hid, w2_ref[...], precision=HIGHEST,
                                       preferred_element_type=F32) + b2_ref[...]))


def _filter_kernel(hid_ref, w3f_ref, w3b_ref, dl_ref, f1_ref, f2_ref, o_ref,
                   hf_ref, hb_ref, sf_ref, sb_ref, *, L):
    n1 = 2 * L // FFT_FAST
    s = n1 // 2
    hid = hid_ref[...]
    row = lax.broadcasted_iota(jnp.int32, (L, 1), 0)
    t = row.astype(F32) * (1.0 / (L - 1))
    win = jnp.exp(-t * dl_ref[...])
    hf = jnp.dot(hid, w3f_ref[...], precision=HIGHEST, preferred_element_type=F32) * win
    hb = jnp.dot(hid, w3b_ref[...], precision=HIGHEST, preferred_element_type=F32) * win
    hb = jnp.where(row == 0, 0.0, hb)
    norm = (jnp.sum(jnp.abs(hf), axis=0, keepdims=True)
            + jnp.sum(jnp.abs(hb), axis=0, keepdims=True))
    cb = hf.shape[1]
    hf_ref[...] = hf.reshape(s, FFT_FAST, cb)
    hb_ref[...] = hb.reshape(s, FFT_FAST, cb)

    def stage1(g, c):
        f0 = pl.multiple_of(g * SUBLANES, SUBLANES)
        for src, dst in ((hf_ref, sf_ref), (hb_ref, sb_ref)):
            xg = src[:, pl.ds(f0, SUBLANES), :].reshape(s * SUBLANES, cb).astype(BF16)
            a = jnp.dot(f1_ref[...], xg, preferred_element_type=F32)
            dst[:, pl.ds(f0, SUBLANES), :] = a.reshape(2 * n1, SUBLANES, cb)
        return c

    lax.fori_loop(0, FFT_FAST // SUBLANES, stage1, 0)
    inv_norm = 1.0 / norm

    def stage2(k1, c):
        m = f2_ref[k1]
        xf = jnp.dot(m, sf_ref[pl.ds(2 * k1, 2)].reshape(2 * FFT_FAST, cb).astype(BF16),
                     preferred_element_type=F32)
        xb = jnp.dot(m, sb_ref[pl.ds(2 * k1, 2)].reshape(2 * FFT_FAST, cb).astype(BF16),
                     preferred_element_type=F32)
        re = (xf[:FFT_FAST] + xb[:FFT_FAST]) * inv_norm
        im = (xf[FFT_FAST:] - xb[FFT_FAST:]) * inv_norm
        r0 = pl.multiple_of(k1 * 2 * FFT_FAST, 2 * FFT_FAST)
        o_ref[0, pl.ds(r0, 2 * FFT_FAST), :] = jnp.concatenate([re, im], axis=0).astype(o_ref.dtype)
        return c

    lax.fori_loop(0, n1, stage2, 0, unroll=2)


def _hyena_filter_spectra(L, width, w1, b1, w2, b2, w3, freq, consts):
    fwd1, fwd2, _, _ = consts
    n1 = 2 * L // FFT_FAST
    cb = LANES
    ncb = width // cb
    hidden = w1.shape[1]
    tt = np.linspace(0.0, 1.0, L)[:, None]
    ww = 2.0 * np.pi * np.arange(L)[:, None] / L
    nb = (FILTER_EMB_DIM - 1) // 2
    bands = np.linspace(1e-4, nb - 1, nb)[None, :]
    feats = np.concatenate([tt, np.cos(bands * ww), -np.sin(bands * ww)], axis=-1)
    feats = np.pad(feats, ((0, 0), (0, FILTER_EMB_PAD - FILTER_EMB_DIM))).astype(np.float32)
    w1p = jnp.pad(w1.astype(F32), ((0, FILTER_EMB_PAD - FILTER_EMB_DIM), (0, 0)))
    min_decay = math.log(DECAY_TARGET) / SLOW_DECAY_PCT
    max_decay = math.log(DECAY_TARGET) / FAST_DECAY_PCT
    deltas = np.abs(np.linspace(min_decay, max_decay, width)).astype(np.float32)[None, :]

    s = n1 // 2
    one = lambda shape: pl.BlockSpec(shape, lambda i: (0,) * len(shape))
    hid = pl.pallas_call(
        _filter_hidden_kernel,
        grid=(1,),
        in_specs=[one((L, FILTER_EMB_PAD)), one((FILTER_EMB_PAD, hidden)), one((1, hidden)),
                  one((hidden, hidden)), one((1, hidden)), one((1, hidden))],
        out_specs=one((L, hidden)),
        out_shape=jax.ShapeDtypeStruct((L, hidden), F32),
        compiler_params=_cparams(("arbitrary",)),
        name="hyena_filter_hidden",
    )(jnp.asarray(feats), w1p, b1.astype(F32)[None, :], w2.astype(F32), b2.astype(F32)[None, :],
      freq.astype(F32)[None, :])
    full = lambda shape: pl.BlockSpec(shape, lambda n, c: (0,) * len(shape))
    return pl.pallas_call(
        functools.partial(_filter_kernel, L=L),
        grid=(HYENA_ORDER, ncb),
        in_specs=[
            full((L, hidden)),
            pl.BlockSpec((hidden, cb), lambda n, c: (0, n * ncb + c)),
            pl.BlockSpec((hidden, cb), lambda n, c: (0, HYENA_ORDER * ncb + n * ncb + c)),
            pl.BlockSpec((1, cb), lambda n, c: (0, c)),
            full(fwd1.shape), full(fwd2.shape),
        ],
        out_specs=pl.BlockSpec((1, n1 * 2 * FFT_FAST, cb), lambda n, c: (n, 0, c)),
        out_shape=jax.ShapeDtypeStruct((HYENA_ORDER, n1 * 2 * FFT_FAST, width), BF16),
        scratch_shapes=[pltpu.VMEM((s, FFT_FAST, cb), F32), pltpu.VMEM((s, FFT_FAST, cb), F32),
                        pltpu.VMEM((2 * n1, FFT_FAST, cb), F32), pltpu.VMEM((2 * n1, FFT_FAST, cb), F32)],
        compiler_params=_cparams(("arbitrary", "arbitrary")),
        name="hyena_filter_spectra",
    )(hid, w3.astype(F32), w3.astype(F32), jnp.asarray(deltas), fwd1, fwd2)


def _short_conv(p, w, b, L):
    row = lax.broadcasted_iota(jnp.int32, p.shape, 0)
    prev = jnp.where(row == 0, 0.0, pltpu.roll(p, 1, 0))
    nxt = jnp.where(row == L - 1, 0.0, pltpu.roll(p, L - 1, 0))
    return b + prev * w[0:1] + p * w[1:2] + nxt * w[2:3]


def _hyena_kernel(x1_ref, x2_ref, v_ref, w1_ref, w2_ref, wv_ref, b1_ref, b2_ref, bv_ref,
                  skip_ref, ng_ref, kf_ref, f1_ref, f2_ref, f2i_ref, g1_ref, gm_ref,
                  o_ref, z_ref, ga_ref, gb_ref, s_ref, *, L):
    n1 = 2 * L // FFT_FAST
    s = n1 // 2
    cb = o_ref.shape[-1]
    tiled = lambda a: a.reshape(s, FFT_FAST, cb)
    z_ref[...] = tiled(_short_conv(v_ref[0].astype(F32), wv_ref[...], bv_ref[...], L))
    ga_ref[...] = tiled(_short_conv(x1_ref[0].astype(F32), w1_ref[...], b1_ref[...], L))
    gb_ref[...] = tiled(_short_conv(x2_ref[0].astype(F32), w2_ref[...], b2_ref[...], L))
    groups = FFT_FAST // SUBLANES

    for n, gate_ref in enumerate((ga_ref, gb_ref)):
        skip = skip_ref[n:n + 1, :].reshape(1, 1, cb)

        def fwd1(g, c):
            f0 = pl.multiple_of(g * SUBLANES, SUBLANES)
            xg = z_ref[:, pl.ds(f0, SUBLANES), :].reshape(s * SUBLANES, cb).astype(BF16)
            a = jnp.dot(f1_ref[...], xg, preferred_element_type=F32)
            s_ref[:, pl.ds(f0, SUBLANES), :] = a.reshape(2 * n1, SUBLANES, cb)
            return c

        lax.fori_loop(0, groups, fwd1, 0)

        def mid(k1, c):
            blk = s_ref[pl.ds(2 * k1, 2)].reshape(2 * FFT_FAST, cb).astype(BF16)
            x = jnp.dot(f2_ref[k1], blk, preferred_element_type=F32)
            r0 = pl.multiple_of(k1 * 2 * FFT_FAST, 2 * FFT_FAST)
            kk = kf_ref[n, pl.ds(r0, 2 * FFT_FAST), :].astype(F32)
            xr, xi = x[:FFT_FAST], x[FFT_FAST:]
            kr, ki = kk[:FFT_FAST], kk[FFT_FAST:]
            y = jnp.concatenate([xr * kr - xi * ki, xr * ki + xi * kr], axis=0).astype(BF16)
            s_ref[pl.ds(2 * k1, 2)] = jnp.dot(f2i_ref[k1], y, preferred_element_type=F32).reshape(
                2, FFT_FAST, cb)
            return c

        lax.fori_loop(0, n1, mid, 0, unroll=4)

        def inv1(g, c):
            f0 = pl.multiple_of(g * SUBLANES, SUBLANES)
            bg = s_ref[:, pl.ds(f0, SUBLANES), :].reshape(2 * n1 * SUBLANES, cb).astype(BF16)
            y = jnp.dot(g1_ref[...], bg, preferred_element_type=F32).reshape(s, SUBLANES, cb)
            zf = z_ref[:, pl.ds(f0, SUBLANES), :]
            gf = gate_ref[:, pl.ds(f0, SUBLANES), :]
            z_ref[:, pl.ds(f0, SUBLANES), :] = gf * (y + skip * zf)
            return c

        lax.fori_loop(0, groups, inv1, 0)

    z = z_ref[...].reshape(L, cb)
    zz = z * z
    hi = zz.astype(BF16)
    lo = (zz - hi.astype(F32)).astype(BF16)
    ms = (jnp.dot(hi, gm_ref[...], preferred_element_type=F32)
          + jnp.dot(lo, gm_ref[...], preferred_element_type=F32))
    o_ref[0] = (z * lax.rsqrt(ms + NORM_EPS) * ng_ref[...]).astype(o_ref.dtype)


def _hyena_mixer(proj_h, conv_w, conv_b, kf, skip, norm_g, consts):
    B, L, cols = proj_h.shape
    width = cols // (HYENA_ORDER + 1)
    cb = LANES
    ncb = width // cb
    gsz = width // HYENA_GROUPS
    n1 = 2 * L // FFT_FAST
    fwd1, fwd2, inv2, inv1 = consts
    gm = np.kron(np.eye(cb // gsz), np.full((gsz, gsz), 1.0 / gsz)).astype(np.float32)

    def stream(i):
        return pl.BlockSpec((1, L, cb), lambda c, b: (b, 0, i * ncb + c))

    def cvec(rows, i):
        return pl.BlockSpec((rows, cb), lambda c, b: (0, i * ncb + c))

    def const(shape):
        return pl.BlockSpec(shape, lambda c, b: (0,) * len(shape), pipeline_mode=pl.Buffered(1))

    cw = conv_w.astype(F32)
    cbias = conv_b.astype(F32)[None, :]
    return pl.pallas_call(
        functools.partial(_hyena_kernel, L=L),
        grid=(ncb, B),
        in_specs=[
            stream(0), stream(1), stream(2),
            cvec(3, 0), cvec(3, 1), cvec(3, 2),
            cvec(1, 0), cvec(1, 1), cvec(1, 2),
            pl.BlockSpec((HYENA_ORDER, cb), lambda c, b: (0, c)),
            pl.BlockSpec((1, cb), lambda c, b: (0, c)),
            pl.BlockSpec((HYENA_ORDER, n1 * 2 * FFT_FAST, cb), lambda c, b: (0, 0, c),
                         pipeline_mode=pl.Buffered(1)),
            const(fwd1.shape), const(fwd2.shape), const(inv2.shape), const(inv1.shape),
            const(gm.shape),
        ],
        out_specs=pl.BlockSpec((1, L, cb), lambda c, b: (b, 0, c)),
        out_shape=jax.ShapeDtypeStruct((B, L, width), BF16),
        scratch_shapes=[pltpu.VMEM((n1 // 2, FFT_FAST, cb), F32), pltpu.VMEM((n1 // 2, FFT_FAST, cb), F32),
                        pltpu.VMEM((n1 // 2, FFT_FAST, cb), F32), pltpu.VMEM((2 * n1, FFT_FAST, cb), F32)],
        compiler_params=_cparams(("arbitrary", "arbitrary")),
        name="hyena_mixer",
    )(proj_h, proj_h, proj_h, cw, cw, cw, cbias, cbias, cbias,
      skip.astype(F32), norm_g.astype(F32)[None, :], kf,
      fwd1, fwd2, inv2, inv1, jnp.asarray(gm).astype(BF16))


def _rms(x, g):
    return x * lax.rsqrt(jnp.mean(x * x, axis=-1, keepdims=True) + NORM_EPS) * g


def _in_proj_kernel(x_ref, g_ref, wh_ref, wa_ref, oh_ref, oa_ref):
    hn = _rms(x_ref[...], g_ref[...]).astype(BF16)
    oh_ref[...] = jnp.dot(hn, wh_ref[...], preferred_element_type=F32).astype(oh_ref.dtype)
    oa_ref[...] = jnp.dot(hn, wa_ref[...], preferred_element_type=F32).astype(oa_ref.dtype)


def _in_proj(x2d, g, w_in, hyena_cols, tm):
    T, D = x2d.shape
    attn_cols = w_in.shape[1] - hyena_cols
    wh = w_in[:, :hyena_cols].astype(BF16)
    wa = w_in[:, hyena_cols:].astype(BF16)
    return pl.pallas_call(
        _in_proj_kernel,
        grid=(T // tm,),
        in_specs=[
            pl.BlockSpec((tm, D), lambda i: (i, 0)),
            pl.BlockSpec((1, D), lambda i: (0, 0)),
            pl.BlockSpec((D, hyena_cols), lambda i: (0, 0)),
            pl.BlockSpec((D, attn_cols), lambda i: (0, 0)),
        ],
        out_specs=[pl.BlockSpec((tm, hyena_cols), lambda i: (i, 0)),
                   pl.BlockSpec((tm, attn_cols), lambda i: (i, 0))],
        out_shape=[jax.ShapeDtypeStruct((T, hyena_cols), BF16),
                   jax.ShapeDtypeStruct((T, attn_cols), BF16)],
        compiler_params=_cparams(("arbitrary",)),
        name="in_proj",
    )(x2d, g.astype(F32)[None, :], wh, wa)


def _attn_kernel(slope_ref, q_ref, k_ref, v_ref, lq1_ref, lk1_ref, lq2_ref, lk2_ref, sg_ref, o_ref,
                 *, L, tq, tk, dh, lam_init):
    h = pl.program_id(1)
    qi = pl.program_id(2)
    slope = slope_ref[h]
    q = q_ref[0]
    lane = lax.broadcasted_iota(jnp.int32, q.shape, 1)
    qs = q * jnp.asarray(dh ** -0.5, q.dtype)
    q1 = jnp.where(lane < dh, qs, jnp.zeros_like(qs))
    q2 = jnp.where(lane >= dh, qs, jnp.zeros_like(qs))
    rel = (lax.broadcasted_iota(jnp.int32, (tq, tk), 0)
           - lax.broadcasted_iota(jnp.int32, (tq, tk), 1)).astype(F32)
    q0 = (qi * tq).astype(F32)

    def update(s, m, l, acc, vb):
        m_new = jnp.maximum(m, jnp.max(s, axis=-1, keepdims=True))
        alpha = jnp.exp(m - m_new)
        p = jnp.exp(s - m_new)
        l = alpha * l + jnp.sum(p, axis=-1, keepdims=True)
        acc = alpha * acc + jnp.dot(p.astype(BF16), vb, preferred_element_type=F32)
        return m_new, l, acc

    def body(j, carry):
        m1, l1, a1, m2, l2, a2 = carry
        k0 = pl.multiple_of(j * tk, tk)
        kb = k_ref[0, pl.ds(k0, tk), :]
        vb = v_ref[0, pl.ds(k0, tk), :]
        pen = slope * jnp.abs(rel + (q0 - k0.astype(F32)))
        nt = (((1,), (1,)), ((), ()))
        s1 = lax.dot_general(q1, kb, nt, preferred_element_type=F32) - pen
        s2 = lax.dot_general(q2, kb, nt, preferred_element_type=F32) - pen
        m1, l1, a1 = update(s1, m1, l1, a1, vb)
        m2, l2, a2 = update(s2, m2, l2, a2, vb)
        return m1, l1, a1, m2, l2, a2

    neg = jnp.full((tq, 1), -1e30, F32)
    zero = jnp.zeros((tq, 1), F32)
    zacc = jnp.zeros((tq, 2 * dh), F32)
    m1, l1, a1, m2, l2, a2 = lax.fori_loop(0, L // tk, body, (neg, zero, zacc, neg, zero, zacc))
    lam = (jnp.exp(jnp.sum(lq1_ref[...] * lk1_ref[...], axis=-1, keepdims=True))
           - jnp.exp(jnp.sum(lq2_ref[...] * lk2_ref[...], axis=-1, keepdims=True)) + lam_init)
    o = a1 / l1 - lam * (a2 / l2)
    o = _rms(o, sg_ref[...]) * (1.0 - lam_init)
    o_ref[0] = o.astype(o_ref.dtype)


def _diff_attention(qkv, lq1, lk1, lq2, lk2, subln_g, lam_init, tq, tk):
    B, L, cols = qkv.shape
    W = cols // 3
    H = ATTN_HEADS
    hw = W // H
    dh = hw // 2
    slopes = jnp.asarray([2.0 ** (-8.0 * (i + 1) / H) for i in range(H)], F32)
    vec = lambda a: a.astype(F32)[None, :]
    small = lambda n: pl.BlockSpec((1, n), lambda b, h, i: (0, 0))
    return pl.pallas_call(
        functools.partial(_attn_kernel, L=L, tq=tq, tk=tk, dh=dh, lam_init=lam_init),
        grid=(B, H, L // tq),
        in_specs=[
            pl.BlockSpec(memory_space=pltpu.SMEM),
            pl.BlockSpec((1, tq, hw), lambda b, h, i: (b, i, h)),
            pl.BlockSpec((1, L, hw), lambda b, h, i: (b, 0, H + h)),
            pl.BlockSpec((1, L, hw), lambda b, h, i: (b, 0, 2 * H + h)),
            small(dh), small(dh), small(dh), small(dh), small(hw),
        ],
        out_specs=pl.BlockSpec((1, tq, hw), lambda b, h, i: (b, i, h)),
        out_shape=jax.ShapeDtypeStruct((B, L, W), BF16),
        compiler_params=_cparams(("arbitrary", "arbitrary", "arbitrary")),
        name="diff_attention",
    )(slopes, qkv, qkv, qkv, vec(lq1), vec(lk1), vec(lq2), vec(lk2), vec(subln_g))


def _out_proj_kernel(x_ref, hy_ref, at_ref, wh_ref, wa_ref, g_ref, wr_ref, x2_ref, h2_ref, aff_ref):
    x2 = (x_ref[...] + jnp.dot(hy_ref[...], wh_ref[...], preferred_element_type=F32)
          + jnp.dot(at_ref[...], wa_ref[...], preferred_element_type=F32))
    x2_ref[...] = x2
    h2 = _rms(x2, g_ref[...])
    h2_ref[...] = h2
    logits = jnp.dot(h2, wr_ref[...], precision=HIGHEST, preferred_element_type=F32)
    e = jnp.exp(logits - jnp.max(logits, axis=-1, keepdims=True))
    aff_ref[...] = e / jnp.sum(e, axis=-1, keepdims=True)


def _out_proj(x2d, hy2d, at2d, w_out, g, w_router, tm):
    T, D = x2d.shape
    wdt = hy2d.shape[1]
    E = w_router.shape[1]
    wh = w_out[:wdt].astype(BF16)
    wa = w_out[wdt:].astype(BF16)
    row = lambda n: pl.BlockSpec((tm, n), lambda i: (i, 0))
    return pl.pallas_call(
        _out_proj_kernel,
        grid=(T // tm,),
        in_specs=[row(D), row(wdt), row(at2d.shape[1]),
                  pl.BlockSpec(wh.shape, lambda i: (0, 0)), pl.BlockSpec(wa.shape, lambda i: (0, 0)),
                  pl.BlockSpec((1, D), lambda i: (0, 0)), pl.BlockSpec((D, E), lambda i: (0, 0))],
        out_specs=[row(D), row(D), row(E)],
        out_shape=[jax.ShapeDtypeStruct((T, D), F32), jax.ShapeDtypeStruct((T, D), F32),
                   jax.ShapeDtypeStruct((T, E), F32)],
        compiler_params=_cparams(("arbitrary",)),
        name="out_proj_router",
    )(x2d, hy2d, at2d, wh, wa, g.astype(F32)[None, :], w_router.astype(F32))


SEL_CHUNK = 256


def _prefix_count(mask, tri_ref, L):
    parts = []
    carry = jnp.zeros((1, mask.shape[1]), F32)
    for blk in range(L // SEL_CHUNK):
        m = mask[blk * SEL_CHUNK:(blk + 1) * SEL_CHUNK]
        parts.append(jnp.dot(tri_ref[...], m.astype(BF16), preferred_element_type=F32) + carry)
        carry = carry + jnp.sum(m, axis=0, keepdims=True)
    return jnp.concatenate(parts, axis=0)


def _select_kernel(aff_ref, tri_ref, idx_ref, gate_ref, *, L, E, cap):
    aff = aff_ref[0]

    def step(i, prefix):
        cand = prefix | jnp.left_shift(jnp.int32(1), 30 - i)
        cnt = jnp.sum((aff >= pltpu.bitcast(cand, F32)).astype(jnp.int32), axis=0, keepdims=True)
        return jnp.where(cnt >= cap, cand, prefix)

    tau = pltpu.bitcast(lax.fori_loop(0, 31, step, jnp.zeros((1, E), jnp.int32)), F32)
    gt = aff > tau
    eq = aff == tau
    need = cap - jnp.sum(gt.astype(jnp.int32), axis=0, keepdims=True)
    tie_rank = _prefix_count(eq.astype(F32), tri_ref, L) - eq.astype(F32)
    sel = gt | (eq & (tie_rank < need.astype(F32)))
    rank = _prefix_count(sel.astype(F32), tri_ref, L)
    slot = jnp.where(sel, rank - 1.0, -1.0)
    tok = lax.broadcasted_iota(jnp.int32, (L, cap), 0).astype(F32)
    jj = lax.broadcasted_iota(jnp.int32, (L, cap), 1).astype(F32)
    for e in range(E):
        hit = slot[:, e:e + 1] == jj
        idx_ref[0, e:e + 1, :] = jnp.sum(jnp.where(hit, tok, 0.0), axis=0, keepdims=True).astype(jnp.int32)
        gate_ref[0, e:e + 1, :] = jnp.sum(jnp.where(hit, aff[:, e:e + 1], 0.0), axis=0, keepdims=True)


def _select(aff, cap):
    B, L, E = aff.shape
    tri = np.tril(np.ones((SEL_CHUNK, SEL_CHUNK), np.float32))
    return pl.pallas_call(
        functools.partial(_select_kernel, L=L, E=E, cap=cap),
        grid=(B,),
        in_specs=[pl.BlockSpec((1, L, E), lambda b: (b, 0, 0)),
                  pl.BlockSpec(tri.shape, lambda b: (0, 0))],
        out_specs=[pl.BlockSpec((1, E, cap), lambda b: (b, 0, 0)),
                   pl.BlockSpec((1, E, cap), lambda b: (b, 0, 0))],
        out_shape=[jax.ShapeDtypeStruct((B, E, cap), jnp.int32),
                   jax.ShapeDtypeStruct((B, E, cap), F32)],
        compiler_params=_cparams(("arbitrary",)),
        name="expert_choice_select",
    )(aff, jnp.asarray(tri).astype(BF16))


ROW_UNROLL = 8


def _gather_kernel(idx_ref, h_ref, o_ref, buf_ref, *, E, cap):
    b = pl.program_id(0)
    e = pl.program_id(1)
    base = (b * E + e) * cap

    def body(j, c):
        t = idx_ref[base + j]
        buf_ref[pl.ds(j, 1), :] = h_ref[0, pl.ds(t, 1), :]
        return c

    lax.fori_loop(0, cap, body, 0, unroll=ROW_UNROLL)
    o_ref[0, 0] = buf_ref[...].astype(o_ref.dtype)


def _gather(idx_flat, h2, E, cap):
    B, L, D = h2.shape
    return pl.pallas_call(
        functools.partial(_gather_kernel, E=E, cap=cap),
        grid_spec=pltpu.PrefetchScalarGridSpec(
            num_scalar_prefetch=1,
            grid=(B, E),
            in_specs=[pl.BlockSpec((1, L, D), lambda b, e, idx: (b, 0, 0))],
            out_specs=pl.BlockSpec((1, 1, cap, D), lambda b, e, idx: (e, b, 0, 0)),
            scratch_shapes=[pltpu.VMEM((cap, D), F32)],
        ),
        out_shape=jax.ShapeDtypeStruct((E, B, cap, D), BF16),
        compiler_params=_cparams(("arbitrary", "arbitrary")),
        name="expert_gather",
    )(idx_flat, h2)


def _ffn_kernel(x_ref, wg_ref, wu_ref, wd_ref, o_ref):
    f = pl.program_id(2)
    x = x_ref[0]
    a = jnp.dot(x, wg_ref[0].astype(BF16), preferred_element_type=F32)
    u = jnp.dot(x, wu_ref[0].astype(BF16), preferred_element_type=F32)
    hmid = (a * jax.nn.sigmoid(a) * u).astype(BF16)
    y = jnp.dot(hmid, wd_ref[0].astype(BF16), preferred_element_type=F32)

    @pl.when(f == 0)
    def _():
        o_ref[0] = y

    @pl.when(f > 0)
    def _():
        o_ref[0] += y


def _expert_ffn(xe, w_gate, w_up, w_down, tt, tf):
    E, T, D = xe.shape
    F = w_gate.shape[2]
    return pl.pallas_call(
        _ffn_kernel,
        grid=(E, T // tt, F // tf),
        in_specs=[
            pl.BlockSpec((1, tt, D), lambda e, i, f: (e, i, 0)),
            pl.BlockSpec((1, D, tf), lambda e, i, f: (e, 0, f)),
            pl.BlockSpec((1, D, tf), lambda e, i, f: (e, 0, f)),
            pl.BlockSpec((1, tf, D), lambda e, i, f: (e, f, 0)),
        ],
        out_specs=pl.BlockSpec((1, tt, D), lambda e, i, f: (e, i, 0)),
        out_shape=jax.ShapeDtypeStruct((E, T, D), F32),
        compiler_params=_cparams(("arbitrary", "arbitrary", "arbitrary")),
        name="expert_ffn",
    )(xe, w_gate, w_up, w_down)


def _scatter_kernel(idx_ref, gate_ref, y_ref, o_ref, *, E, cap):
    b = pl.program_id(0)
    e = pl.program_id(2)
    base = (b * E + e) * cap

    @pl.when(e == 0)
    def _():
        o_ref[...] = jnp.zeros_like(o_ref)

    def body(j, c):
        t = idx_ref[base + j]
        g = gate_ref[base + j]
        o_ref[0, pl.ds(t, 1), :] += g * y_ref[0, 0, pl.ds(j, 1), :]
        return c

    lax.fori_loop(0, cap, body, 0, unroll=ROW_UNROLL)


def _scatter(idx_flat, gate_flat, ye, L, dsplit):
    E, B, cap, D = ye.shape
    dc = D // dsplit
    return pl.pallas_call(
        functools.partial(_scatter_kernel, E=E, cap=cap),
        grid_spec=pltpu.PrefetchScalarGridSpec(
            num_scalar_prefetch=2,
            grid=(B, dsplit, E),
            in_specs=[pl.BlockSpec((1, 1, cap, dc), lambda b, d, e, idx, gt: (e, b, 0, d))],
            out_specs=pl.BlockSpec((1, L, dc), lambda b, d, e, idx, gt: (b, 0, d)),
        ),
        out_shape=jax.ShapeDtypeStruct((B, L, D), F32),
        compiler_params=_cparams(("arbitrary", "arbitrary", "arbitrary")),
        name="expert_scatter_add",
    )(idx_flat, gate_flat, ye)


def _final_kernel(x_ref, m_ref, g_ref, o_ref):
    o_ref[...] = _rms(x_ref[...] + m_ref[...], g_ref[...])


def _final_norm(x2, moe, g, tm):
    T, D = x2.shape
    row = pl.BlockSpec((tm, D), lambda i: (i, 0))
    return pl.pallas_call(
        _final_kernel,
        grid=(T // tm,),
        in_specs=[row, row, pl.BlockSpec((1, D), lambda i: (0, 0))],
        out_specs=row,
        out_shape=jax.ShapeDtypeStruct((T, D), F32),
        compiler_params=_cparams(("arbitrary",)),
        name="final_norm",
    )(x2, moe, g.astype(F32)[None, :])


def _moe(h2, aff, w_gate, w_up, w_down, tt, tf):
    B, L, D = h2.shape
    E = aff.shape[-1]
    cap = EC_CAPACITY_FACTOR * L // E
    idx, gate = _select(aff, cap)
    idx_flat = idx.reshape(-1)
    xe = _gather(idx_flat, h2, E, cap)
    ye = _expert_ffn(xe.reshape(E, B * cap, D), w_gate, w_up, w_down, tt, tf)
    return _scatter(idx_flat, gate.reshape(-1), ye.reshape(E, B, cap, D), L, dsplit=2)


def kernel(x, attn_norm_g, w_in, conv_w, conv_b, filt_w1, filt_b1, filt_w2, filt_b2, filt_w3, filt_freq, hyena_skip, hyena_norm_g, lambda_q1, lambda_k1, lambda_q2, lambda_k2, subln_g, w_out, ffn_norm_g, w_router, w_gate, w_up, w_down, final_norm_g):
    B, L, D = x.shape
    depth = w_in.shape[0]
    width = hyena_skip.shape[-1]
    hyena_cols = (HYENA_ORDER + 1) * width
    E = w_router.shape[-1]
    tm = min(512, B * L)
    tq = min(512, L)
    consts = _dft_constants(L)
    x2d = x.reshape(B * L, D)
    for l in range(depth):
        lam_init = 0.8 - 0.6 * math.exp(-0.3 * l)
        ph, qkv = _in_proj(x2d, attn_norm_g[l], w_in[l], hyena_cols, tm)
        kf = _hyena_filter_spectra(L, width, filt_w1[l], filt_b1[l], filt_w2[l], filt_b2[l],
                                   filt_w3[l], filt_freq[l], consts)
        hy = _hyena_mixer(ph.reshape(B, L, hyena_cols), conv_w[l], conv_b[l], kf, hyena_skip[l],
                          hyena_norm_g[l], consts)
        at = _diff_attention(qkv.reshape(B, L, -1), lambda_q1[l], lambda_k1[l], lambda_q2[l],
                             lambda_k2[l], subln_g[l], lam_init, tq, tq)
        x2, h2, aff = _out_proj(x2d, hy.reshape(B * L, width), at.reshape(B * L, -1), w_out[l],
                                ffn_norm_g[l], w_router[l], tm)
        cap = EC_CAPACITY_FACTOR * L // E
        moe = _moe(h2.reshape(B, L, D), aff.reshape(B, L, E), w_gate[l], w_up[l], w_down[l],
                   tt=min(2048, B * cap), tf=256)
        if l + 1 < depth:
            x2d = x2 + moe.reshape(B * L, D)
    return _final_norm(x2, moe.reshape(B * L, D), final_norm_g, tm).reshape(B, L, D)
```

```python
import functools
import math

import jax
import jax.numpy as jnp
import numpy as np
from jax import lax
from jax.experimental import pallas as pl
from jax.experimental.pallas import tpu as pltpu

F32 = jnp.float32
BF16 = jnp.bfloat16
HIGHEST = lax.Precision.HIGHEST

NORM_EPS = 1e-6
HYENA_ORDER = 2
HYENA_GROUPS = 8
ATTN_HEADS = 4
N_EXPERTS = 16
EC_CAPACITY_FACTOR = 2
FILTER_EMB_DIM = 33
FILTER_EMB_PAD = 40
DECAY_TARGET = 1e-2
FAST_DECAY_PCT = 0.3
SLOW_DECAY_PCT = 1.5

LANES = 128
FFT_FAST = 128
V7X_VMEM_LIMIT = 56 * 1024 * 1024


def _cparams(sem, vmem=V7X_VMEM_LIMIT):
    return pltpu.CompilerParams(dimension_semantics=sem, vmem_limit_bytes=vmem)


SUBLANES = 8


def _dft_constants(L):
    n2 = FFT_FAST
    n = 2 * L
    n1 = n // n2
    s = n1 // 2
    k1 = np.arange(n1)
    th1 = 2.0 * np.pi * np.outer(k1, np.arange(s)) / n1
    f1 = np.empty((2 * n1, s), np.float64)
    f1[0::2], f1[1::2] = np.cos(th1), -np.sin(th1)
    g1 = np.empty((s, 2 * n1), np.float64)
    g1[:, 0::2], g1[:, 1::2] = np.cos(th1.T), -np.sin(th1.T)
    eye = np.eye(SUBLANES)
    fwd1 = np.kron(f1, eye)
    inv1 = np.kron(g1, eye)
    a = np.arange(n2)
    ph = 2.0 * np.pi * (np.outer(a, a)[None] / n2 + (k1[:, None, None] * a[None, None, :]) / n)
    tr, ti = np.cos(ph), -np.sin(ph)
    fwd2 = np.concatenate([np.concatenate([tr, -ti], 2), np.concatenate([ti, tr], 2)], 1)
    pht = np.transpose(ph, (0, 2, 1))
    ur, ui = np.cos(pht) / n, np.sin(pht) / n
    inv2 = np.concatenate([np.concatenate([ur, -ui], 2), np.concatenate([ui, ur], 2)], 1)
    return tuple(jnp.asarray(m.astype(np.float32)).astype(BF16) for m in (fwd1, fwd2, inv2, inv1))


def _filter_hidden_kernel(zf_ref, w1_ref, b1_ref, w2_ref, b2_ref, fq_ref, o_ref):
    fq = fq_ref[...]
    hid = jnp.sin(fq * (jnp.dot(zf_ref[...], w1_ref[...], precision=HIGHEST,
                                preferred_element_type=F32) + b1_ref[...]))
    o_ref[...] = jnp.sin(fq * (jnp.dot(hid, w2_ref[...], precision=HIGHEST,
                                       preferred_element_type=F32) + b2_ref[...]))


def _filter_kernel(hid_ref, w3f_ref, w3b_ref, dl_ref, f1_ref, f2_ref, o_ref,
                   hf_ref, hb_ref, sf_ref, sb_ref, *, L):
    n1 = 2 * L // FFT_FAST
    s = n1 // 2
    hid = hid_ref[...]
    row = lax.broadcasted_iota(jnp.int32, (L, 1), 0)
    t = row.astype(F32) * (1.0 / (L - 1))
    win = jnp.exp(-t * dl_ref[...])
    hf = jnp.dot(hid, w3f_ref[...], precision=HIGHEST, preferred_element_type=F32) * win
    hb = jnp.dot(hid, w3b_ref[...], precision=HIGHEST, preferred_element_type=F32) * win
    hb = jnp.where(row == 0, 0.0, hb)
    norm = (jnp.sum(jnp.abs(hf), axis=0, keepdims=True)
            + jnp.sum(jnp.abs(hb), axis=0, keepdims=True))
    cb = hf.shape[1]
    hf_ref[...] = hf.reshape(s, FFT_FAST, cb)
    hb_ref[...] = hb.reshape(s, FFT_FAST, cb)

    def stage1(g, c):
        f0 = pl.multiple_of(g * SUBLANES, SUBLANES)
        for src, dst in ((hf_ref, sf_ref), (hb_ref, sb_ref)):
            xg = src[:, pl.ds(f0, SUBLANES), :].reshape(s * SUBLANES, cb).astype(BF16)
            a = jnp.dot(f1_ref[...], xg, preferred_element_type=F32)
            dst[:, pl.ds(f0, SUBLANES), :] = a.reshape(2 * n1, SUBLANES, cb)
        return c

    lax.fori_loop(0, FFT_FAST // SUBLANES, stage1, 0)
    inv_norm = 1.0 / norm

    def stage2(k1, c):
        m = f2_ref[k1]
        xf = jnp.dot(m, sf_ref[pl.ds(2 * k1, 2)].reshape(2 * FFT_FAST, cb).astype(BF16),
                     preferred_element_type=F32)
        xb = jnp.dot(m, sb_ref[pl.ds(2 * k1, 2)].reshape(2 * FFT_FAST, cb).astype(BF16),
                     preferred_element_type=F32)
        re = (xf[:FFT_FAST] + xb[:FFT_FAST]) * inv_norm
        im = (xf[FFT_FAST:] - xb[FFT_FAST:]) * inv_norm
        r0 = pl.multiple_of(k1 * 2 * FFT_FAST, 2 * FFT_FAST)
        o_ref[0, pl.ds(r0, 2 * FFT_FAST), :] = jnp.concatenate([re, im], axis=0).astype(o_ref.dtype)
        return c

    lax.fori_loop(0, n1, stage2, 0, unroll=2)


def _hyena_filter_spectra(L, width, w1, b1, w2, b2, w3, freq, consts):
    fwd1, fwd2, _, _ = consts
    n1 = 2 * L // FFT_FAST
    cb = LANES
    ncb = width // cb
    hidden = w1.shape[1]
    tt = np.linspace(0.0, 1.0, L)[:, None]
    ww = 2.0 * np.pi * np.arange(L)[:, None] / L
    nb = (FILTER_EMB_DIM - 1) // 2
    bands = np.linspace(1e-4, nb - 1, nb)[None, :]
    feats = np.concatenate([tt, np.cos(bands * ww), -np.sin(bands * ww)], axis=-1)
    feats = np.pad(feats, ((0, 0), (0, FILTER_EMB_PAD - FILTER_EMB_DIM))).astype(np.float32)
    w1p = jnp.pad(w1.astype(F32), ((0, FILTER_EMB_PAD - FILTER_EMB_DIM), (0, 0)))
    min_decay = math.log(DECAY_TARGET) / SLOW_DECAY_PCT
    max_decay = math.log(DECAY_TARGET) / FAST_DECAY_PCT
    deltas = np.abs(np.linspace(min_decay, max_decay, width)).astype(np.float32)[None, :]

    s = n1 // 2
    one = lambda shape: pl.BlockSpec(shape, lambda i: (0,) * len(shape))
    hid = pl.pallas_call(
        _filter_hidden_kernel,
        grid=(1,),
        in_specs=[one((L, FILTER_EMB_PAD)), one((FILTER_EMB_PAD, hidden)), one((1, hidden)),
                  one((hidden, hidden)), one((1, hidden)), one((1, hidden))],
        out_specs=one((L, hidden)),
        out_shape=jax.ShapeDtypeStruct((L, hidden), F32),
        compiler_params=_cparams(("arbitrary",)),
        name="hyena_filter_hidden",
    )(jnp.asarray(feats), w1p, b1.astype(F32)[None, :], w2.astype(F32), b2.astype(F32)[None, :],
      freq.astype(F32)[None, :])
    full = lambda shape: pl.BlockSpec(shape, lambda n, c: (0,) * len(shape))
    return pl.pallas_call(
        functools.partial(_filter_kernel, L=L),
        grid=(HYENA_ORDER, ncb),
        in_specs=[
            full((L, hidden)),
            pl.BlockSpec((hidden, cb), lambda n, c: (0, n * ncb + c)),
            pl.BlockSpec((hidden, cb), lambda n, c: (0, HYENA_ORDER * ncb + n * ncb + c)),
            pl.BlockSpec((1, cb), lambda n, c: (0, c)),
            full(fwd1.shape), full(fwd2.shape),
        ],
        out_specs=pl.BlockSpec((1, n1 * 2 * FFT_FAST, cb), lambda n, c: (n, 0, c)),
        out_shape=jax.ShapeDtypeStruct((HYENA_ORDER, n1 * 2 * FFT_FAST, width), BF16),
        scratch_shapes=[pltpu.VMEM((s, FFT_FAST, cb), F32), pltpu.VMEM((s, FFT_FAST, cb), F32),
                        pltpu.VMEM((2 * n1, FFT_FAST, cb), F32), pltpu.VMEM((2 * n1, FFT_FAST, cb), F32)],
        compiler_params=_cparams(("arbitrary", "arbitrary")),
        name="hyena_filter_spectra",
    )(hid, w3.astype(F32), w3.astype(F32), jnp.asarray(deltas), fwd1, fwd2)


def _short_conv(p, w, b, L):
    row = lax.broadcasted_iota(jnp.int32, p.shape, 0)
    prev = jnp.where(row == 0, 0.0, pltpu.roll(p, 1, 0))
    nxt = jnp.where(row == L - 1, 0.0, pltpu.roll(p, L - 1, 0))
    return b + prev * w[0:1] + p * w[1:2] + nxt * w[2:3]


def _hyena_kernel(x1_ref, x2_ref, v_ref, w1_ref, w2_ref, wv_ref, b1_ref, b2_ref, bv_ref,
                  skip_ref, ng_ref, kf_ref, f1_ref, f2_ref, f2i_ref, g1_ref, gm_ref,
                  o_ref, z_ref, ga_ref, gb_ref, s_ref, *, L):
    n1 = 2 * L // FFT_FAST
    s = n1 // 2
    cb = o_ref.shape[-1]
    tiled = lambda a: a.reshape(s, FFT_FAST, cb)
    z_ref[...] = tiled(_short_conv(v_ref[0].astype(F32), wv_ref[...], bv_ref[...], L))
    ga_ref[...] = tiled(_short_conv(x1_ref[0].astype(F32), w1_ref[...], b1_ref[...], L))
    gb_ref[...] = tiled(_short_conv(x2_ref[0].astype(F32), w2_ref[...], b2_ref[...], L))
    groups = FFT_FAST // SUBLANES

    for n, gate_ref in enumerate((ga_ref, gb_ref)):
        skip = skip_ref[n:n + 1, :].reshape(1, 1, cb)

        def fwd1(g, c):
            f0 = pl.multiple_of(g * SUBLANES, SUBLANES)
            xg = z_ref[:, pl.ds(f0, SUBLANES), :].reshape(s * SUBLANES, cb).astype(BF16)
            a = jnp.dot(f1_ref[...], xg, preferred_element_type=F32)
            s_ref[:, pl.ds(f0, SUBLANES), :] = a.reshape(2 * n1, SUBLANES, cb)
            return c

        lax.fori_loop(0, groups, fwd1, 0, unroll=4)

        def mid(k1, c):
            blk = s_ref[pl.ds(2 * k1, 2)].reshape(2 * FFT_FAST, cb).astype(BF16)
            x = jnp.dot(f2_ref[k1], blk, preferred_element_type=F32)
            r0 = pl.multiple_of(k1 * 2 * FFT_FAST, 2 * FFT_FAST)
            kk = kf_ref[n, pl.ds(r0, 2 * FFT_FAST), :].astype(F32)
            xr, xi = x[:FFT_FAST], x[FFT_FAST:]
            kr, ki = kk[:FFT_FAST], kk[FFT_FAST:]
            y = jnp.concatenate([xr * kr - xi * ki, xr * ki + xi * kr], axis=0).astype(BF16)
            s_ref[pl.ds(2 * k1, 2)] = jnp.dot(f2i_ref[k1], y, preferred_element_type=F32).reshape(
                2, FFT_FAST, cb)
            return c

        lax.fori_loop(0, n1, mid, 0, unroll=8)

        def inv1(g, c):
            f0 = pl.multiple_of(g * SUBLANES, SUBLANES)
            bg = s_ref[:, pl.ds(f0, SUBLANES), :].reshape(2 * n1 * SUBLANES, cb).astype(BF16)
            y = jnp.dot(g1_ref[...], bg, preferred_element_type=F32).reshape(s, SUBLANES, cb)
            zf = z_ref[:, pl.ds(f0, SUBLANES), :]
            gf = gate_ref[:, pl.ds(f0, SUBLANES), :]
            z_ref[:, pl.ds(f0, SUBLANES), :] = gf * (y + skip * zf)
            return c

        lax.fori_loop(0, groups, inv1, 0, unroll=4)

    z = z_ref[...].reshape(L, cb)
    zz = z * z
    hi = zz.astype(BF16)
    lo = (zz - hi.astype(F32)).astype(BF16)
    ms = (jnp.dot(hi, gm_ref[...], preferred_element_type=F32)
          + jnp.dot(lo, gm_ref[...], preferred_element_type=F32))
    o_ref[0] = (z * lax.rsqrt(ms + NORM_EPS) * ng_ref[...]).astype(o_ref.dtype)


def _hyena_mixer(proj_h, conv_w, conv_b, kf, skip, norm_g, consts):
    B, L, cols = proj_h.shape
    width = cols // (HYENA_ORDER + 1)
    cb = LANES
    ncb = width // cb
    gsz = width // HYENA_GROUPS
    n1 = 2 * L // FFT_FAST
    fwd1, fwd2, inv2, inv1 = consts
    gm = np.kron(np.eye(cb // gsz), np.full((gsz, gsz), 1.0 / gsz)).astype(np.float32)

    def stream(i):
        return pl.BlockSpec((1, L, cb), lambda c, b: (b, 0, i * ncb + c))

    def cvec(rows, i):
        return pl.BlockSpec((rows, cb), lambda c, b: (0, i * ncb + c))

    def const(shape):
        return pl.BlockSpec(shape, lambda c, b: (0,) * len(shape), pipeline_mode=pl.Buffered(1))

    cw = conv_w.astype(F32)
    cbias = conv_b.astype(F32)[None, :]
    return pl.pallas_call(
        functools.partial(_hyena_kernel, L=L),
        grid=(ncb, B),
        in_specs=[
            stream(0), stream(1), stream(2),
            cvec(3, 0), cvec(3, 1), cvec(3, 2),
            cvec(1, 0), cvec(1, 1), cvec(1, 2),
            pl.BlockSpec((HYENA_ORDER, cb), lambda c, b: (0, c)),
            pl.BlockSpec((1, cb), lambda c, b: (0, c)),
            pl.BlockSpec((HYENA_ORDER, n1 * 2 * FFT_FAST, cb), lambda c, b: (0, 0, c),
                         pipeline_mode=pl.Buffered(1)),
            const(fwd1.shape), const(fwd2.shape), const(inv2.shape), const(inv1.shape),
            const(gm.shape),
        ],
        out_specs=pl.BlockSpec((1, L, cb), lambda c, b: (b, 0, c)),
        out_shape=jax.ShapeDtypeStruct((B, L, width), BF16),
        scratch_shapes=[pltpu.VMEM((n1 // 2, FFT_FAST, cb), F32), pltpu.VMEM((n1 // 2, FFT_FAST, cb), F32),
                        pltpu.VMEM((n1 // 2, FFT_FAST, cb), F32), pltpu.VMEM((2 * n1, FFT_FAST, cb), F32)],
        compiler_params=_cparams(("arbitrary", "arbitrary")),
        name="hyena_mixer",
    )(proj_h, proj_h, proj_h, cw, cw, cw, cbias, cbias, cbias,
      skip.astype(F32), norm_g.astype(F32)[None, :], kf,
      fwd1, fwd2, inv2, inv1, jnp.asarray(gm).astype(BF16))


def _rms(x, g):
    return x * lax.rsqrt(jnp.mean(x * x, axis=-1, keepdims=True) + NORM_EPS) * g


def _in_proj_kernel(x_ref, g_ref, wh_ref, wa_ref, oh_ref, oa_ref):
    hn = _rms(x_ref[...], g_ref[...]).astype(BF16)
    oh_ref[...] = jnp.dot(hn, wh_ref[...], preferred_element_type=F32).astype(oh_ref.dtype)
    oa_ref[...] = jnp.dot(hn, wa_ref[...], preferred_element_type=F32).astype(oa_ref.dtype)


def _in_proj(x2d, g, w_in, hyena_cols, tm):
    T, D = x2d.shape
    attn_cols = w_in.shape[1] - hyena_cols
    wh = w_in[:, :hyena_cols].astype(BF16)
    wa = w_in[:, hyena_cols:].astype(BF16)
    return pl.pallas_call(
        _in_proj_kernel,
        grid=(T // tm,),
        in_specs=[
            pl.BlockSpec((tm, D), lambda i: (i, 0)),
            pl.BlockSpec((1, D), lambda i: (0, 0)),
            pl.BlockSpec((D, hyena_cols), lambda i: (0, 0)),
            pl.BlockSpec((D, attn_cols), lambda i: (0, 0)),
        ],
        out_specs=[pl.BlockSpec((tm, hyena_cols), lambda i: (i, 0)),
                   pl.BlockSpec((tm, attn_cols), lambda i: (i, 0))],
        out_shape=[jax.ShapeDtypeStruct((T, hyena_cols), BF16),
                   jax.ShapeDtypeStruct((T, attn_cols), BF16)],
        compiler_params=_cparams(("arbitrary",)),
        name="in_proj",
    )(x2d, g.astype(F32)[None, :], wh, wa)


def _attn_kernel(slope_ref, q_ref, k_ref, v_ref, lq1_ref, lk1_ref, lq2_ref, lk2_ref, sg_ref, o_ref,
                 *, L, tq, tk, dh, lam_init):
    h = pl.program_id(1)
    qi = pl.program_id(2)
    slope = slope_ref[h]
    q = q_ref[0]
    lane = lax.broadcasted_iota(jnp.int32, q.shape, 1)
    qs = q * jnp.asarray(dh ** -0.5, q.dtype)
    q1 = jnp.where(lane < dh, qs, jnp.zeros_like(qs))
    q2 = jnp.where(lane >= dh, qs, jnp.zeros_like(qs))
    rel = (lax.broadcasted_iota(jnp.int32, (tq, tk), 0)
           - lax.broadcasted_iota(jnp.int32, (tq, tk), 1)).astype(F32)
    q0 = (qi * tq).astype(F32)

    def update(s, m, l, acc, vb):
        m_new = jnp.maximum(m, jnp.max(s, axis=-1, keepdims=True))
        alpha = jnp.exp(m - m_new)
        p = jnp.exp(s - m_new)
        l = alpha * l + jnp.sum(p, axis=-1, keepdims=True)
        acc = alpha * acc + jnp.dot(p.astype(BF16), vb, preferred_element_type=F32)
        return m_new, l, acc

    def body(j, carry):
        m1, l1, a1, m2, l2, a2 = carry
        k0 = pl.multiple_of(j * tk, tk)
        kb = k_ref[0, pl.ds(k0, tk), :]
        vb = v_ref[0, pl.ds(k0, tk), :]
        pen = slope * jnp.abs(rel + (q0 - k0.astype(F32)))
        nt = (((1,), (1,)), ((), ()))
        s1 = lax.dot_general(q1, kb, nt, preferred_element_type=F32) - pen
        s2 = lax.dot_general(q2, kb, nt, preferred_element_type=F32) - pen
        m1, l1, a1 = update(s1, m1, l1, a1, vb)
        m2, l2, a2 = update(s2, m2, l2, a2, vb)
        return m1, l1, a1, m2, l2, a2

    neg = jnp.full((tq, 1), -1e30, F32)
    zero = jnp.zeros((tq, 1), F32)
    zacc = jnp.zeros((tq, 2 * dh), F32)
    m1, l1, a1, m2, l2, a2 = lax.fori_loop(0, L // tk, body, (neg, zero, zacc, neg, zero, zacc))
    lam = (jnp.exp(jnp.sum(lq1_ref[...] * lk1_ref[...], axis=-1, keepdims=True))
           - jnp.exp(jnp.sum(lq2_ref[...] * lk2_ref[...], axis=-1, keepdims=True)) + lam_init)
    o = a1 / l1 - lam * (a2 / l2)
    o = _rms(o, sg_ref[...]) * (1.0 - lam_init)
    o_ref[0] = o.astype(o_ref.dtype)


def _diff_attention(qkv, lq1, lk1, lq2, lk2, subln_g, lam_init, tq, tk):
    B, L, cols = qkv.shape
    W = cols // 3
    H = ATTN_HEADS
    hw = W // H
    dh = hw // 2
    slopes = jnp.asarray([2.0 ** (-8.0 * (i + 1) / H) for i in range(H)], F32)
    vec = lambda a: a.astype(F32)[None, :]
    small = lambda n: pl.BlockSpec((1, n), lambda b, h, i: (0, 0))
    return pl.pallas_call(
        functools.partial(_attn_kernel, L=L, tq=tq, tk=tk, dh=dh, lam_init=lam_init),
        grid=(B, H, L // tq),
        in_specs=[
            pl.BlockSpec(memory_space=pltpu.SMEM),
            pl.BlockSpec((1, tq, hw), lambda b, h, i: (b, i, h)),
            pl.BlockSpec((1, L, hw), lambda b, h, i: (b, 0, H + h)),
            pl.BlockSpec((1, L, hw), lambda b, h, i: (b, 0, 2 * H + h)),
            small(dh), small(dh), small(dh), small(dh), small(hw),
        ],
        out_specs=pl.BlockSpec((1, tq, hw), lambda b, h, i: (b, i, h)),
        out_shape=jax.ShapeDtypeStruct((B, L, W), BF16),
        compiler_params=_cparams(("arbitrary", "arbitrary", "arbitrary")),
        name="diff_attention",
    )(slopes, qkv, qkv, qkv, vec(lq1), vec(lk1), vec(lq2), vec(lk2), vec(subln_g))


def _out_proj_kernel(x_ref, hy_ref, at_ref, wh_ref, wa_ref, g_ref, wr_ref, x2_ref, h2_ref, aff_ref):
    x2 = (x_ref[...] + jnp.dot(hy_ref[...], wh_ref[...], preferred_element_type=F32)
          + jnp.dot(at_ref[...], wa_ref[...], preferred_element_type=F32))
    x2_ref[...] = x2
    h2 = _rms(x2, g_ref[...])
    h2_ref[...] = h2
    logits = jnp.dot(h2, wr_ref[...], precision=HIGHEST, preferred_element_type=F32)
    e = jnp.exp(logits - jnp.max(logits, axis=-1, keepdims=True))
    aff_ref[...] = e / jnp.sum(e, axis=-1, keepdims=True)


def _out_proj(x2d, hy2d, at2d, w_out, g, w_router, tm):
    T, D = x2d.shape
    wdt = hy2d.shape[1]
    E = w_router.shape[1]
    wh = w_out[:wdt].astype(BF16)
    wa = w_out[wdt:].astype(BF16)
    row = lambda n: pl.BlockSpec((tm, n), lambda i: (i, 0))
    return pl.pallas_call(
        _out_proj_kernel,
        grid=(T // tm,),
        in_specs=[row(D), row(wdt), row(at2d.shape[1]),
                  pl.BlockSpec(wh.shape, lambda i: (0, 0)), pl.BlockSpec(wa.shape, lambda i: (0, 0)),
                  pl.BlockSpec((1, D), lambda i: (0, 0)), pl.BlockSpec((D, E), lambda i: (0, 0))],
        out_specs=[row(D), row(D), row(E)],
        out_shape=[jax.ShapeDtypeStruct((T, D), F32), jax.ShapeDtypeStruct((T, D), F32),
                   jax.ShapeDtypeStruct((T, E), F32)],
        compiler_params=_cparams(("arbitrary",)),
        name="out_proj_router",
    )(x2d, hy2d, at2d, wh, wa, g.astype(F32)[None, :], w_router.astype(F32))


SEL_CHUNK = 256


def _prefix_count(mask, tri_ref, L):
    parts = []
    carry = jnp.zeros((1, mask.shape[1]), F32)
    for blk in range(L // SEL_CHUNK):
        m = mask[blk * SEL_CHUNK:(blk + 1) * SEL_CHUNK]
        parts.append(jnp.dot(tri_ref[...], m.astype(BF16), preferred_element_type=F32) + carry)
        carry = carry + jnp.sum(m, axis=0, keepdims=True)
    return jnp.concatenate(parts, axis=0)


def _select_kernel(aff_ref, tri_ref, idx_ref, gate_ref, *, L, E, cap):
    aff = aff_ref[0]

    def step(i, prefix):
        cand = prefix | jnp.left_shift(jnp.int32(1), 30 - i)
        cnt = jnp.sum((aff >= pltpu.bitcast(cand, F32)).astype(jnp.int32), axis=0, keepdims=True)
        return jnp.where(cnt >= cap, cand, prefix)

    tau = pltpu.bitcast(lax.fori_loop(0, 31, step, jnp.zeros((1, E), jnp.int32)), F32)
    gt = aff > tau
    eq = aff == tau
    need = cap - jnp.sum(gt.astype(jnp.int32), axis=0, keepdims=True)
    tie_rank = _prefix_count(eq.astype(F32), tri_ref, L) - eq.astype(F32)
    sel = gt | (eq & (tie_rank < need.astype(F32)))
    rank = _prefix_count(sel.astype(F32), tri_ref, L)
    slot = jnp.where(sel, rank - 1.0, -1.0)
    tok = lax.broadcasted_iota(jnp.int32, (L, cap), 0).astype(F32)
    jj = lax.broadcasted_iota(jnp.int32, (L, cap), 1).astype(F32)
    for e in range(E):
        hit = slot[:, e:e + 1] == jj
        idx_ref[0, e:e + 1, :] = jnp.sum(jnp.where(hit, tok, 0.0), axis=0, keepdims=True).astype(jnp.int32)
        gate_ref[0, e:e + 1, :] = jnp.sum(jnp.where(hit, aff[:, e:e + 1], 0.0), axis=0, keepdims=True)


def _select(aff, cap):
    B, L, E = aff.shape
    tri = np.tril(np.ones((SEL_CHUNK, SEL_CHUNK), np.float32))
    return pl.pallas_call(
        functools.partial(_select_kernel, L=L, E=E, cap=cap),
        grid=(B,),
        in_specs=[pl.BlockSpec((1, L, E), lambda b: (b, 0, 0)),
                  pl.BlockSpec(tri.shape, lambda b: (0, 0))],
        out_specs=[pl.BlockSpec((1, E, cap), lambda b: (b, 0, 0)),
                   pl.BlockSpec((1, E, cap), lambda b: (b, 0, 0))],
        out_shape=[jax.ShapeDtypeStruct((B, E, cap), jnp.int32),
                   jax.ShapeDtypeStruct((B, E, cap), F32)],
        compiler_params=_cparams(("arbitrary",)),
        name="expert_choice_select",
    )(aff, jnp.asarray(tri).astype(BF16))


ROW_UNROLL = 8


def _gather_kernel(idx_ref, h_ref, o_ref, buf_ref, *, E, cap):
    b = pl.program_id(0)
    e = pl.program_id(1)
    base = (b * E + e) * cap

    def body(j, c):
        t = idx_ref[base + j]
        buf_ref[pl.ds(j, 1), :] = h_ref[0, pl.ds(t, 1), :]
        return c

    lax.fori_loop(0, cap, body, 0, unroll=ROW_UNROLL)
    o_ref[0, 0] = buf_ref[...].astype(o_ref.dtype)


def _gather(idx_flat, h2, E, cap):
    B, L, D = h2.shape
    return pl.pallas_call(
        functools.partial(_gather_kernel, E=E, cap=cap),
        grid_spec=pltpu.PrefetchScalarGridSpec(
            num_scalar_prefetch=1,
            grid=(B, E),
            in_specs=[pl.BlockSpec((1, L, D), lambda b, e, idx: (b, 0, 0))],
            out_specs=pl.BlockSpec((1, 1, cap, D), lambda b, e, idx: (e, b, 0, 0)),
            scratch_shapes=[pltpu.VMEM((cap, D), F32)],
        ),
        out_shape=jax.ShapeDtypeStruct((E, B, cap, D), BF16),
        compiler_params=_cparams(("arbitrary", "arbitrary")),
        name="expert_gather",
    )(idx_flat, h2)


def _ffn_kernel(x_ref, wg_ref, wu_ref, wd_ref, *rest):
    x = x_ref[0]
    a = jnp.dot(x, wg_ref[0].astype(BF16), preferred_element_type=F32)
    u = jnp.dot(x, wu_ref[0].astype(BF16), preferred_element_type=F32)
    hmid = (a * jax.nn.sigmoid(a) * u).astype(BF16)
    y = jnp.dot(hmid, wd_ref[0].astype(BF16), preferred_element_type=F32)
    if len(rest) == 2:
        y = rest[0][0] + y
    rest[-1][0] = y


def _expert_ffn(xe, w_gate, w_up, w_down, tt, slabs):
    E, T, D = xe.shape
    F = w_gate.shape[2]
    tf = F // slabs
    y = None
    for s in range(slabs):
        tile = pl.BlockSpec((1, tt, D), lambda e, i: (e, i, 0))
        in_specs = [
            tile,
            pl.BlockSpec((1, D, tf), lambda e, i, s=s: (e, 0, s)),
            pl.BlockSpec((1, D, tf), lambda e, i, s=s: (e, 0, s)),
            pl.BlockSpec((1, tf, D), lambda e, i, s=s: (e, s, 0)),
        ]
        args = [xe, w_gate, w_up, w_down]
        if y is not None:
            in_specs.append(tile)
            args.append(y)
        y = pl.pallas_call(
            _ffn_kernel,
            grid=(E, T // tt),
            in_specs=in_specs,
            out_specs=tile,
            out_shape=jax.ShapeDtypeStruct((E, T, D), F32),
            compiler_params=_cparams(("arbitrary", "arbitrary")),
            name=f"expert_ffn_slab{s}",
        )(*args)
    return y


def _scatter_kernel(idx_ref, gate_ref, y_ref, o_ref, *, E, cap):
    b = pl.program_id(0)
    e = pl.program_id(2)
    base = (b * E + e) * cap

    @pl.when(e == 0)
    def _():
        o_ref[...] = jnp.zeros_like(o_ref)

    def body(j, c):
        t = idx_ref[base + j]
        g = gate_ref[base + j]
        o_ref[0, pl.ds(t, 1), :] += g * y_ref[0, 0, pl.ds(j, 1), :]
        return c

    lax.fori_loop(0, cap, body, 0, unroll=ROW_UNROLL)


def _scatter(idx_flat, gate_flat, ye, L, dsplit):
    E, B, cap, D = ye.shape
    dc = D // dsplit
    return pl.pallas_call(
        functools.partial(_scatter_kernel, E=E, cap=cap),
        grid_spec=pltpu.PrefetchScalarGridSpec(
            num_scalar_prefetch=2,
            grid=(B, dsplit, E),
            in_specs=[pl.BlockSpec((1, 1, cap, dc), lambda b, d, e, idx, gt: (e, b, 0, d))],
            out_specs=pl.BlockSpec((1, L, dc), lambda b, d, e, idx, gt: (b, 0, d)),
        ),
        out_shape=jax.ShapeDtypeStruct((B, L, D), F32),
        compiler_params=_cparams(("arbitrary", "arbitrary", "arbitrary")),
        name="expert_scatter_add",
    )(idx_flat, gate_flat, ye)


def _final_kernel(x_ref, m_ref, g_ref, o_ref):
    o_ref[...] = _rms(x_ref[...] + m_ref[...], g_ref[...])


def _final_norm(x2, moe, g, tm):
    T, D = x2.shape
    row = pl.BlockSpec((tm, D), lambda i: (i, 0))
    return pl.pallas_call(
        _final_kernel,
        grid=(T // tm,),
        in_specs=[row, row, pl.BlockSpec((1, D), lambda i: (0, 0))],
        out_specs=row,
        out_shape=jax.ShapeDtypeStruct((T, D), F32),
        compiler_params=_cparams(("arbitrary",)),
        name="final_norm",
    )(x2, moe, g.astype(F32)[None, :])


FFN_SLABS = 2


def _moe(h2, aff, w_gate, w_up, w_down, tt):
    B, L, D = h2.shape
    E = aff.shape[-1]
    cap = EC_CAPACITY_FACTOR * L // E
    idx, gate = _select(aff, cap)
    idx_flat = idx.reshape(-1)
    xe = _gather(idx_flat, h2, E, cap)
    ye = _expert_ffn(xe.reshape(E, B * cap, D), w_gate, w_up, w_down, tt, FFN_SLABS)
    return _scatter(idx_flat, gate.reshape(-1), ye.reshape(E, B, cap, D), L, dsplit=1)


def kernel(x, attn_norm_g, w_in, conv_w, conv_b, filt_w1, filt_b1, filt_w2, filt_b2, filt_w3, filt_freq, hyena_skip, hyena_norm_g, lambda_q1, lambda_k1, lambda_q2, lambda_k2, subln_g, w_out, ffn_norm_g, w_router, w_gate, w_up, w_down, final_norm_g):
    B, L, D = x.shape
    depth = w_in.shape[0]
    width = hyena_skip.shape[-1]
    hyena_cols = (HYENA_ORDER + 1) * width
    E = w_router.shape[-1]
    tm = min(512, B * L)
    tq = min(512, L)
    consts = _dft_constants(L)
    x2d = x.reshape(B * L, D)
    for l in range(depth):
        lam_init = 0.8 - 0.6 * math.exp(-0.3 * l)
        ph, qkv = _in_proj(x2d, attn_norm_g[l], w_in[l], hyena_cols, tm)
        kf = _hyena_filter_spectra(L, width, filt_w1[l], filt_b1[l], filt_w2[l], filt_b2[l],
                                   filt_w3[l], filt_freq[l], consts)
        hy = _hyena_mixer(ph.reshape(B, L, hyena_cols), conv_w[l], conv_b[l], kf, hyena_skip[l],
                          hyena_norm_g[l], consts)
        at = _diff_attention(qkv.reshape(B, L, -1), lambda_q1[l], lambda_k1[l], lambda_q2[l],
                             lambda_k2[l], subln_g[l], lam_init, tq, tq)
        x2, h2, aff = _out_proj(x2d, hy.reshape(B * L, width), at.reshape(B * L, -1), w_out[l],
                                ffn_norm_g[l], w_router[l], tm)
        cap = EC_CAPACITY_FACTOR * L // E
        moe = _moe(h2.reshape(B, L, D), aff.reshape(B, L, E), w_gate[l], w_up[l], w_down[l],
                   tt=min(512, B * cap))
        if l + 1 < depth:
            x2d = x2 + moe.reshape(B * L, D)
    return _final_norm(x2, moe.reshape(B * L, D), final_norm_g, tm).reshape(B, L, D)
```

```python
import functools
import math

import jax
import jax.numpy as jnp
import numpy as np
from jax import lax
from jax.experimental import pallas as pl
from jax.experimental.pallas import tpu as pltpu

F32 = jnp.float32
BF16 = jnp.bfloat16
HIGHEST = lax.Precision.HIGHEST

NORM_EPS = 1e-6
HYENA_ORDER = 2
HYENA_GROUPS = 8
ATTN_HEADS = 4
N_EXPERTS = 16
EC_CAPACITY_FACTOR = 2
FILTER_EMB_DIM = 33
FILTER_EMB_PAD = 40
DECAY_TARGET = 1e-2
FAST_DECAY_PCT = 0.3
SLOW_DECAY_PCT = 1.5

LANES = 128
FFT_FAST = 128
V7X_VMEM_LIMIT = 56 * 1024 * 1024


def _cparams(sem, vmem=V7X_VMEM_LIMIT):
    return pltpu.CompilerParams(dimension_semantics=sem, vmem_limit_bytes=vmem)


SUBLANES = 8


def _dft_constants(L):
    n2 = FFT_FAST
    n = 2 * L
    n1 = n // n2
    s = n1 // 2
    k1 = np.arange(n1)
    th1 = 2.0 * np.pi * np.outer(k1, np.arange(s)) / n1
    f1 = np.empty((2 * n1, s), np.float64)
    f1[0::2], f1[1::2] = np.cos(th1), -np.sin(th1)
    g1 = np.empty((s, 2 * n1), np.float64)
    g1[:, 0::2], g1[:, 1::2] = np.cos(th1.T), -np.sin(th1.T)
    eye = np.eye(SUBLANES)
    fwd1 = np.kron(f1, eye)
    inv1 = np.kron(g1, eye)
    a = np.arange(n2)
    ph = 2.0 * np.pi * (np.outer(a, a)[None] / n2 + (k1[:, None, None] * a[None, None, :]) / n)
    tr, ti = np.cos(ph), -np.sin(ph)
    fwd2 = np.concatenate([np.concatenate([tr, -ti], 2), np.concatenate([ti, tr], 2)], 1)
    pht = np.transpose(ph, (0, 2, 1))
    ur, ui = np.cos(pht) / n, np.sin(pht) / n
    inv2 = np.concatenate([np.concatenate([ur, -ui], 2), np.concatenate([ui, ur], 2)], 1)
    return tuple(jnp.asarray(m.astype(np.float32)).astype(BF16) for m in (fwd1, fwd2, inv2, inv1))


def _filter_hidden_kernel(zf_ref, w1_ref, b1_ref, w2_ref, b2_ref, fq_ref, o_ref):
    fq = fq_ref[...]
    hid = jnp.sin(fq * (jnp.dot(zf_ref[...], w1_ref[...], precision=HIGHEST,
                                preferred_element_type=F32) + b1_ref[...]))
    o_ref[...] = jnp.sin(fq * (jnp.dot(hid, w2_ref[...], precision=HIGHEST,
                                       preferred_element_type=F32) + b2_ref[...]))


def _filter_kernel(hid_ref, w3f_ref, w3b_ref, dl_ref, f1_ref, f2_ref, o_ref,
                   hf_ref, hb_ref, sf_ref, sb_ref, *, L):
    n1 = 2 * L // FFT_FAST
    s = n1 // 2
    hid = hid_ref[...]
    row = lax.broadcasted_iota(jnp.int32, (L, 1), 0)
    t = row.astype(F32) * (1.0 / (L - 1))
    win = jnp.exp(-t * dl_ref[...])
    hf = jnp.dot(hid, w3f_ref[...], precision=HIGHEST, preferred_element_type=F32) * win
    hb = jnp.dot(hid, w3b_ref[...], precision=HIGHEST, preferred_element_type=F32) * win
    hb = jnp.where(row == 0, 0.0, hb)
    norm = (jnp.sum(jnp.abs(hf), axis=0, keepdims=True)
            + jnp.sum(jnp.abs(hb), axis=0, keepdims=True))
    cb = hf.shape[1]
    hf_ref[...] = hf.reshape(s, FFT_FAST, cb)
    hb_ref[...] = hb.reshape(s, FFT_FAST, cb)

    def stage1(g, c):
        f0 = pl.multiple_of(g * SUBLANES, SUBLANES)
        for src, dst in ((hf_ref, sf_ref), (hb_ref, sb_ref)):
            xg = src[:, pl.ds(f0, SUBLANES), :].reshape(s * SUBLANES, cb).astype(BF16)
            a = jnp.dot(f1_ref[...], xg, preferred_element_type=F32)
            dst[:, pl.ds(f0, SUBLANES), :] = a.reshape(2 * n1, SUBLANES, cb)
        return c

    lax.fori_loop(0, FFT_FAST // SUBLANES, stage1, 0)
    inv_norm = 1.0 / norm

    def stage2(k1, c):
        m = f2_ref[k1]
        xf = jnp.dot(m, sf_ref[pl.ds(2 * k1, 2)].reshape(2 * FFT_FAST, cb).astype(BF16),
                     preferred_element_type=F32)
        xb = jnp.dot(m, sb_ref[pl.ds(2 * k1, 2)].reshape(2 * FFT_FAST, cb).astype(BF16),
                     preferred_element_type=F32)
        re = (xf[:FFT_FAST] + xb[:FFT_FAST]) * inv_norm
        im = (xf[FFT_FAST:] - xb[FFT_FAST:]) * inv_norm
        r0 = pl.multiple_of(k1 * 2 * FFT_FAST, 2 * FFT_FAST)
        o_ref[0, pl.ds(r0, 2 * FFT_FAST), :] = jnp.concatenate([re, im], axis=0).astype(o_ref.dtype)
        return c

    lax.fori_loop(0, n1, stage2, 0, unroll=2)


def _hyena_filter_spectra(L, width, w1, b1, w2, b2, w3, freq, consts):
    fwd1, fwd2, _, _ = consts
    n1 = 2 * L // FFT_FAST
    cb = LANES
    ncb = width // cb
    hidden = w1.shape[1]
    tt = np.linspace(0.0, 1.0, L)[:, None]
    ww = 2.0 * np.pi * np.arange(L)[:, None] / L
    nb = (FILTER_EMB_DIM - 1) // 2
    bands = np.linspace(1e-4, nb - 1, nb)[None, :]
    feats = np.concatenate([tt, np.cos(bands * ww), -np.sin(bands * ww)], axis=-1)
    feats = np.pad(feats, ((0, 0), (0, FILTER_EMB_PAD - FILTER_EMB_DIM))).astype(np.float32)
    w1p = jnp.pad(w1.astype(F32), ((0, FILTER_EMB_PAD - FILTER_EMB_DIM), (0, 0)))
    min_decay = math.log(DECAY_TARGET) / SLOW_DECAY_PCT
    max_decay = math.log(DECAY_TARGET) / FAST_DECAY_PCT
    deltas = np.abs(np.linspace(min_decay, max_decay, width)).astype(np.float32)[None, :]

    s = n1 // 2
    one = lambda shape: pl.BlockSpec(shape, lambda i: (0,) * len(shape))
    hid = pl.pallas_call(
        _filter_hidden_kernel,
        grid=(1,),
        in_specs=[one((L, FILTER_EMB_PAD)), one((FILTER_EMB_PAD, hidden)), one((1, hidden)),
                  one((hidden, hidden)), one((1, hidden)), one((1, hidden))],
        out_specs=one((L, hidden)),
        out_shape=jax.ShapeDtypeStruct((L, hidden), F32),
        compiler_params=_cparams(("arbitrary",)),
        name="hyena_filter_hidden",
    )(jnp.asarray(feats), w1p, b1.astype(F32)[None, :], w2.astype(F32), b2.astype(F32)[None, :],
      freq.astype(F32)[None, :])
    full = lambda shape: pl.BlockSpec(shape, lambda n, c: (0,) * len(shape))
    return pl.pallas_call(
        functools.partial(_filter_kernel, L=L),
        grid=(HYENA_ORDER, ncb),
        in_specs=[
            full((L, hidden)),
            pl.BlockSpec((hidden, cb), lambda n, c: (0, n * ncb + c)),
            pl.BlockSpec((hidden, cb), lambda n, c: (0, HYENA_ORDER * ncb + n * ncb + c)),
            pl.BlockSpec((1, cb), lambda n, c: (0, c)),
            full(fwd1.shape), full(fwd2.shape),
        ],
        out_specs=pl.BlockSpec((1, n1 * 2 * FFT_FAST, cb), lambda n, c: (n, 0, c)),
        out_shape=jax.ShapeDtypeStruct((HYENA_ORDER, n1 * 2 * FFT_FAST, width), BF16),
        scratch_shapes=[pltpu.VMEM((s, FFT_FAST, cb), F32), pltpu.VMEM((s, FFT_FAST, cb), F32),
                        pltpu.VMEM((2 * n1, FFT_FAST, cb), F32), pltpu.VMEM((2 * n1, FFT_FAST, cb), F32)],
        compiler_params=_cparams(("arbitrary", "arbitrary")),
        name="hyena_filter_spectra",
    )(hid, w3.astype(F32), w3.astype(F32), jnp.asarray(deltas), fwd1, fwd2)


def _short_conv(p, w, b, L):
    row = lax.broadcasted_iota(jnp.int32, p.shape, 0)
    prev = jnp.where(row == 0, 0.0, pltpu.roll(p, 1, 0))
    nxt = jnp.where(row == L - 1, 0.0, pltpu.roll(p, L - 1, 0))
    return b + prev * w[0:1] + p * w[1:2] + nxt * w[2:3]


def _hyena_kernel(x1_ref, x2_ref, v_ref, w1_ref, w2_ref, wv_ref, b1_ref, b2_ref, bv_ref,
                  skip_ref, ng_ref, kf_ref, f1_ref, f2_ref, f2i_ref, g1_ref, gm_ref,
                  o_ref, z_ref, ga_ref, gb_ref, s_ref, *, L):
    n1 = 2 * L // FFT_FAST
    s = n1 // 2
    cb = o_ref.shape[-1]
    tiled = lambda a: a.reshape(s, FFT_FAST, cb)
    z_ref[...] = tiled(_short_conv(v_ref[0].astype(F32), wv_ref[...], bv_ref[...], L))
    ga_ref[...] = tiled(_short_conv(x1_ref[0].astype(F32), w1_ref[...], b1_ref[...], L))
    gb_ref[...] = tiled(_short_conv(x2_ref[0].astype(F32), w2_ref[...], b2_ref[...], L))
    groups = FFT_FAST // SUBLANES

    for n, gate_ref in enumerate((ga_ref, gb_ref)):
        skip = skip_ref[n:n + 1, :].reshape(1, 1, cb)

        def fwd1(g, c):
            f0 = pl.multiple_of(g * SUBLANES, SUBLANES)
            xg = z_ref[:, pl.ds(f0, SUBLANES), :].reshape(s * SUBLANES, cb).astype(BF16)
            a = jnp.dot(f1_ref[...], xg, preferred_element_type=F32)
            s_ref[:, pl.ds(f0, SUBLANES), :] = a.reshape(2 * n1, SUBLANES, cb)
            return c

        lax.fori_loop(0, groups, fwd1, 0, unroll=8)

        def mid(k1, c):
            blk = s_ref[pl.ds(2 * k1, 2)].reshape(2 * FFT_FAST, cb).astype(BF16)
            x = jnp.dot(f2_ref[k1], blk, preferred_element_type=F32)
            r0 = pl.multiple_of(k1 * 2 * FFT_FAST, 2 * FFT_FAST)
            kk = kf_ref[n, pl.ds(r0, 2 * FFT_FAST), :].astype(F32)
            xr, xi = x[:FFT_FAST], x[FFT_FAST:]
            kr, ki = kk[:FFT_FAST], kk[FFT_FAST:]
            y = jnp.concatenate([xr * kr - xi * ki, xr * ki + xi * kr], axis=0).astype(BF16)
            s_ref[pl.ds(2 * k1, 2)] = jnp.dot(f2i_ref[k1], y, preferred_element_type=F32).reshape(
                2, FFT_FAST, cb)
            return c

        lax.fori_loop(0, n1, mid, 0, unroll=16)

        def inv1(g, c):
            f0 = pl.multiple_of(g * SUBLANES, SUBLANES)
            bg = s_ref[:, pl.ds(f0, SUBLANES), :].reshape(2 * n1 * SUBLANES, cb).astype(BF16)
            y = jnp.dot(g1_ref[...], bg, preferred_element_type=F32).reshape(s, SUBLANES, cb)
            zf = z_ref[:, pl.ds(f0, SUBLANES), :]
            gf = gate_ref[:, pl.ds(f0, SUBLANES), :]
            z_ref[:, pl.ds(f0, SUBLANES), :] = gf * (y + skip * zf)
            return c

        lax.fori_loop(0, groups, inv1, 0, unroll=8)

    z = z_ref[...].reshape(L, cb)
    zz = z * z
    hi = zz.astype(BF16)
    lo = (zz - hi.astype(F32)).astype(BF16)
    ms = (jnp.dot(hi, gm_ref[...], preferred_element_type=F32)
          + jnp.dot(lo, gm_ref[...], preferred_element_type=F32))
    o_ref[0] = (z * lax.rsqrt(ms + NORM_EPS) * ng_ref[...]).astype(o_ref.dtype)


def _hyena_mixer(proj_h, conv_w, conv_b, kf, skip, norm_g, consts):
    B, L, cols = proj_h.shape
    width = cols // (HYENA_ORDER + 1)
    cb = LANES
    ncb = width // cb
    gsz = width // HYENA_GROUPS
    n1 = 2 * L // FFT_FAST
    fwd1, fwd2, inv2, inv1 = consts
    gm = np.kron(np.eye(cb // gsz), np.full((gsz, gsz), 1.0 / gsz)).astype(np.float32)

    def stream(i):
        return pl.BlockSpec((1, L, cb), lambda c, b: (b, 0, i * ncb + c))

    def cvec(rows, i):
        return pl.BlockSpec((rows, cb), lambda c, b: (0, i * ncb + c))

    def const(shape):
        return pl.BlockSpec(shape, lambda c, b: (0,) * len(shape), pipeline_mode=pl.Buffered(1))

    cw = conv_w.astype(F32)
    cbias = conv_b.astype(F32)[None, :]
    return pl.pallas_call(
        functools.partial(_hyena_kernel, L=L),
        grid=(ncb, B),
        in_specs=[
            stream(0), stream(1), stream(2),
            cvec(3, 0), cvec(3, 1), cvec(3, 2),
            cvec(1, 0), cvec(1, 1), cvec(1, 2),
            pl.BlockSpec((HYENA_ORDER, cb), lambda c, b: (0, c)),
            pl.BlockSpec((1, cb), lambda c, b: (0, c)),
            pl.BlockSpec((HYENA_ORDER, n1 * 2 * FFT_FAST, cb), lambda c, b: (0, 0, c),
                         pipeline_mode=pl.Buffered(1)),
            const(fwd1.shape), const(fwd2.shape), const(inv2.shape), const(inv1.shape),
            const(gm.shape),
        ],
        out_specs=pl.BlockSpec((1, L, cb), lambda c, b: (b, 0, c)),
        out_shape=jax.ShapeDtypeStruct((B, L, width), BF16),
        scratch_shapes=[pltpu.VMEM((n1 // 2, FFT_FAST, cb), F32), pltpu.VMEM((n1 // 2, FFT_FAST, cb), F32),
                        pltpu.VMEM((n1 // 2, FFT_FAST, cb), F32), pltpu.VMEM((2 * n1, FFT_FAST, cb), F32)],
        compiler_params=_cparams(("arbitrary", "arbitrary")),
        name="hyena_mixer",
    )(proj_h, proj_h, proj_h, cw, cw, cw, cbias, cbias, cbias,
      skip.astype(F32), norm_g.astype(F32)[None, :], kf,
      fwd1, fwd2, inv2, inv1, jnp.asarray(gm).astype(BF16))


def _rms(x, g):
    return x * lax.rsqrt(jnp.mean(x * x, axis=-1, keepdims=True) + NORM_EPS) * g


def _in_proj_kernel(x_ref, g_ref, wh_ref, wa_ref, oh_ref, oa_ref):
    hn = _rms(x_ref[...], g_ref[...]).astype(BF16)
    oh_ref[...] = jnp.dot(hn, wh_ref[...], preferred_element_type=F32).astype(oh_ref.dtype)
    oa_ref[...] = jnp.dot(hn, wa_ref[...], preferred_element_type=F32).astype(oa_ref.dtype)


def _in_proj(x2d, g, w_in, hyena_cols, tm):
    T, D = x2d.shape
    attn_cols = w_in.shape[1] - hyena_cols
    wh = w_in[:, :hyena_cols].astype(BF16)
    wa = w_in[:, hyena_cols:].astype(BF16)
    return pl.pallas_call(
        _in_proj_kernel,
        grid=(T // tm,),
        in_specs=[
            pl.BlockSpec((tm, D), lambda i: (i, 0)),
            pl.BlockSpec((1, D), lambda i: (0, 0)),
            pl.BlockSpec((D, hyena_cols), lambda i: (0, 0)),
            pl.BlockSpec((D, attn_cols), lambda i: (0, 0)),
        ],
        out_specs=[pl.BlockSpec((tm, hyena_cols), lambda i: (i, 0)),
                   pl.BlockSpec((tm, attn_cols), lambda i: (i, 0))],
        out_shape=[jax.ShapeDtypeStruct((T, hyena_cols), BF16),
                   jax.ShapeDtypeStruct((T, attn_cols), BF16)],
        compiler_params=_cparams(("arbitrary",)),
        name="in_proj",
    )(x2d, g.astype(F32)[None, :], wh, wa)


SCORE_BOUND = 60.0


def _attn_kernel(slope_ref, q_ref, k_ref, v_ref, lq1_ref, lk1_ref, lq2_ref, lk2_ref, sg_ref, o_ref,
                 kmax_ref, *, L, tq, tk, dh, lam_init):
    h = pl.program_id(1)
    qi = pl.program_id(2)
    slope = slope_ref[h]
    q = q_ref[0]
    lane = lax.broadcasted_iota(jnp.int32, q.shape, 1)
    qs = q * jnp.asarray(dh ** -0.5, q.dtype)
    q1 = jnp.where(lane < dh, qs, jnp.zeros_like(qs))
    q2 = jnp.where(lane >= dh, qs, jnp.zeros_like(qs))
    rel = (lax.broadcasted_iota(jnp.int32, (tq, tk), 0)
           - lax.broadcasted_iota(jnp.int32, (tq, tk), 1)).astype(F32)
    q0 = (qi * tq).astype(F32)

    def update(s, m, l, acc, vb):
        m_new = jnp.maximum(m, jnp.max(s, axis=-1, keepdims=True))
        alpha = jnp.exp(m - m_new)
        p = jnp.exp(s - m_new)
        l = alpha * l + jnp.sum(p, axis=-1, keepdims=True)
        acc = alpha * acc + jnp.dot(p.astype(BF16), vb, preferred_element_type=F32)
        return m_new, l, acc

    def body(j, carry):
        m1, l1, a1, m2, l2, a2 = carry
        k0 = pl.multiple_of(j * tk, tk)
        kb = k_ref[0, pl.ds(k0, tk), :]
        vb = v_ref[0, pl.ds(k0, tk), :]
        pen = slope * jnp.abs(rel + (q0 - k0.astype(F32)))
        nt = (((1,), (1,)), ((), ()))
        s1 = lax.dot_general(q1, kb, nt, preferred_element_type=F32) - pen
        s2 = lax.dot_general(q2, kb, nt, preferred_element_type=F32) - pen
        m1, l1, a1 = update(s1, m1, l1, a1, vb)
        m2, l2, a2 = update(s2, m2, l2, a2, vb)
        return m1, l1, a1, m2, l2, a2

    def online_softmax():
        neg = jnp.full((tq, 1), -1e30, F32)
        zero = jnp.zeros((tq, 1), F32)
        zacc = jnp.zeros((tq, 2 * dh), F32)
        m1, l1, a1, m2, l2, a2 = lax.fori_loop(0, L // tk, body, (neg, zero, zacc, neg, zero, zacc))
        return a1 / l1, a2 / l2

    @pl.when(qi == 0)
    def _():
        kmax_ref[...] = jnp.max(jnp.abs(k_ref[0].astype(F32)), axis=0, keepdims=True)

    w = jnp.abs(qs.astype(F32)) * kmax_ref[...]
    ub = jnp.maximum(jnp.sum(jnp.where(lane < dh, w, 0.0), axis=-1, keepdims=True),
                     jnp.sum(jnp.where(lane >= dh, w, 0.0), axis=-1, keepdims=True))
    bounded = jnp.max(ub) <= SCORE_BOUND

    def plain_softmax():
        ones = jnp.ones((tk, 2 * dh), BF16)

        def pbody(j, carry):
            a1, a2 = carry
            k0 = pl.multiple_of(j * tk, tk)
            kb = k_ref[0, pl.ds(k0, tk), :]
            va = jnp.concatenate([v_ref[0, pl.ds(k0, tk), :], ones], axis=1)
            pen = slope * jnp.abs(rel + (q0 - k0.astype(F32)))
            nt = (((1,), (1,)), ((), ()))
            p1 = jnp.exp(lax.dot_general(q1, kb, nt, preferred_element_type=F32) - pen)
            p2 = jnp.exp(lax.dot_general(q2, kb, nt, preferred_element_type=F32) - pen)
            a1 = a1 + jnp.dot(p1.astype(BF16), va, preferred_element_type=F32)
            a2 = a2 + jnp.dot(p2.astype(BF16), va, preferred_element_type=F32)
            return a1, a2

        zacc = jnp.zeros((tq, 4 * dh), F32)
        a1, a2 = lax.fori_loop(0, L // tk, pbody, (zacc, zacc), unroll=4)
        return a1[:, :2 * dh] / a1[:, 2 * dh:], a2[:, :2 * dh] / a2[:, 2 * dh:]

    o1, o2 = lax.cond(bounded, plain_softmax, online_softmax)
    lam = (jnp.exp(jnp.sum(lq1_ref[...] * lk1_ref[...], axis=-1, keepdims=True))
           - jnp.exp(jnp.sum(lq2_ref[...] * lk2_ref[...], axis=-1, keepdims=True)) + lam_init)
    o = o1 - lam * o2
    o = _rms(o, sg_ref[...]) * (1.0 - lam_init)
    o_ref[0] = o.astype(o_ref.dtype)


def _diff_attention(qkv, lq1, lk1, lq2, lk2, subln_g, lam_init, tq, tk):
    B, L, cols = qkv.shape
    W = cols // 3
    H = ATTN_HEADS
    hw = W // H
    dh = hw // 2
    slopes = jnp.asarray([2.0 ** (-8.0 * (i + 1) / H) for i in range(H)], F32)
    vec = lambda a: a.astype(F32)[None, :]
    small = lambda n: pl.BlockSpec((1, n), lambda b, h, i: (0, 0))
    return pl.pallas_call(
        functools.partial(_attn_kernel, L=L, tq=tq, tk=tk, dh=dh, lam_init=lam_init),
        grid=(B, H, L // tq),
        scratch_shapes=[pltpu.VMEM((1, hw), F32)],
        in_specs=[
            pl.BlockSpec(memory_space=pltpu.SMEM),
            pl.BlockSpec((1, tq, hw), lambda b, h, i: (b, i, h)),
            pl.BlockSpec((1, L, hw), lambda b, h, i: (b, 0, H + h)),
            pl.BlockSpec((1, L, hw), lambda b, h, i: (b, 0, 2 * H + h)),
            small(dh), small(dh), small(dh), small(dh), small(hw),
        ],
        out_specs=pl.BlockSpec((1, tq, hw), lambda b, h, i: (b, i, h)),
        out_shape=jax.ShapeDtypeStruct((B, L, W), BF16),
        compiler_params=_cparams(("arbitrary", "arbitrary", "arbitrary")),
        name="diff_attention",
    )(slopes, qkv, qkv, qkv, vec(lq1), vec(lk1), vec(lq2), vec(lk2), vec(subln_g))


def _out_proj_kernel(x_ref, hy_ref, at_ref, wh_ref, wa_ref, g_ref, wr_ref, x2_ref, h2_ref, aff_ref):
    x2 = (x_ref[...] + jnp.dot(hy_ref[...], wh_ref[...], preferred_element_type=F32)
          + jnp.dot(at_ref[...], wa_ref[...], preferred_element_type=F32))
    x2_ref[...] = x2
    h2 = _rms(x2, g_ref[...])
    h2_ref[...] = h2
    wr = wr_ref[...]
    h_hi, w_hi = h2.astype(BF16), wr.astype(BF16)
    h_lo = (h2 - h_hi.astype(F32)).astype(BF16)
    w_lo = (wr - w_hi.astype(F32)).astype(BF16)
    logits = (jnp.dot(h_hi, w_hi, preferred_element_type=F32)
              + (jnp.dot(h_hi, w_lo, preferred_element_type=F32)
                 + jnp.dot(h_lo, w_hi, preferred_element_type=F32)))
    e = jnp.exp(logits - jnp.max(logits, axis=-1, keepdims=True))
    aff_ref[...] = e / jnp.sum(e, axis=-1, keepdims=True)


def _out_proj(x2d, hy2d, at2d, w_out, g, w_router, tm):
    T, D = x2d.shape
    wdt = hy2d.shape[1]
    E = w_router.shape[1]
    wh = w_out[:wdt].astype(BF16)
    wa = w_out[wdt:].astype(BF16)
    row = lambda n: pl.BlockSpec((tm, n), lambda i: (i, 0))
    return pl.pallas_call(
        _out_proj_kernel,
        grid=(T // tm,),
        in_specs=[row(D), row(wdt), row(at2d.shape[1]),
                  pl.BlockSpec(wh.shape, lambda i: (0, 0)), pl.BlockSpec(wa.shape, lambda i: (0, 0)),
                  pl.BlockSpec((1, D), lambda i: (0, 0)), pl.BlockSpec((D, E), lambda i: (0, 0))],
        out_specs=[row(D), row(D), row(E)],
        out_shape=[jax.ShapeDtypeStruct((T, D), F32), jax.ShapeDtypeStruct((T, D), F32),
                   jax.ShapeDtypeStruct((T, E), F32)],
        compiler_params=_cparams(("arbitrary",)),
        name="out_proj_router",
    )(x2d, hy2d, at2d, wh, wa, g.astype(F32)[None, :], w_router.astype(F32))


SEL_CHUNK = 256


def _prefix_count(mask, tri_ref, L):
    parts = []
    carry = jnp.zeros((1, mask.shape[1]), F32)
    for blk in range(L // SEL_CHUNK):
        m = mask[blk * SEL_CHUNK:(blk + 1) * SEL_CHUNK]
        parts.append(jnp.dot(tri_ref[...], m.astype(BF16), preferred_element_type=F32) + carry)
        carry = carry + jnp.sum(m, axis=0, keepdims=True)
    return jnp.concatenate(parts, axis=0)


def _select_kernel(aff_ref, tri_ref, idx_ref, gate_ref, *, L, E, cap):
    aff = aff_ref[0]

    def step(i, prefix):
        cand = prefix | jnp.left_shift(jnp.int32(1), 30 - i)
        cnt = jnp.sum((aff >= pltpu.bitcast(cand, F32)).astype(jnp.int32), axis=0, keepdims=True)
        return jnp.where(cnt >= cap, cand, prefix)

    tau = pltpu.bitcast(lax.fori_loop(0, 31, step, jnp.zeros((1, E), jnp.int32)), F32)
    gt = aff > tau
    eq = aff == tau
    need = cap - jnp.sum(gt.astype(jnp.int32), axis=0, keepdims=True)
    tie_rank = _prefix_count(eq.astype(F32), tri_ref, L) - eq.astype(F32)
    sel = gt | (eq & (tie_rank < need.astype(F32)))
    rank = _prefix_count(sel.astype(F32), tri_ref, L)
    slot = jnp.where(sel, rank - 1.0, -1.0)
    tok = lax.broadcasted_iota(jnp.int32, (L, cap), 0).astype(F32)
    jj = lax.broadcasted_iota(jnp.int32, (L, cap), 1).astype(F32)
    for e in range(E):
        hit = slot[:, e:e + 1] == jj
        idx_ref[0, e:e + 1, :] = jnp.sum(jnp.where(hit, tok, 0.0), axis=0, keepdims=True).astype(jnp.int32)
        gate_ref[0, e:e + 1, :] = jnp.sum(jnp.where(hit, aff[:, e:e + 1], 0.0), axis=0, keepdims=True)


def _select(aff, cap):
    B, L, E = aff.shape
    tri = np.tril(np.ones((SEL_CHUNK, SEL_CHUNK), np.float32))
    return pl.pallas_call(
        functools.partial(_select_kernel, L=L, E=E, cap=cap),
        grid=(B,),
        in_specs=[pl.BlockSpec((1, L, E), lambda b: (b, 0, 0)),
                  pl.BlockSpec(tri.shape, lambda b: (0, 0))],
        out_specs=[pl.BlockSpec((1, E, cap), lambda b: (b, 0, 0)),
                   pl.BlockSpec((1, E, cap), lambda b: (b, 0, 0))],
        out_shape=[jax.ShapeDtypeStruct((B, E, cap), jnp.int32),
                   jax.ShapeDtypeStruct((B, E, cap), F32)],
        compiler_params=_cparams(("arbitrary",)),
        name="expert_choice_select",
    )(aff, jnp.asarray(tri).astype(BF16))


ROW_UNROLL = 8


def _gather_kernel(idx_ref, h_ref, o_ref, buf_ref, *, E, cap):
    b = pl.program_id(0)
    e = pl.program_id(1)
    base = (b * E + e) * cap

    def body(j, c):
        t = idx_ref[base + j]
        buf_ref[pl.ds(j, 1), :] = h_ref[0, pl.ds(t, 1), :]
        return c

    lax.fori_loop(0, cap, body, 0, unroll=ROW_UNROLL)
    o_ref[0, 0] = buf_ref[...].astype(o_ref.dtype)


def _gather(idx_flat, h2, E, cap):
    B, L, D = h2.shape
    return pl.pallas_call(
        functools.partial(_gather_kernel, E=E, cap=cap),
        grid_spec=pltpu.PrefetchScalarGridSpec(
            num_scalar_prefetch=1,
            grid=(B, E),
            in_specs=[pl.BlockSpec((1, L, D), lambda b, e, idx: (b, 0, 0))],
            out_specs=pl.BlockSpec((1, 1, cap, D), lambda b, e, idx: (e, b, 0, 0)),
            scratch_shapes=[pltpu.VMEM((cap, D), F32)],
        ),
        out_shape=jax.ShapeDtypeStruct((E, B, cap, D), BF16),
        compiler_params=_cparams(("arbitrary", "arbitrary")),
        name="expert_gather",
    )(idx_flat, h2)


def _ffn_kernel(x_ref, wg_ref, wu_ref, wd_ref, *rest):
    x = x_ref[0]
    a = jnp.dot(x, wg_ref[0].astype(BF16), preferred_element_type=F32)
    u = jnp.dot(x, wu_ref[0].astype(BF16), preferred_element_type=F32)
    hmid = (a * jax.nn.sigmoid(a) * u).astype(BF16)
    y = jnp.dot(hmid, wd_ref[0].astype(BF16), preferred_element_type=F32)
    if len(rest) == 2:
        y = rest[0][0] + y
    rest[-1][0] = y


def _expert_ffn(xe, w_gate, w_up, w_down, tt, slabs):
    E, T, D = xe.shape
    F = w_gate.shape[2]
    tf = F // slabs
    y = None
    for s in range(slabs):
        tile = pl.BlockSpec((1, tt, D), lambda e, i: (e, i, 0))
        in_specs = [
            tile,
            pl.BlockSpec((1, D, tf), lambda e, i, s=s: (e, 0, s)),
            pl.BlockSpec((1, D, tf), lambda e, i, s=s: (e, 0, s)),
            pl.BlockSpec((1, tf, D), lambda e, i, s=s: (e, s, 0)),
        ]
        args = [xe, w_gate, w_up, w_down]
        if y is not None:
            in_specs.append(tile)
            args.append(y)
        y = pl.pallas_call(
            _ffn_kernel,
            grid=(E, T // tt),
            in_specs=in_specs,
            out_specs=tile,
            out_shape=jax.ShapeDtypeStruct((E, T, D), F32),
            compiler_params=_cparams(("arbitrary", "arbitrary")),
            name=f"expert_ffn_slab{s}",
        )(*args)
    return y


def _scatter_kernel(idx_ref, gate_ref, y_ref, o_ref, *, E, cap):
    b = pl.program_id(0)
    e = pl.program_id(2)
    base = (b * E + e) * cap

    @pl.when(e == 0)
    def _():
        o_ref[...] = jnp.zeros_like(o_ref)

    def body(j, c):
        t = idx_ref[base + j]
        g = gate_ref[base + j]
        o_ref[0, pl.ds(t, 1), :] += g * y_ref[0, 0, pl.ds(j, 1), :]
        return c

    lax.fori_loop(0, cap, body, 0, unroll=ROW_UNROLL)


def _scatter(idx_flat, gate_flat, ye, L, dsplit):
    E, B, cap, D = ye.shape
    dc = D // dsplit
    return pl.pallas_call(
        functools.partial(_scatter_kernel, E=E, cap=cap),
        grid_spec=pltpu.PrefetchScalarGridSpec(
            num_scalar_prefetch=2,
            grid=(B, dsplit, E),
            in_specs=[pl.BlockSpec((1, 1, cap, dc), lambda b, d, e, idx, gt: (e, b, 0, d))],
            out_specs=pl.BlockSpec((1, L, dc), lambda b, d, e, idx, gt: (b, 0, d)),
        ),
        out_shape=jax.ShapeDtypeStruct((B, L, D), F32),
        compiler_params=_cparams(("arbitrary", "arbitrary", "arbitrary")),
        name="expert_scatter_add",
    )(idx_flat, gate_flat, ye)


def _final_kernel(x_ref, m_ref, g_ref, o_ref):
    o_ref[...] = _rms(x_ref[...] + m_ref[...], g_ref[...])


def _final_norm(x2, moe, g, tm):
    T, D = x2.shape
    row = pl.BlockSpec((tm, D), lambda i: (i, 0))
    return pl.pallas_call(
        _final_kernel,
        grid=(T // tm,),
        in_specs=[row, row, pl.BlockSpec((1, D), lambda i: (0, 0))],
        out_specs=row,
        out_shape=jax.ShapeDtypeStruct((T, D), F32),
        compiler_params=_cparams(("arbitrary",)),
        name="final_norm",
    )(x2, moe, g.astype(F32)[None, :])


FFN_SLABS = 2


def _moe(h2, aff, w_gate, w_up, w_down, tt):
    B, L, D = h2.shape
    E = aff.shape[-1]
    cap = EC_CAPACITY_FACTOR * L // E
    idx, gate = _select(aff, cap)
    idx_flat = idx.reshape(-1)
    xe = _gather(idx_flat, h2, E, cap)
    ye = _expert_ffn(xe.reshape(E, B * cap, D), w_gate, w_up, w_down, tt, FFN_SLABS)
    return _scatter(idx_flat, gate.reshape(-1), ye.reshape(E, B, cap, D), L, dsplit=1)


def kernel(x, attn_norm_g, w_in, conv_w, conv_b, filt_w1, filt_b1, filt_w2, filt_b2, filt_w3, filt_freq, hyena_skip, hyena_norm_g, lambda_q1, lambda_k1, lambda_q2, lambda_k2, subln_g, w_out, ffn_norm_g, w_router, w_gate, w_up, w_down, final_norm_g):
    B, L, D = x.shape
    depth = w_in.shape[0]
    width = hyena_skip.shape[-1]
    hyena_cols = (HYENA_ORDER + 1) * width
    E = w_router.shape[-1]
    tm = min(512, B * L)
    tq = min(512, L)
    consts = _dft_constants(L)
    x2d = x.reshape(B * L, D)
    for l in range(depth):
        lam_init = 0.8 - 0.6 * math.exp(-0.3 * l)
        ph, qkv = _in_proj(x2d, attn_norm_g[l], w_in[l], hyena_cols, tm)
        kf = _hyena_filter_spectra(L, width, filt_w1[l], filt_b1[l], filt_w2[l], filt_b2[l],
                                   filt_w3[l], filt_freq[l], consts)
        hy = _hyena_mixer(ph.reshape(B, L, hyena_cols), conv_w[l], conv_b[l], kf, hyena_skip[l],
                          hyena_norm_g[l], consts)
        at = _diff_attention(qkv.reshape(B, L, -1), lambda_q1[l], lambda_k1[l], lambda_q2[l],
                             lambda_k2[l], subln_g[l], lam_init, tq, tq)
        x2, h2, aff = _out_proj(x2d, hy.reshape(B * L, width), at.reshape(B * L, -1), w_out[l],
                                ffn_norm_g[l], w_router[l], tm)
        cap = EC_CAPACITY_FACTOR * L // E
        moe = _moe(h2.reshape(B, L, D), aff.reshape(B, L, E), w_gate[l], w_up[l], w_down[l],
                   tt=min(512, B * cap))
        if l + 1 < depth:
            x2d = x2 + moe.reshape(B * L, D)
    return _final_norm(x2, moe.reshape(B * L, D), final_norm_g, tm).reshape(B, L, D)
```

```python
import functools
import math

import jax
import jax.numpy as jnp
import numpy as np
from jax import lax
from jax.experimental import pallas as pl
from jax.experimental.pallas import tpu as pltpu

F32 = jnp.float32
BF16 = jnp.bfloat16
HIGHEST = lax.Precision.HIGHEST

NORM_EPS = 1e-6
HYENA_ORDER = 2
HYENA_GROUPS = 8
ATTN_HEADS = 4
N_EXPERTS = 16
EC_CAPACITY_FACTOR = 2
FILTER_EMB_DIM = 33
FILTER_EMB_PAD = 40
DECAY_TARGET = 1e-2
FAST_DECAY_PCT = 0.3
SLOW_DECAY_PCT = 1.5

LANES = 128
FFT_FAST = 128
V7X_VMEM_LIMIT = 56 * 1024 * 1024


def _cparams(sem, vmem=V7X_VMEM_LIMIT):
    return pltpu.CompilerParams(dimension_semantics=sem, vmem_limit_bytes=vmem)


SUBLANES = 8


def _dft_constants(L):
    n2 = FFT_FAST
    n = 2 * L
    n1 = n // n2
    s = n1 // 2
    k1 = np.arange(n1)
    th1 = 2.0 * np.pi * np.outer(k1, np.arange(s)) / n1
    f1 = np.empty((2 * n1, s), np.float64)
    f1[0::2], f1[1::2] = np.cos(th1), -np.sin(th1)
    g1 = np.empty((s, 2 * n1), np.float64)
    g1[:, 0::2], g1[:, 1::2] = np.cos(th1.T), -np.sin(th1.T)
    eye = np.eye(SUBLANES)
    fwd1 = np.kron(f1, eye)
    inv1 = np.kron(g1, eye)
    a = np.arange(n2)
    ph = 2.0 * np.pi * (np.outer(a, a)[None] / n2 + (k1[:, None, None] * a[None, None, :]) / n)
    tr, ti = np.cos(ph), -np.sin(ph)
    fwd2 = np.concatenate([np.concatenate([tr, -ti], 2), np.concatenate([ti, tr], 2)], 1)
    pht = np.transpose(ph, (0, 2, 1))
    ur, ui = np.cos(pht) / n, np.sin(pht) / n
    inv2 = np.concatenate([np.concatenate([ur, -ui], 2), np.concatenate([ui, ur], 2)], 1)
    return tuple(jnp.asarray(m.astype(np.float32)).astype(BF16) for m in (fwd1, fwd2, inv2, inv1))


def _filter_hidden_kernel(zf_ref, w1_ref, b1_ref, w2_ref, b2_ref, fq_ref, o_ref):
    fq = fq_ref[...]
    hid = jnp.sin(fq * (jnp.dot(zf_ref[...], w1_ref[...], precision=HIGHEST,
                                preferred_element_type=F32) + b1_ref[...]))
    o_ref[...] = jnp.sin(fq * (jnp.dot(hid, w2_ref[...], precision=HIGHEST,
                                       preferred_element_type=F32) + b2_ref[...]))


def _filter_kernel(hid_ref, w3f_ref, w3b_ref, dl_ref, f1_ref, f2_ref, o_ref,
                   hf_ref, hb_ref, sf_ref, sb_ref, *, L):
    n1 = 2 * L // FFT_FAST
    s = n1 // 2
    hid = hid_ref[...]
    row = lax.broadcasted_iota(jnp.int32, (L, 1), 0)
    t = row.astype(F32) * (1.0 / (L - 1))
    win = jnp.exp(-t * dl_ref[...])
    hf = jnp.dot(hid, w3f_ref[...], precision=HIGHEST, preferred_element_type=F32) * win
    hb = jnp.dot(hid, w3b_ref[...], precision=HIGHEST, preferred_element_type=F32) * win
    hb = jnp.where(row == 0, 0.0, hb)
    norm = (jnp.sum(jnp.abs(hf), axis=0, keepdims=True)
            + jnp.sum(jnp.abs(hb), axis=0, keepdims=True))
    cb = hf.shape[1]
    hf_ref[...] = hf.reshape(s, FFT_FAST, cb)
    hb_ref[...] = hb.reshape(s, FFT_FAST, cb)

    def stage1(g, c):
        f0 = pl.multiple_of(g * SUBLANES, SUBLANES)
        for src, dst in ((hf_ref, sf_ref), (hb_ref, sb_ref)):
            xg = src[:, pl.ds(f0, SUBLANES), :].reshape(s * SUBLANES, cb).astype(BF16)
            a = jnp.dot(f1_ref[...], xg, preferred_element_type=F32)
            dst[:, pl.ds(f0, SUBLANES), :] = a.reshape(2 * n1, SUBLANES, cb)
        return c

    lax.fori_loop(0, FFT_FAST // SUBLANES, stage1, 0)
    inv_norm = 1.0 / norm

    def stage2(k1, c):
        m = f2_ref[k1]
        xf = jnp.dot(m, sf_ref[pl.ds(2 * k1, 2)].reshape(2 * FFT_FAST, cb).astype(BF16),
                     preferred_element_type=F32)
        xb = jnp.dot(m, sb_ref[pl.ds(2 * k1, 2)].reshape(2 * FFT_FAST, cb).astype(BF16),
                     preferred_element_type=F32)
        re = (xf[:FFT_FAST] + xb[:FFT_FAST]) * inv_norm
        im = (xf[FFT_FAST:] - xb[FFT_FAST:]) * inv_norm
        r0 = pl.multiple_of(k1 * 2 * FFT_FAST, 2 * FFT_FAST)
        o_ref[0, pl.ds(r0, 2 * FFT_FAST), :] = jnp.concatenate([re, im], axis=0).astype(o_ref.dtype)
        return c

    lax.fori_loop(0, n1, stage2, 0, unroll=2)


def _hyena_filter_spectra(L, width, w1, b1, w2, b2, w3, freq, consts):
    fwd1, fwd2, _, _ = consts
    n1 = 2 * L // FFT_FAST
    cb = LANES
    ncb = width // cb
    hidden = w1.shape[1]
    tt = np.linspace(0.0, 1.0, L)[:, None]
    ww = 2.0 * np.pi * np.arange(L)[:, None] / L
    nb = (FILTER_EMB_DIM - 1) // 2
    bands = np.linspace(1e-4, nb - 1, nb)[None, :]
    feats = np.concatenate([tt, np.cos(bands * ww), -np.sin(bands * ww)], axis=-1)
    feats = np.pad(feats, ((0, 0), (0, FILTER_EMB_PAD - FILTER_EMB_DIM))).astype(np.float32)
    w1p = jnp.pad(w1.astype(F32), ((0, FILTER_EMB_PAD - FILTER_EMB_DIM), (0, 0)))
    min_decay = math.log(DECAY_TARGET) / SLOW_DECAY_PCT
    max_decay = math.log(DECAY_TARGET) / FAST_DECAY_PCT
    deltas = np.abs(np.linspace(min_decay, max_decay, width)).astype(np.float32)[None, :]

    s = n1 // 2
    one = lambda shape: pl.BlockSpec(shape, lambda i: (0,) * len(shape))
    hid = pl.pallas_call(
        _filter_hidden_kernel,
        grid=(1,),
        in_specs=[one((L, FILTER_EMB_PAD)), one((FILTER_EMB_PAD, hidden)), one((1, hidden)),
                  one((hidden, hidden)), one((1, hidden)), one((1, hidden))],
        out_specs=one((L, hidden)),
        out_shape=jax.ShapeDtypeStruct((L, hidden), F32),
        compiler_params=_cparams(("arbitrary",)),
        name="hyena_filter_hidden",
    )(jnp.asarray(feats), w1p, b1.astype(F32)[None, :], w2.astype(F32), b2.astype(F32)[None, :],
      freq.astype(F32)[None, :])
    full = lambda shape: pl.BlockSpec(shape, lambda n, c: (0,) * len(shape))
    return pl.pallas_call(
        functools.partial(_filter_kernel, L=L),
        grid=(HYENA_ORDER, ncb),
        in_specs=[
            full((L, hidden)),
            pl.BlockSpec((hidden, cb), lambda n, c: (0, n * ncb + c)),
            pl.BlockSpec((hidden, cb), lambda n, c: (0, HYENA_ORDER * ncb + n * ncb + c)),
            pl.BlockSpec((1, cb), lambda n, c: (0, c)),
            full(fwd1.shape), full(fwd2.shape),
        ],
        out_specs=pl.BlockSpec((1, n1 * 2 * FFT_FAST, cb), lambda n, c: (n, 0, c)),
        out_shape=jax.ShapeDtypeStruct((HYENA_ORDER, n1 * 2 * FFT_FAST, width), BF16),
        scratch_shapes=[pltpu.VMEM((s, FFT_FAST, cb), F32), pltpu.VMEM((s, FFT_FAST, cb), F32),
                        pltpu.VMEM((2 * n1, FFT_FAST, cb), F32), pltpu.VMEM((2 * n1, FFT_FAST, cb), F32)],
        compiler_params=_cparams(("arbitrary", "arbitrary")),
        name="hyena_filter_spectra",
    )(hid, w3.astype(F32), w3.astype(F32), jnp.asarray(deltas), fwd1, fwd2)


def _short_conv(p, w, b, L):
    row = lax.broadcasted_iota(jnp.int32, p.shape, 0)
    prev = jnp.where(row == 0, 0.0, pltpu.roll(p, 1, 0))
    nxt = jnp.where(row == L - 1, 0.0, pltpu.roll(p, L - 1, 0))
    return b + prev * w[0:1] + p * w[1:2] + nxt * w[2:3]


def _hyena_kernel(x1_ref, x2_ref, v_ref, w1_ref, w2_ref, wv_ref, b1_ref, b2_ref, bv_ref,
                  skip_ref, ng_ref, kf_ref, f1_ref, f2_ref, f2i_ref, g1_ref, gm_ref,
                  o_ref, z_ref, ga_ref, gb_ref, s_ref, *, L):
    n1 = 2 * L // FFT_FAST
    s = n1 // 2
    cb = o_ref.shape[-1]
    tiled = lambda a: a.reshape(s, FFT_FAST, cb)
    z_ref[...] = tiled(_short_conv(v_ref[0].astype(F32), wv_ref[...], bv_ref[...], L))
    ga_ref[...] = tiled(_short_conv(x1_ref[0].astype(F32), w1_ref[...], b1_ref[...], L))
    gb_ref[...] = tiled(_short_conv(x2_ref[0].astype(F32), w2_ref[...], b2_ref[...], L))
    groups = FFT_FAST // SUBLANES

    for n, gate_ref in enumerate((ga_ref, gb_ref)):
        skip = skip_ref[n:n + 1, :].reshape(1, 1, cb)

        def fwd1(g, c):
            f0 = pl.multiple_of(g * SUBLANES, SUBLANES)
            xg = z_ref[:, pl.ds(f0, SUBLANES), :].reshape(s * SUBLANES, cb).astype(BF16)
            a = jnp.dot(f1_ref[...], xg, preferred_element_type=F32)
            s_ref[:, pl.ds(f0, SUBLANES), :] = a.reshape(2 * n1, SUBLANES, cb)
            return c

        lax.fori_loop(0, groups, fwd1, 0, unroll=8)

        def mid(k1, c):
            blk = s_ref[pl.ds(2 * k1, 2)].reshape(2 * FFT_FAST, cb).astype(BF16)
            x = jnp.dot(f2_ref[k1], blk, preferred_element_type=F32)
            r0 = pl.multiple_of(k1 * 2 * FFT_FAST, 2 * FFT_FAST)
            kk = kf_ref[n, pl.ds(r0, 2 * FFT_FAST), :].astype(F32)
            xr, xi = x[:FFT_FAST], x[FFT_FAST:]
            kr, ki = kk[:FFT_FAST], kk[FFT_FAST:]
            y = jnp.concatenate([xr * kr - xi * ki, xr * ki + xi * kr], axis=0).astype(BF16)
            s_ref[pl.ds(2 * k1, 2)] = jnp.dot(f2i_ref[k1], y, preferred_element_type=F32).reshape(
                2, FFT_FAST, cb)
            return c

        lax.fori_loop(0, n1, mid, 0, unroll=16)

        def inv1(g, c):
            f0 = pl.multiple_of(g * SUBLANES, SUBLANES)
            bg = s_ref[:, pl.ds(f0, SUBLANES), :].reshape(2 * n1 * SUBLANES, cb).astype(BF16)
            y = jnp.dot(g1_ref[...], bg, preferred_element_type=F32).reshape(s, SUBLANES, cb)
            zf = z_ref[:, pl.ds(f0, SUBLANES), :]
            gf = gate_ref[:, pl.ds(f0, SUBLANES), :]
            z_ref[:, pl.ds(f0, SUBLANES), :] = gf * (y + skip * zf)
            return c

        lax.fori_loop(0, groups, inv1, 0, unroll=8)

    z = z_ref[...].reshape(L, cb)
    zz = z * z
    hi = zz.astype(BF16)
    lo = (zz - hi.astype(F32)).astype(BF16)
    ms = (jnp.dot(hi, gm_ref[...], preferred_element_type=F32)
          + jnp.dot(lo, gm_ref[...], preferred_element_type=F32))
    o_ref[0] = (z * lax.rsqrt(ms + NORM_EPS) * ng_ref[...]).astype(o_ref.dtype)


def _hyena_mixer(proj_h, conv_w, conv_b, kf, skip, norm_g, consts):
    B, L, cols = proj_h.shape
    width = cols // (HYENA_ORDER + 1)
    cb = LANES
    ncb = width // cb
    gsz = width // HYENA_GROUPS
    n1 = 2 * L // FFT_FAST
    fwd1, fwd2, inv2, inv1 = consts
    gm = np.kron(np.eye(cb // gsz), np.full((gsz, gsz), 1.0 / gsz)).astype(np.float32)

    def stream(i):
        return pl.BlockSpec((1, L, cb), lambda c, b: (b, 0, i * ncb + c))

    def cvec(rows, i):
        return pl.BlockSpec((rows, cb), lambda c, b: (0, i * ncb + c))

    def const(shape):
        return pl.BlockSpec(shape, lambda c, b: (0,) * len(shape), pipeline_mode=pl.Buffered(1))

    cw = conv_w.astype(F32)
    cbias = conv_b.astype(F32)[None, :]
    return pl.pallas_call(
        functools.partial(_hyena_kernel, L=L),
        grid=(ncb, B),
        in_specs=[
            stream(0), stream(1), stream(2),
            cvec(3, 0), cvec(3, 1), cvec(3, 2),
            cvec(1, 0), cvec(1, 1), cvec(1, 2),
            pl.BlockSpec((HYENA_ORDER, cb), lambda c, b: (0, c)),
            pl.BlockSpec((1, cb), lambda c, b: (0, c)),
            pl.BlockSpec((HYENA_ORDER, n1 * 2 * FFT_FAST, cb), lambda c, b: (0, 0, c),
                         pipeline_mode=pl.Buffered(1)),
            const(fwd1.shape), const(fwd2.shape), const(inv2.shape), const(inv1.shape),
            const(gm.shape),
        ],
        out_specs=pl.BlockSpec((1, L, cb), lambda c, b: (b, 0, c)),
        out_shape=jax.ShapeDtypeStruct((B, L, width), BF16),
        scratch_shapes=[pltpu.VMEM((n1 // 2, FFT_FAST, cb), F32), pltpu.VMEM((n1 // 2, FFT_FAST, cb), F32),
                        pltpu.VMEM((n1 // 2, FFT_FAST, cb), F32), pltpu.VMEM((2 * n1, FFT_FAST, cb), F32)],
        compiler_params=_cparams(("arbitrary", "arbitrary")),
        name="hyena_mixer",
    )(proj_h, proj_h, proj_h, cw, cw, cw, cbias, cbias, cbias,
      skip.astype(F32), norm_g.astype(F32)[None, :], kf,
      fwd1, fwd2, inv2, inv1, jnp.asarray(gm).astype(BF16))


def _rms(x, g):
    return x * lax.rsqrt(jnp.mean(x * x, axis=-1, keepdims=True) + NORM_EPS) * g


def _in_proj_kernel(x_ref, g_ref, wh_ref, wa_ref, oh_ref, oa_ref):
    hn = _rms(x_ref[...], g_ref[...]).astype(BF16)
    oh_ref[...] = jnp.dot(hn, wh_ref[...], preferred_element_type=F32).astype(oh_ref.dtype)
    oa_ref[...] = jnp.dot(hn, wa_ref[...], preferred_element_type=F32).astype(oa_ref.dtype)


def _in_proj(x2d, g, w_in, hyena_cols, tm):
    T, D = x2d.shape
    attn_cols = w_in.shape[1] - hyena_cols
    wh = w_in[:, :hyena_cols].astype(BF16)
    wa = w_in[:, hyena_cols:].astype(BF16)
    return pl.pallas_call(
        _in_proj_kernel,
        grid=(T // tm,),
        in_specs=[
            pl.BlockSpec((tm, D), lambda i: (i, 0)),
            pl.BlockSpec((1, D), lambda i: (0, 0)),
            pl.BlockSpec((D, hyena_cols), lambda i: (0, 0)),
            pl.BlockSpec((D, attn_cols), lambda i: (0, 0)),
        ],
        out_specs=[pl.BlockSpec((tm, hyena_cols), lambda i: (i, 0)),
                   pl.BlockSpec((tm, attn_cols), lambda i: (i, 0))],
        out_shape=[jax.ShapeDtypeStruct((T, hyena_cols), BF16),
                   jax.ShapeDtypeStruct((T, attn_cols), BF16)],
        compiler_params=_cparams(("arbitrary",)),
        name="in_proj",
    )(x2d, g.astype(F32)[None, :], wh, wa)


SCORE_BOUND = 60.0


def _attn_kernel(slope_ref, q_ref, k_ref, v_ref, lq1_ref, lk1_ref, lq2_ref, lk2_ref, sg_ref, o_ref,
                 kmax_ref, *, L, tq, tk, dh, lam_init):
    h = pl.program_id(1)
    qi = pl.program_id(2)
    slope = slope_ref[h]
    q = q_ref[0]
    lane = lax.broadcasted_iota(jnp.int32, q.shape, 1)
    qs = q * jnp.asarray(dh ** -0.5, q.dtype)
    q1 = jnp.where(lane < dh, qs, jnp.zeros_like(qs))
    q2 = jnp.where(lane >= dh, qs, jnp.zeros_like(qs))
    rel = (lax.broadcasted_iota(jnp.int32, (tq, tk), 0)
           - lax.broadcasted_iota(jnp.int32, (tq, tk), 1)).astype(F32)
    q0 = (qi * tq).astype(F32)

    def update(s, m, l, acc, vb):
        m_new = jnp.maximum(m, jnp.max(s, axis=-1, keepdims=True))
        alpha = jnp.exp(m - m_new)
        p = jnp.exp(s - m_new)
        l = alpha * l + jnp.sum(p, axis=-1, keepdims=True)
        acc = alpha * acc + jnp.dot(p.astype(BF16), vb, preferred_element_type=F32)
        return m_new, l, acc

    def body(j, carry):
        m1, l1, a1, m2, l2, a2 = carry
        k0 = pl.multiple_of(j * tk, tk)
        kb = k_ref[0, pl.ds(k0, tk), :]
        vb = v_ref[0, pl.ds(k0, tk), :]
        pen = slope * jnp.abs(rel + (q0 - k0.astype(F32)))
        nt = (((1,), (1,)), ((), ()))
        s1 = lax.dot_general(q1, kb, nt, preferred_element_type=F32) - pen
        s2 = lax.dot_general(q2, kb, nt, preferred_element_type=F32) - pen
        m1, l1, a1 = update(s1, m1, l1, a1, vb)
        m2, l2, a2 = update(s2, m2, l2, a2, vb)
        return m1, l1, a1, m2, l2, a2

    def online_softmax():
        neg = jnp.full((tq, 1), -1e30, F32)
        zero = jnp.zeros((tq, 1), F32)
        zacc = jnp.zeros((tq, 2 * dh), F32)
        m1, l1, a1, m2, l2, a2 = lax.fori_loop(0, L // tk, body, (neg, zero, zacc, neg, zero, zacc))
        return a1 / l1, a2 / l2

    @pl.when(qi == 0)
    def _():
        kmax_ref[...] = jnp.max(jnp.abs(k_ref[0].astype(F32)), axis=0, keepdims=True)

    w = jnp.abs(qs.astype(F32)) * kmax_ref[...]
    ub = jnp.maximum(jnp.sum(jnp.where(lane < dh, w, 0.0), axis=-1, keepdims=True),
                     jnp.sum(jnp.where(lane >= dh, w, 0.0), axis=-1, keepdims=True))
    bounded = jnp.max(ub) <= SCORE_BOUND

    def plain_softmax():
        ones = jnp.ones((tk, 2 * dh), BF16)

        nt = (((1,), (1,)), ((), ()))
        a1 = a2 = None
        for j in range(L // tk):
            kb = k_ref[0, j * tk:(j + 1) * tk, :]
            va = jnp.concatenate([v_ref[0, j * tk:(j + 1) * tk, :], ones], axis=1)
            pen = slope * jnp.abs(rel + (q0 - float(j * tk)))
            p1 = jnp.exp(lax.dot_general(q1, kb, nt, preferred_element_type=F32) - pen)
            p2 = jnp.exp(lax.dot_general(q2, kb, nt, preferred_element_type=F32) - pen)
            d1 = jnp.dot(p1.astype(BF16), va, preferred_element_type=F32)
            d2 = jnp.dot(p2.astype(BF16), va, preferred_element_type=F32)
            a1 = d1 if a1 is None else a1 + d1
            a2 = d2 if a2 is None else a2 + d2
        return a1[:, :2 * dh] / a1[:, 2 * dh:], a2[:, :2 * dh] / a2[:, 2 * dh:]

    o1, o2 = lax.cond(bounded, plain_softmax, online_softmax)
    lam = (jnp.exp(jnp.sum(lq1_ref[...] * lk1_ref[...], axis=-1, keepdims=True))
           - jnp.exp(jnp.sum(lq2_ref[...] * lk2_ref[...], axis=-1, keepdims=True)) + lam_init)
    o = o1 - lam * o2
    o = _rms(o, sg_ref[...]) * (1.0 - lam_init)
    o_ref[0] = o.astype(o_ref.dtype)


def _diff_attention(qkv, lq1, lk1, lq2, lk2, subln_g, lam_init, tq, tk):
    B, L, cols = qkv.shape
    W = cols // 3
    H = ATTN_HEADS
    hw = W // H
    dh = hw // 2
    slopes = jnp.asarray([2.0 ** (-8.0 * (i + 1) / H) for i in range(H)], F32)
    vec = lambda a: a.astype(F32)[None, :]
    small = lambda n: pl.BlockSpec((1, n), lambda b, h, i: (0, 0))
    return pl.pallas_call(
        functools.partial(_attn_kernel, L=L, tq=tq, tk=tk, dh=dh, lam_init=lam_init),
        grid=(B, H, L // tq),
        scratch_shapes=[pltpu.VMEM((1, hw), F32)],
        in_specs=[
            pl.BlockSpec(memory_space=pltpu.SMEM),
            pl.BlockSpec((1, tq, hw), lambda b, h, i: (b, i, h)),
            pl.BlockSpec((1, L, hw), lambda b, h, i: (b, 0, H + h)),
            pl.BlockSpec((1, L, hw), lambda b, h, i: (b, 0, 2 * H + h)),
            small(dh), small(dh), small(dh), small(dh), small(hw),
        ],
        out_specs=pl.BlockSpec((1, tq, hw), lambda b, h, i: (b, i, h)),
        out_shape=jax.ShapeDtypeStruct((B, L, W), BF16),
        compiler_params=_cparams(("arbitrary", "arbitrary", "arbitrary")),
        name="diff_attention",
    )(slopes, qkv, qkv, qkv, vec(lq1), vec(lk1), vec(lq2), vec(lk2), vec(subln_g))


def _out_proj_kernel(x_ref, hy_ref, at_ref, wh_ref, wa_ref, g_ref, wr_ref, x2_ref, h2_ref, aff_ref):
    x2 = (x_ref[...] + jnp.dot(hy_ref[...], wh_ref[...], preferred_element_type=F32)
          + jnp.dot(at_ref[...], wa_ref[...], preferred_element_type=F32))
    x2_ref[...] = x2
    h2 = _rms(x2, g_ref[...])
    h2_ref[...] = h2
    wr = wr_ref[...]
    h_hi, w_hi = h2.astype(BF16), wr.astype(BF16)
    h_lo = (h2 - h_hi.astype(F32)).astype(BF16)
    w_lo = (wr - w_hi.astype(F32)).astype(BF16)
    logits = (jnp.dot(h_hi, w_hi, preferred_element_type=F32)
              + (jnp.dot(h_hi, w_lo, preferred_element_type=F32)
                 + jnp.dot(h_lo, w_hi, preferred_element_type=F32)))
    e = jnp.exp(logits - jnp.max(logits, axis=-1, keepdims=True))
    aff_ref[...] = e / jnp.sum(e, axis=-1, keepdims=True)


def _out_proj(x2d, hy2d, at2d, w_out, g, w_router, tm):
    T, D = x2d.shape
    wdt = hy2d.shape[1]
    E = w_router.shape[1]
    wh = w_out[:wdt].astype(BF16)
    wa = w_out[wdt:].astype(BF16)
    row = lambda n: pl.BlockSpec((tm, n), lambda i: (i, 0))
    return pl.pallas_call(
        _out_proj_kernel,
        grid=(T // tm,),
        in_specs=[row(D), row(wdt), row(at2d.shape[1]),
                  pl.BlockSpec(wh.shape, lambda i: (0, 0)), pl.BlockSpec(wa.shape, lambda i: (0, 0)),
                  pl.BlockSpec((1, D), lambda i: (0, 0)), pl.BlockSpec((D, E), lambda i: (0, 0))],
        out_specs=[row(D), row(D), row(E)],
        out_shape=[jax.ShapeDtypeStruct((T, D), F32), jax.ShapeDtypeStruct((T, D), F32),
                   jax.ShapeDtypeStruct((T, E), F32)],
        compiler_params=_cparams(("arbitrary",)),
        name="out_proj_router",
    )(x2d, hy2d, at2d, wh, wa, g.astype(F32)[None, :], w_router.astype(F32))


SEL_CHUNK = 256


def _prefix_count(mask, tri_ref, L):
    parts = []
    carry = jnp.zeros((1, mask.shape[1]), F32)
    for blk in range(L // SEL_CHUNK):
        m = mask[blk * SEL_CHUNK:(blk + 1) * SEL_CHUNK]
        parts.append(jnp.dot(tri_ref[...], m.astype(BF16), preferred_element_type=F32) + carry)
        carry = carry + jnp.sum(m, axis=0, keepdims=True)
    return jnp.concatenate(parts, axis=0)


def _select_kernel(aff_ref, tri_ref, idx_ref, gate_ref, *, L, E, cap):
    aff = aff_ref[0]

    def step(i, prefix):
        cand = prefix | jnp.left_shift(jnp.int32(1), 30 - i)
        cnt = jnp.sum((aff >= pltpu.bitcast(cand, F32)).astype(jnp.int32), axis=0, keepdims=True)
        return jnp.where(cnt >= cap, cand, prefix)

    tau = pltpu.bitcast(lax.fori_loop(0, 31, step, jnp.zeros((1, E), jnp.int32)), F32)
    gt = aff > tau
    eq = aff == tau
    need = cap - jnp.sum(gt.astype(jnp.int32), axis=0, keepdims=True)
    tie_rank = _prefix_count(eq.astype(F32), tri_ref, L) - eq.astype(F32)
    sel = gt | (eq & (tie_rank < need.astype(F32)))
    rank = _prefix_count(sel.astype(F32), tri_ref, L)
    slot = jnp.where(sel, rank - 1.0, -1.0)
    tok = lax.broadcasted_iota(jnp.int32, (L, cap), 0).astype(F32)
    jj = lax.broadcasted_iota(jnp.int32, (L, cap), 1).astype(F32)
    for e in range(E):
        hit = slot[:, e:e + 1] == jj
        idx_ref[0, e:e + 1, :] = jnp.sum(jnp.where(hit, tok, 0.0), axis=0, keepdims=True).astype(jnp.int32)
        gate_ref[0, e:e + 1, :] = jnp.sum(jnp.where(hit, aff[:, e:e + 1], 0.0), axis=0, keepdims=True)


def _select(aff, cap):
    B, L, E = aff.shape
    tri = np.tril(np.ones((SEL_CHUNK, SEL_CHUNK), np.float32))
    return pl.pallas_call(
        functools.partial(_select_kernel, L=L, E=E, cap=cap),
        grid=(B,),
        in_specs=[pl.BlockSpec((1, L, E), lambda b: (b, 0, 0)),
                  pl.BlockSpec(tri.shape, lambda b: (0, 0))],
        out_specs=[pl.BlockSpec((1, E, cap), lambda b: (b, 0, 0)),
                   pl.BlockSpec((1, E, cap), lambda b: (b, 0, 0))],
        out_shape=[jax.ShapeDtypeStruct((B, E, cap), jnp.int32),
                   jax.ShapeDtypeStruct((B, E, cap), F32)],
        compiler_params=_cparams(("arbitrary",)),
        name="expert_choice_select",
    )(aff, jnp.asarray(tri).astype(BF16))


ROW_UNROLL = 8


def _gather_kernel(idx_ref, h_ref, o_ref, buf_ref, *, E, cap):
    b = pl.program_id(0)
    e = pl.program_id(1)
    base = (b * E + e) * cap

    def body(j, c):
        t = idx_ref[base + j]
        buf_ref[pl.ds(j, 1), :] = h_ref[0, pl.ds(t, 1), :]
        return c

    lax.fori_loop(0, cap, body, 0, unroll=ROW_UNROLL)
    o_ref[0, 0] = buf_ref[...].astype(o_ref.dtype)


def _gather(idx_flat, h2, E, cap):
    B, L, D = h2.shape
    return pl.pallas_call(
        functools.partial(_gather_kernel, E=E, cap=cap),
        grid_spec=pltpu.PrefetchScalarGridSpec(
            num_scalar_prefetch=1,
            grid=(B, E),
            in_specs=[pl.BlockSpec((1, L, D), lambda b, e, idx: (b, 0, 0))],
            out_specs=pl.BlockSpec((1, 1, cap, D), lambda b, e, idx: (e, b, 0, 0)),
            scratch_shapes=[pltpu.VMEM((cap, D), F32)],
        ),
        out_shape=jax.ShapeDtypeStruct((E, B, cap, D), BF16),
        compiler_params=_cparams(("arbitrary", "arbitrary")),
        name="expert_gather",
    )(idx_flat, h2)


MXU_TILE = 256


def _ffn_kernel(x_ref, *refs, nchunk):
    wg, wu, wd = refs[:nchunk], refs[nchunk:2 * nchunk], refs[2 * nchunk:3 * nchunk]
    rest = refs[3 * nchunk:]
    x = x_ref[0]
    y = rest[0][0] if len(rest) == 2 else None
    for k in range(nchunk):
        a = jnp.dot(x, wg[k][0].astype(BF16), preferred_element_type=F32)
        u = jnp.dot(x, wu[k][0].astype(BF16), preferred_element_type=F32)
        hmid = (a * jax.nn.sigmoid(a) * u).astype(BF16)
        d = jnp.dot(hmid, wd[k][0].astype(BF16), preferred_element_type=F32)
        y = d if y is None else y + d
    rest[-1][0] = y


def _expert_ffn(xe, w_gate, w_up, w_down, tt, slabs):
    E, T, D = xe.shape
    nchunks = w_gate.shape[2] // MXU_TILE
    per = -(-nchunks // slabs)
    y = None
    for s in range(slabs):
        chunks = range(s * per, min((s + 1) * per, nchunks))
        tile = pl.BlockSpec((1, tt, D), lambda e, i: (e, i, 0))
        col = [pl.BlockSpec((1, D, MXU_TILE), lambda e, i, c=c: (e, 0, c)) for c in chunks]
        row = [pl.BlockSpec((1, MXU_TILE, D), lambda e, i, c=c: (e, c, 0)) for c in chunks]
        n = len(chunks)
        in_specs = [tile] + col + col + row
        args = [xe] + [w_gate] * n + [w_up] * n + [w_down] * n
        if y is not None:
            in_specs.append(tile)
            args.append(y)
        y = pl.pallas_call(
            functools.partial(_ffn_kernel, nchunk=n),
            grid=(E, T // tt),
            in_specs=in_specs,
            out_specs=tile,
            out_shape=jax.ShapeDtypeStruct((E, T, D), F32),
            compiler_params=_cparams(("arbitrary", "arbitrary")),
            name=f"expert_ffn_slab{s}",
        )(*args)
    return y


FIN_ROWS = 512


def _scatter_kernel(idx_ref, gate_ref, y_ref, x_ref, g_ref, o_ref, *, E, cap, L, final):
    b = pl.program_id(0)
    e = pl.program_id(1)
    base = (b * E + e) * cap

    @pl.when(e == 0)
    def _():
        o_ref[...] = jnp.zeros_like(o_ref)

    def body(j, c):
        t = idx_ref[base + j]
        g = gate_ref[base + j]
        o_ref[0, pl.ds(t, 1), :] += g * y_ref[0, 0, pl.ds(j, 1), :]
        return c

    lax.fori_loop(0, cap, body, 0, unroll=ROW_UNROLL)

    @pl.when(e == E - 1)
    def _():
        def fin(r, c):
            rows = pl.ds(pl.multiple_of(r * FIN_ROWS, FIN_ROWS), FIN_ROWS)
            v = x_ref[0, rows, :] + o_ref[0, rows, :]
            o_ref[0, rows, :] = _rms(v, g_ref[...]) if final else v
            return c

        lax.fori_loop(0, L // FIN_ROWS, fin, 0)


def _scatter(idx_flat, gate_flat, ye, x2, norm_g, final):
    E, B, cap, D = ye.shape
    L = x2.shape[1]
    return pl.pallas_call(
        functools.partial(_scatter_kernel, E=E, cap=cap, L=L, final=final),
        grid_spec=pltpu.PrefetchScalarGridSpec(
            num_scalar_prefetch=2,
            grid=(B, E),
            in_specs=[pl.BlockSpec((1, 1, cap, D), lambda b, e, idx, gt: (e, b, 0, 0)),
                      pl.BlockSpec((1, L, D), lambda b, e, idx, gt: (b, 0, 0),
                                   pipeline_mode=pl.Buffered(1)),
                      pl.BlockSpec((1, D), lambda b, e, idx, gt: (0, 0))],
            out_specs=pl.BlockSpec((1, L, D), lambda b, e, idx, gt: (b, 0, 0)),
        ),
        out_shape=jax.ShapeDtypeStruct((B, L, D), F32),
        compiler_params=_cparams(("arbitrary", "arbitrary")),
        name="expert_scatter_add",
    )(idx_flat, gate_flat, ye, x2, norm_g.astype(F32)[None, :])


FFN_SLABS = 2


def _moe(x2, h2, aff, w_gate, w_up, w_down, norm_g, tt, final):
    B, L, D = h2.shape
    E = aff.shape[-1]
    cap = EC_CAPACITY_FACTOR * L // E
    idx, gate = _select(aff, cap)
    idx_flat = idx.reshape(-1)
    xe = _gather(idx_flat, h2, E, cap)
    ye = _expert_ffn(xe.reshape(E, B * cap, D), w_gate, w_up, w_down, tt, FFN_SLABS)
    return _scatter(idx_flat, gate.reshape(-1), ye.reshape(E, B, cap, D), x2, norm_g, final)


def kernel(x, attn_norm_g, w_in, conv_w, conv_b, filt_w1, filt_b1, filt_w2, filt_b2, filt_w3, filt_freq, hyena_skip, hyena_norm_g, lambda_q1, lambda_k1, lambda_q2, lambda_k2, subln_g, w_out, ffn_norm_g, w_router, w_gate, w_up, w_down, final_norm_g):
    B, L, D = x.shape
    depth = w_in.shape[0]
    width = hyena_skip.shape[-1]
    hyena_cols = (HYENA_ORDER + 1) * width
    E = w_router.shape[-1]
    tm = min(512, B * L)
    tq = min(512, L)
    consts = _dft_constants(L)
    x2d = x.reshape(B * L, D)
    for l in range(depth):
        lam_init = 0.8 - 0.6 * math.exp(-0.3 * l)
        ph, qkv = _in_proj(x2d, attn_norm_g[l], w_in[l], hyena_cols, tm)
        kf = _hyena_filter_spectra(L, width, filt_w1[l], filt_b1[l], filt_w2[l], filt_b2[l],
                                   filt_w3[l], filt_freq[l], consts)
        hy = _hyena_mixer(ph.reshape(B, L, hyena_cols), conv_w[l], conv_b[l], kf, hyena_skip[l],
                          hyena_norm_g[l], consts)
        at = _diff_attention(qkv.reshape(B, L, -1), lambda_q1[l], lambda_k1[l], lambda_q2[l],
                             lambda_k2[l], subln_g[l], lam_init, tq, tq)
        x2, h2, aff = _out_proj(x2d, hy.reshape(B * L, width), at.reshape(B * L, -1), w_out[l],
                                ffn_norm_g[l], w_router[l], tm)
        cap = EC_CAPACITY_FACTOR * L // E
        x2d = _moe(x2.reshape(B, L, D), h2.reshape(B, L, D), aff.reshape(B, L, E), w_gate[l], w_up[l],
                   w_down[l], final_norm_g, tt=min(512, B * cap), final=l + 1 == depth).reshape(B * L, D)
    return x2d.reshape(B, L, D)
```

```python
import functools
import math

import jax
import jax.numpy as jnp
import numpy as np
from jax import lax
from jax.experimental import pallas as pl
from jax.experimental.pallas import tpu as pltpu

F32 = jnp.float32
BF16 = jnp.bfloat16
HIGHEST = lax.Precision.HIGHEST

NORM_EPS = 1e-6
HYENA_ORDER = 2
HYENA_GROUPS = 8
ATTN_HEADS = 4
N_EXPERTS = 16
EC_CAPACITY_FACTOR = 2
FILTER_EMB_DIM = 33
FILTER_EMB_PAD = 40
DECAY_TARGET = 1e-2
FAST_DECAY_PCT = 0.3
SLOW_DECAY_PCT = 1.5

LANES = 128
FFT_FAST = 128
V7X_VMEM_LIMIT = 56 * 1024 * 1024


def _cparams(sem, vmem=V7X_VMEM_LIMIT):
    return pltpu.CompilerParams(dimension_semantics=sem, vmem_limit_bytes=vmem)


SUBLANES = 8


def _dft_constants(L):
    n2 = FFT_FAST
    n = 2 * L
    n1 = n // n2
    s = n1 // 2
    k1 = np.arange(n1)
    th1 = 2.0 * np.pi * np.outer(k1, np.arange(s)) / n1
    f1 = np.empty((2 * n1, s), np.float64)
    f1[0::2], f1[1::2] = np.cos(th1), -np.sin(th1)
    g1 = np.empty((s, 2 * n1), np.float64)
    g1[:, 0::2], g1[:, 1::2] = np.cos(th1.T), -np.sin(th1.T)
    eye = np.eye(SUBLANES)
    fwd1 = np.kron(f1, eye)
    inv1 = np.kron(g1, eye)
    a = np.arange(n2)
    ph = 2.0 * np.pi * (np.outer(a, a)[None] / n2 + (k1[:, None, None] * a[None, None, :]) / n)
    tr, ti = np.cos(ph), -np.sin(ph)
    fwd2 = np.concatenate([np.concatenate([tr, -ti], 2), np.concatenate([ti, tr], 2)], 1)
    pht = np.transpose(ph, (0, 2, 1))
    ur, ui = np.cos(pht) / n, np.sin(pht) / n
    inv2 = np.concatenate([np.concatenate([ur, -ui], 2), np.concatenate([ui, ur], 2)], 1)
    return tuple(jnp.asarray(m.astype(np.float32)).astype(BF16) for m in (fwd1, fwd2, inv2, inv1))


def _filter_hidden_kernel(zf_ref, w1_ref, b1_ref, w2_ref, b2_ref, fq_ref, o_ref):
    fq = fq_ref[...]
    hid = jnp.sin(fq * (jnp.dot(zf_ref[...], w1_ref[...], precision=HIGHEST,
                                preferred_element_type=F32) + b1_ref[...]))
    o_ref[...] = jnp.sin(fq * (jnp.dot(hid, w2_ref[...], precision=HIGHEST,
                                       preferred_element_type=F32) + b2_ref[...]))


def _filter_kernel(hid_ref, w3f_ref, w3b_ref, dl_ref, f1_ref, f2_ref, o_ref,
                   hf_ref, hb_ref, sf_ref, sb_ref, *, L):
    n1 = 2 * L // FFT_FAST
    s = n1 // 2
    hid = hid_ref[...]
    row = lax.broadcasted_iota(jnp.int32, (L, 1), 0)
    t = row.astype(F32) * (1.0 / (L - 1))
    win = jnp.exp(-t * dl_ref[...])
    hf = jnp.dot(hid, w3f_ref[...], precision=HIGHEST, preferred_element_type=F32) * win
    hb = jnp.dot(hid, w3b_ref[...], precision=HIGHEST, preferred_element_type=F32) * win
    hb = jnp.where(row == 0, 0.0, hb)
    norm = (jnp.sum(jnp.abs(hf), axis=0, keepdims=True)
            + jnp.sum(jnp.abs(hb), axis=0, keepdims=True))
    cb = hf.shape[1]
    hf_ref[...] = hf.reshape(s, FFT_FAST, cb)
    hb_ref[...] = hb.reshape(s, FFT_FAST, cb)

    def stage1(g, c):
        f0 = pl.multiple_of(g * SUBLANES, SUBLANES)
        xg = jnp.concatenate(
            [hf_ref[:, pl.ds(f0, SUBLANES), :].reshape(s * SUBLANES, cb),
             hb_ref[:, pl.ds(f0, SUBLANES), :].reshape(s * SUBLANES, cb)], axis=1).astype(BF16)
        a = jnp.dot(f1_ref[...], xg, preferred_element_type=F32)
        sf_ref[:, pl.ds(f0, SUBLANES), :] = a[:, :cb].reshape(2 * n1, SUBLANES, cb)
        sb_ref[:, pl.ds(f0, SUBLANES), :] = a[:, cb:].reshape(2 * n1, SUBLANES, cb)
        return c

    lax.fori_loop(0, FFT_FAST // SUBLANES, stage1, 0, unroll=4)
    inv_norm = 1.0 / norm

    def stage2(k1, c):
        blk = jnp.concatenate(
            [sf_ref[pl.ds(2 * k1, 2)].reshape(2 * FFT_FAST, cb),
             sb_ref[pl.ds(2 * k1, 2)].reshape(2 * FFT_FAST, cb)], axis=1).astype(BF16)
        x = jnp.dot(f2_ref[k1], blk, preferred_element_type=F32)
        xf, xb = x[:, :cb], x[:, cb:]
        re = (xf[:FFT_FAST] + xb[:FFT_FAST]) * inv_norm
        im = (xf[FFT_FAST:] - xb[FFT_FAST:]) * inv_norm
        r0 = pl.multiple_of(k1 * 2 * FFT_FAST, 2 * FFT_FAST)
        o_ref[0, pl.ds(r0, 2 * FFT_FAST), :] = jnp.concatenate([re, im], axis=0).astype(o_ref.dtype)
        return c

    lax.fori_loop(0, n1, stage2, 0, unroll=8)


def _hyena_filter_spectra(L, width, w1, b1, w2, b2, w3, freq, consts):
    fwd1, fwd2, _, _ = consts
    n1 = 2 * L // FFT_FAST
    cb = LANES
    ncb = width // cb
    hidden = w1.shape[1]
    tt = np.linspace(0.0, 1.0, L)[:, None]
    ww = 2.0 * np.pi * np.arange(L)[:, None] / L
    nb = (FILTER_EMB_DIM - 1) // 2
    bands = np.linspace(1e-4, nb - 1, nb)[None, :]
    feats = np.concatenate([tt, np.cos(bands * ww), -np.sin(bands * ww)], axis=-1)
    feats = np.pad(feats, ((0, 0), (0, FILTER_EMB_PAD - FILTER_EMB_DIM))).astype(np.float32)
    w1p = jnp.pad(w1.astype(F32), ((0, FILTER_EMB_PAD - FILTER_EMB_DIM), (0, 0)))
    min_decay = math.log(DECAY_TARGET) / SLOW_DECAY_PCT
    max_decay = math.log(DECAY_TARGET) / FAST_DECAY_PCT
    deltas = np.abs(np.linspace(min_decay, max_decay, width)).astype(np.float32)[None, :]

    s = n1 // 2
    one = lambda shape: pl.BlockSpec(shape, lambda i: (0,) * len(shape))
    hid = pl.pallas_call(
        _filter_hidden_kernel,
        grid=(1,),
        in_specs=[one((L, FILTER_EMB_PAD)), one((FILTER_EMB_PAD, hidden)), one((1, hidden)),
                  one((hidden, hidden)), one((1, hidden)), one((1, hidden))],
        out_specs=one((L, hidden)),
        out_shape=jax.ShapeDtypeStruct((L, hidden), F32),
        compiler_params=_cparams(("arbitrary",)),
        name="hyena_filter_hidden",
    )(jnp.asarray(feats), w1p, b1.astype(F32)[None, :], w2.astype(F32), b2.astype(F32)[None, :],
      freq.astype(F32)[None, :])
    full = lambda shape: pl.BlockSpec(shape, lambda n, c: (0,) * len(shape))
    return pl.pallas_call(
        functools.partial(_filter_kernel, L=L),
        grid=(HYENA_ORDER, ncb),
        in_specs=[
            full((L, hidden)),
            pl.BlockSpec((hidden, cb), lambda n, c: (0, n * ncb + c)),
            pl.BlockSpec((hidden, cb), lambda n, c: (0, HYENA_ORDER * ncb + n * ncb + c)),
            pl.BlockSpec((1, cb), lambda n, c: (0, c)),
            full(fwd1.shape), full(fwd2.shape),
        ],
        out_specs=pl.BlockSpec((1, n1 * 2 * FFT_FAST, cb), lambda n, c: (n, 0, c)),
        out_shape=jax.ShapeDtypeStruct((HYENA_ORDER, n1 * 2 * FFT_FAST, width), BF16),
        scratch_shapes=[pltpu.VMEM((s, FFT_FAST, cb), F32), pltpu.VMEM((s, FFT_FAST, cb), F32),
                        pltpu.VMEM((2 * n1, FFT_FAST, cb), F32), pltpu.VMEM((2 * n1, FFT_FAST, cb), F32)],
        compiler_params=_cparams(("arbitrary", "arbitrary")),
        name="hyena_filter_spectra",
    )(hid, w3.astype(F32), w3.astype(F32), jnp.asarray(deltas), fwd1, fwd2)


def _short_conv(p, w, b, L):
    row = lax.broadcasted_iota(jnp.int32, p.shape, 0)
    prev = jnp.where(row == 0, 0.0, pltpu.roll(p, 1, 0))
    nxt = jnp.where(row == L - 1, 0.0, pltpu.roll(p, L - 1, 0))
    return b + prev * w[0:1] + p * w[1:2] + nxt * w[2:3]


def _hyena_kernel(x1_ref, x2_ref, v_ref, w1_ref, w2_ref, wv_ref, b1_ref, b2_ref, bv_ref,
                  skip_ref, ng_ref, kf_ref, f1_ref, f2_ref, f2i_ref, g1_ref, gm_ref,
                  o_ref, z_ref, ga_ref, gb_ref, s_ref, *, L):
    n1 = 2 * L // FFT_FAST
    s = n1 // 2
    cb = o_ref.shape[-1]
    tiled = lambda a: a.reshape(s, FFT_FAST, cb)
    z_ref[...] = tiled(_short_conv(v_ref[0].astype(F32), wv_ref[...], bv_ref[...], L))
    ga_ref[...] = tiled(_short_conv(x1_ref[0].astype(F32), w1_ref[...], b1_ref[...], L))
    gb_ref[...] = tiled(_short_conv(x2_ref[0].astype(F32), w2_ref[...], b2_ref[...], L))
    groups = FFT_FAST // SUBLANES

    for n, gate_ref in enumerate((ga_ref, gb_ref)):
        skip = skip_ref[n:n + 1, :].reshape(1, 1, cb)

        def fwd1(gp, c):
            fa = pl.multiple_of(gp * 2 * SUBLANES, 2 * SUBLANES)
            fb = fa + SUBLANES
            xg = jnp.concatenate(
                [z_ref[:, pl.ds(fa, SUBLANES), :].reshape(s * SUBLANES, cb),
                 z_ref[:, pl.ds(fb, SUBLANES), :].reshape(s * SUBLANES, cb)], axis=1).astype(BF16)
            a = jnp.dot(f1_ref[...], xg, preferred_element_type=F32)
            s_ref[:, pl.ds(fa, SUBLANES), :] = a[:, :cb].reshape(2 * n1, SUBLANES, cb)
            s_ref[:, pl.ds(fb, SUBLANES), :] = a[:, cb:].reshape(2 * n1, SUBLANES, cb)
            return c

        lax.fori_loop(0, groups // 2, fwd1, 0, unroll=4)

        def mid(k1, c):
            blk = s_ref[pl.ds(2 * k1, 2)].reshape(2 * FFT_FAST, cb).astype(BF16)
            x = jnp.dot(f2_ref[k1], blk, preferred_element_type=F32)
            r0 = pl.multiple_of(k1 * 2 * FFT_FAST, 2 * FFT_FAST)
            kk = kf_ref[n, pl.ds(r0, 2 * FFT_FAST), :].astype(F32)
            xr, xi = x[:FFT_FAST], x[FFT_FAST:]
            kr, ki = kk[:FFT_FAST], kk[FFT_FAST:]
            y = jnp.concatenate([xr * kr - xi * ki, xr * ki + xi * kr], axis=0).astype(BF16)
            s_ref[pl.ds(2 * k1, 2)] = jnp.dot(f2i_ref[k1], y, preferred_element_type=F32).reshape(
                2, FFT_FAST, cb)
            return c

        lax.fori_loop(0, n1, mid, 0, unroll=16)

        def inv1(gp, c):
            fa = pl.multiple_of(gp * 2 * SUBLANES, 2 * SUBLANES)
            fb = fa + SUBLANES
            bg = jnp.concatenate(
                [s_ref[:, pl.ds(fa, SUBLANES), :].reshape(2 * n1 * SUBLANES, cb),
                 s_ref[:, pl.ds(fb, SUBLANES), :].reshape(2 * n1 * SUBLANES, cb)], axis=1).astype(BF16)
            y = jnp.dot(g1_ref[...], bg, preferred_element_type=F32)
            for f0, yh in ((fa, y[:, :cb]), (fb, y[:, cb:])):
                zf = z_ref[:, pl.ds(f0, SUBLANES), :]
                gf = gate_ref[:, pl.ds(f0, SUBLANES), :]
                z_ref[:, pl.ds(f0, SUBLANES), :] = gf * (yh.reshape(s, SUBLANES, cb) + skip * zf)
            return c

        lax.fori_loop(0, groups // 2, inv1, 0, unroll=4)

    z = z_ref[...].reshape(L, cb)
    zz = z * z
    hi = zz.astype(BF16)
    lo = (zz - hi.astype(F32)).astype(BF16)
    ms = (jnp.dot(hi, gm_ref[...], preferred_element_type=F32)
          + jnp.dot(lo, gm_ref[...], preferred_element_type=F32))
    o_ref[0] = (z * lax.rsqrt(ms + NORM_EPS) * ng_ref[...]).astype(o_ref.dtype)


def _hyena_mixer(proj_h, conv_w, conv_b, kf, skip, norm_g, consts):
    B, L, cols = proj_h.shape
    width = cols // (HYENA_ORDER + 1)
    cb = LANES
    ncb = width // cb
    gsz = width // HYENA_GROUPS
    n1 = 2 * L // FFT_FAST
    fwd1, fwd2, inv2, inv1 = consts
    gm = np.kron(np.eye(cb // gsz), np.full((gsz, gsz), 1.0 / gsz)).astype(np.float32)

    def stream(i):
        return pl.BlockSpec((1, L, cb), lambda c, b: (b, 0, i * ncb + c))

    def cvec(rows, i):
        return pl.BlockSpec((rows, cb), lambda c, b: (0, i * ncb + c))

    def const(shape):
        return pl.BlockSpec(shape, lambda c, b: (0,) * len(shape), pipeline_mode=pl.Buffered(1))

    cw = conv_w.astype(F32)
    cbias = conv_b.astype(F32)[None, :]
    return pl.pallas_call(
        functools.partial(_hyena_kernel, L=L),
        grid=(ncb, B),
        in_specs=[
            stream(0), stream(1), stream(2),
            cvec(3, 0), cvec(3, 1), cvec(3, 2),
            cvec(1, 0), cvec(1, 1), cvec(1, 2),
            pl.BlockSpec((HYENA_ORDER, cb), lambda c, b: (0, c)),
            pl.BlockSpec((1, cb), lambda c, b: (0, c)),
            pl.BlockSpec((HYENA_ORDER, n1 * 2 * FFT_FAST, cb), lambda c, b: (0, 0, c),
                         pipeline_mode=pl.Buffered(1)),
            const(fwd1.shape), const(fwd2.shape), const(inv2.shape), const(inv1.shape),
            const(gm.shape),
        ],
        out_specs=pl.BlockSpec((1, L, cb), lambda c, b: (b, 0, c)),
        out_shape=jax.ShapeDtypeStruct((B, L, width), BF16),
        scratch_shapes=[pltpu.VMEM((n1 // 2, FFT_FAST, cb), F32), pltpu.VMEM((n1 // 2, FFT_FAST, cb), F32),
                        pltpu.VMEM((n1 // 2, FFT_FAST, cb), F32), pltpu.VMEM((2 * n1, FFT_FAST, cb), F32)],
        compiler_params=_cparams(("arbitrary", "arbitrary")),
        name="hyena_mixer",
    )(proj_h, proj_h, proj_h, cw, cw, cw, cbias, cbias, cbias,
      skip.astype(F32), norm_g.astype(F32)[None, :], kf,
      fwd1, fwd2, inv2, inv1, jnp.asarray(gm).astype(BF16))


def _rms(x, g):
    return x * lax.rsqrt(jnp.mean(x * x, axis=-1, keepdims=True) + NORM_EPS) * g


def _in_proj_kernel(x_ref, g_ref, wh_ref, wa_ref, oh_ref, oa_ref):
    hn = _rms(x_ref[...], g_ref[...]).astype(BF16)
    oh_ref[...] = jnp.dot(hn, wh_ref[...], preferred_element_type=F32).astype(oh_ref.dtype)
    oa_ref[...] = jnp.dot(hn, wa_ref[...], preferred_element_type=F32).astype(oa_ref.dtype)


def _in_proj(x2d, g, w_in, hyena_cols, tm):
    T, D = x2d.shape
    attn_cols = w_in.shape[1] - hyena_cols
    wh = w_in[:, :hyena_cols].astype(BF16)
    wa = w_in[:, hyena_cols:].astype(BF16)
    return pl.pallas_call(
        _in_proj_kernel,
        grid=(T // tm,),
        in_specs=[
            pl.BlockSpec((tm, D), lambda i: (i, 0)),
            pl.BlockSpec((1, D), lambda i: (0, 0)),
            pl.BlockSpec((D, hyena_cols), lambda i: (0, 0)),
            pl.BlockSpec((D, attn_cols), lambda i: (0, 0)),
        ],
        out_specs=[pl.BlockSpec((tm, hyena_cols), lambda i: (i, 0)),
                   pl.BlockSpec((tm, attn_cols), lambda i: (i, 0))],
        out_shape=[jax.ShapeDtypeStruct((T, hyena_cols), BF16),
                   jax.ShapeDtypeStruct((T, attn_cols), BF16)],
        compiler_params=_cparams(("arbitrary",)),
        name="in_proj",
    )(x2d, g.astype(F32)[None, :], wh, wa)


SCORE_BOUND = 60.0


def _attn_kernel(slope_ref, q_ref, k_ref, v_ref, lq1_ref, lk1_ref, lq2_ref, lk2_ref, sg_ref, o_ref,
                 kmax_ref, *, L, tq, tk, dh, lam_init):
    h = pl.program_id(1)
    qi = pl.program_id(2)
    slope = slope_ref[h]
    q = q_ref[0]
    lane = lax.broadcasted_iota(jnp.int32, q.shape, 1)
    qs = q * jnp.asarray(dh ** -0.5, q.dtype)
    q1 = jnp.where(lane < dh, qs, jnp.zeros_like(qs))
    q2 = jnp.where(lane >= dh, qs, jnp.zeros_like(qs))
    rel = (lax.broadcasted_iota(jnp.int32, (tq, tk), 0)
           - lax.broadcasted_iota(jnp.int32, (tq, tk), 1)).astype(F32)
    q0 = (qi * tq).astype(F32)

    def update(s, m, l, acc, vb):
        m_new = jnp.maximum(m, jnp.max(s, axis=-1, keepdims=True))
        alpha = jnp.exp(m - m_new)
        p = jnp.exp(s - m_new)
        l = alpha * l + jnp.sum(p, axis=-1, keepdims=True)
        acc = alpha * acc + jnp.dot(p.astype(BF16), vb, preferred_element_type=F32)
        return m_new, l, acc

    def body(j, carry):
        m1, l1, a1, m2, l2, a2 = carry
        k0 = pl.multiple_of(j * tk, tk)
        kb = k_ref[0, pl.ds(k0, tk), :]
        vb = v_ref[0, pl.ds(k0, tk), :]
        pen = slope * jnp.abs(rel + (q0 - k0.astype(F32)))
        nt = (((1,), (1,)), ((), ()))
        s1 = lax.dot_general(q1, kb, nt, preferred_element_type=F32) - pen
        s2 = lax.dot_general(q2, kb, nt, preferred_element_type=F32) - pen
        m1, l1, a1 = update(s1, m1, l1, a1, vb)
        m2, l2, a2 = update(s2, m2, l2, a2, vb)
        return m1, l1, a1, m2, l2, a2

    def online_softmax():
        neg = jnp.full((tq, 1), -1e30, F32)
        zero = jnp.zeros((tq, 1), F32)
        zacc = jnp.zeros((tq, 2 * dh), F32)
        m1, l1, a1, m2, l2, a2 = lax.fori_loop(0, L // tk, body, (neg, zero, zacc, neg, zero, zacc))
        return a1 / l1, a2 / l2

    @pl.when(qi == 0)
    def _():
        kmax_ref[...] = jnp.max(jnp.abs(k_ref[0].astype(F32)), axis=0, keepdims=True)

    w = jnp.abs(qs.astype(F32)) * kmax_ref[...]
    ub = jnp.maximum(jnp.sum(jnp.where(lane < dh, w, 0.0), axis=-1, keepdims=True),
                     jnp.sum(jnp.where(lane >= dh, w, 0.0), axis=-1, keepdims=True))
    bounded = jnp.max(ub) <= SCORE_BOUND

    def plain_softmax():
        ones = jnp.ones((tk, 2 * dh), BF16)

        nt = (((1,), (1,)), ((), ()))
        a1 = a2 = None
        for j in range(L // tk):
            kb = k_ref[0, j * tk:(j + 1) * tk, :]
            va = jnp.concatenate([v_ref[0, j * tk:(j + 1) * tk, :], ones], axis=1)
            pen = slope * jnp.abs(rel + (q0 - float(j * tk)))
            p1 = jnp.exp(lax.dot_general(q1, kb, nt, preferred_element_type=F32) - pen)
            p2 = jnp.exp(lax.dot_general(q2, kb, nt, preferred_element_type=F32) - pen)
            d1 = jnp.dot(p1.astype(BF16), va, preferred_element_type=F32)
            d2 = jnp.dot(p2.astype(BF16), va, preferred_element_type=F32)
            a1 = d1 if a1 is None else a1 + d1
            a2 = d2 if a2 is None else a2 + d2
        return a1[:, :2 * dh] / a1[:, 2 * dh:], a2[:, :2 * dh] / a2[:, 2 * dh:]

    o1, o2 = lax.cond(bounded, plain_softmax, online_softmax)
    lam = (jnp.exp(jnp.sum(lq1_ref[...] * lk1_ref[...], axis=-1, keepdims=True))
           - jnp.exp(jnp.sum(lq2_ref[...] * lk2_ref[...], axis=-1, keepdims=True)) + lam_init)
    o = o1 - lam * o2
    o = _rms(o, sg_ref[...]) * (1.0 - lam_init)
    o_ref[0] = o.astype(o_ref.dtype)


def _diff_attention(qkv, lq1, lk1, lq2, lk2, subln_g, lam_init, tq, tk):
    B, L, cols = qkv.shape
    W = cols // 3
    H = ATTN_HEADS
    hw = W // H
    dh = hw // 2
    slopes = jnp.asarray([2.0 ** (-8.0 * (i + 1) / H) for i in range(H)], F32)
    vec = lambda a: a.astype(F32)[None, :]
    small = lambda n: pl.BlockSpec((1, n), lambda b, h, i: (0, 0))
    return pl.pallas_call(
        functools.partial(_attn_kernel, L=L, tq=tq, tk=tk, dh=dh, lam_init=lam_init),
        grid=(B, H, L // tq),
        scratch_shapes=[pltpu.VMEM((1, hw), F32)],
        in_specs=[
            pl.BlockSpec(memory_space=pltpu.SMEM),
            pl.BlockSpec((1, tq, hw), lambda b, h, i: (b, i, h)),
            pl.BlockSpec((1, L, hw), lambda b, h, i: (b, 0, H + h)),
            pl.BlockSpec((1, L, hw), lambda b, h, i: (b, 0, 2 * H + h)),
            small(dh), small(dh), small(dh), small(dh), small(hw),
        ],
        out_specs=pl.BlockSpec((1, tq, hw), lambda b, h, i: (b, i, h)),
        out_shape=jax.ShapeDtypeStruct((B, L, W), BF16),
        compiler_params=_cparams(("arbitrary", "arbitrary", "arbitrary")),
        name="diff_attention",
    )(slopes, qkv, qkv, qkv, vec(lq1), vec(lk1), vec(lq2), vec(lk2), vec(subln_g))


def _out_proj_kernel(x_ref, hy_ref, at_ref, wh_ref, wa_ref, g_ref, wr_ref, x2_ref, h2_ref, aff_ref):
    x2 = (x_ref[...] + jnp.dot(hy_ref[...], wh_ref[...], preferred_element_type=F32)
          + jnp.dot(at_ref[...], wa_ref[...], preferred_element_type=F32))
    x2_ref[...] = x2
    h2 = _rms(x2, g_ref[...])
    h2_ref[...] = h2
    wr = wr_ref[...]
    h_hi, w_hi = h2.astype(BF16), wr.astype(BF16)
    h_lo = (h2 - h_hi.astype(F32)).astype(BF16)
    w_lo = (wr - w_hi.astype(F32)).astype(BF16)
    logits = (jnp.dot(h_hi, w_hi, preferred_element_type=F32)
              + (jnp.dot(h_hi, w_lo, preferred_element_type=F32)
                 + jnp.dot(h_lo, w_hi, preferred_element_type=F32)))
    e = jnp.exp(logits - jnp.max(logits, axis=-1, keepdims=True))
    aff_ref[...] = e / jnp.sum(e, axis=-1, keepdims=True)


def _out_proj(x2d, hy2d, at2d, w_out, g, w_router, tm):
    T, D = x2d.shape
    wdt = hy2d.shape[1]
    E = w_router.shape[1]
    wh = w_out[:wdt].astype(BF16)
    wa = w_out[wdt:].astype(BF16)
    row = lambda n: pl.BlockSpec((tm, n), lambda i: (i, 0))
    return pl.pallas_call(
        _out_proj_kernel,
        grid=(T // tm,),
        in_specs=[row(D), row(wdt), row(at2d.shape[1]),
                  pl.BlockSpec(wh.shape, lambda i: (0, 0)), pl.BlockSpec(wa.shape, lambda i: (0, 0)),
                  pl.BlockSpec((1, D), lambda i: (0, 0)), pl.BlockSpec((D, E), lambda i: (0, 0))],
        out_specs=[row(D), row(D), row(E)],
        out_shape=[jax.ShapeDtypeStruct((T, D), F32), jax.ShapeDtypeStruct((T, D), F32),
                   jax.ShapeDtypeStruct((T, E), F32)],
        compiler_params=_cparams(("arbitrary",)),
        name="out_proj_router",
    )(x2d, hy2d, at2d, wh, wa, g.astype(F32)[None, :], w_router.astype(F32))


SEL_CHUNK = 256


def _prefix_count(mask, tri_ref, L):
    parts = []
    carry = jnp.zeros((1, mask.shape[1]), F32)
    for blk in range(L // SEL_CHUNK):
        m = mask[blk * SEL_CHUNK:(blk + 1) * SEL_CHUNK]
        parts.append(jnp.dot(tri_ref[...], m.astype(BF16), preferred_element_type=F32) + carry)
        carry = carry + jnp.sum(m, axis=0, keepdims=True)
    return jnp.concatenate(parts, axis=0)


def _select_kernel(aff_ref, tri_ref, idx_ref, gate_ref, *, L, E, cap):
    aff = aff_ref[0]

    def step(i, prefix):
        cand = prefix | jnp.left_shift(jnp.int32(1), 30 - i)
        cnt = jnp.sum((aff >= pltpu.bitcast(cand, F32)).astype(jnp.int32), axis=0, keepdims=True)
        return jnp.where(cnt >= cap, cand, prefix)

    tau = pltpu.bitcast(lax.fori_loop(0, 31, step, jnp.zeros((1, E), jnp.int32)), F32)
    gt = aff > tau
    eq = aff == tau
    need = cap - jnp.sum(gt.astype(jnp.int32), axis=0, keepdims=True)
    tie_rank = _prefix_count(eq.astype(F32), tri_ref, L) - eq.astype(F32)
    sel = gt | (eq & (tie_rank < need.astype(F32)))
    rank = _prefix_count(sel.astype(F32), tri_ref, L)
    slot = jnp.where(sel, rank - 1.0, -1.0)
    tok = lax.broadcasted_iota(jnp.int32, (L, cap), 0).astype(F32)
    jj = lax.broadcasted_iota(jnp.int32, (L, cap), 1).astype(F32)
    for e in range(E):
        hit = slot[:, e:e + 1] == jj
        idx_ref[0, e:e + 1, :] = jnp.sum(jnp.where(hit, tok, 0.0), axis=0, keepdims=True).astype(jnp.int32)
        gate_ref[0, e:e + 1, :] = jnp.sum(jnp.where(hit, aff[:, e:e + 1], 0.0), axis=0, keepdims=True)


def _select(aff, cap):
    B, L, E = aff.shape
    tri = np.tril(np.ones((SEL_CHUNK, SEL_CHUNK), np.float32))
    return pl.pallas_call(
        functools.partial(_select_kernel, L=L, E=E, cap=cap),
        grid=(B,),
        in_specs=[pl.BlockSpec((1, L, E), lambda b: (b, 0, 0)),
                  pl.BlockSpec(tri.shape, lambda b: (0, 0))],
        out_specs=[pl.BlockSpec((1, E, cap), lambda b: (b, 0, 0)),
                   pl.BlockSpec((1, E, cap), lambda b: (b, 0, 0))],
        out_shape=[jax.ShapeDtypeStruct((B, E, cap), jnp.int32),
                   jax.ShapeDtypeStruct((B, E, cap), F32)],
        compiler_params=_cparams(("arbitrary",)),
        name="expert_choice_select",
    )(aff, jnp.asarray(tri).astype(BF16))


ROW_UNROLL = 8


def _gather_kernel(idx_ref, h_ref, o_ref, buf_ref, *, E, cap):
    b = pl.program_id(0)
    e = pl.program_id(1)
    base = (b * E + e) * cap

    def body(i, c):
        for u in range(SUBLANES):
            t = idx_ref[base + i * SUBLANES + u]
            buf_ref[i, pl.ds(u, 1), :] = h_ref[0, pl.ds(t, 1), :]
        return c

    lax.fori_loop(0, cap // SUBLANES, body, 0)
    o_ref[0, 0] = buf_ref[...].reshape(cap, buf_ref.shape[-1]).astype(o_ref.dtype)


def _gather(idx_flat, h2, E, cap):
    B, L, D = h2.shape
    return pl.pallas_call(
        functools.partial(_gather_kernel, E=E, cap=cap),
        grid_spec=pltpu.PrefetchScalarGridSpec(
            num_scalar_prefetch=1,
            grid=(B, E),
            in_specs=[pl.BlockSpec((1, L, D), lambda b, e, idx: (b, 0, 0))],
            out_specs=pl.BlockSpec((1, 1, cap, D), lambda b, e, idx: (e, b, 0, 0)),
            scratch_shapes=[pltpu.VMEM((cap // SUBLANES, SUBLANES, D), F32)],
        ),
        out_shape=jax.ShapeDtypeStruct((E, B, cap, D), BF16),
        compiler_params=_cparams(("arbitrary", "arbitrary")),
        name="expert_gather",
    )(idx_flat, h2)


MXU_TILE = 256


def _ffn_kernel(x_ref, *refs, nchunk):
    wg, wu, wd = refs[:nchunk], refs[nchunk:2 * nchunk], refs[2 * nchunk:3 * nchunk]
    rest = refs[3 * nchunk:]
    x = x_ref[0]
    y = rest[0][0] if len(rest) == 2 else None
    for k in range(nchunk):
        a = jnp.dot(x, wg[k][0].astype(BF16), preferred_element_type=F32)
        u = jnp.dot(x, wu[k][0].astype(BF16), preferred_element_type=F32)
        hmid = (a * jax.nn.sigmoid(a) * u).astype(BF16)
        d = jnp.dot(hmid, wd[k][0].astype(BF16), preferred_element_type=F32)
        y = d if y is None else y + d
    rest[-1][0] = y


def _expert_ffn(xe, w_gate, w_up, w_down, tt, slabs):
    E, T, D = xe.shape
    nchunks = w_gate.shape[2] // MXU_TILE
    per = -(-nchunks // slabs)
    y = None
    for s in range(slabs):
        chunks = range(s * per, min((s + 1) * per, nchunks))
        tile = pl.BlockSpec((1, tt, D), lambda e, i: (e, i, 0))
        col = [pl.BlockSpec((1, D, MXU_TILE), lambda e, i, c=c: (e, 0, c)) for c in chunks]
        row = [pl.BlockSpec((1, MXU_TILE, D), lambda e, i, c=c: (e, c, 0)) for c in chunks]
        n = len(chunks)
        in_specs = [tile] + col + col + row
        args = [xe] + [w_gate] * n + [w_up] * n + [w_down] * n
        if y is not None:
            in_specs.append(tile)
            args.append(y)
        y = pl.pallas_call(
            functools.partial(_ffn_kernel, nchunk=n),
            grid=(E, T // tt),
            in_specs=in_specs,
            out_specs=tile,
            out_shape=jax.ShapeDtypeStruct((E, T, D), F32),
            compiler_params=_cparams(("arbitrary", "arbitrary")),
            name=f"expert_ffn_slab{s}",
        )(*args)
    return y


FIN_ROWS = 512


def _scatter_kernel(idx_ref, gate_ref, y_ref, x_ref, g_ref, o_ref, *, E, cap, L, final):
    b = pl.program_id(0)
    e = pl.program_id(1)
    base = (b * E + e) * cap

    @pl.when(e == 0)
    def _():
        o_ref[...] = jnp.zeros_like(o_ref)

    def body(i, c):
        j0 = i * ROW_UNROLL
        toks = [idx_ref[base + j0 + u] for u in range(ROW_UNROLL)]
        vals = [o_ref[0, pl.ds(toks[u], 1), :]
                + gate_ref[base + j0 + u] * y_ref[0, 0, i, pl.ds(u, 1), :] for u in range(ROW_UNROLL)]
        for u in range(ROW_UNROLL):
            o_ref[0, pl.ds(toks[u], 1), :] = vals[u]
        return c

    lax.fori_loop(0, cap // ROW_UNROLL, body, 0)

    @pl.when(e == E - 1)
    def _():
        def fin(r, c):
            rows = pl.ds(pl.multiple_of(r * FIN_ROWS, FIN_ROWS), FIN_ROWS)
            v = x_ref[0, rows, :] + o_ref[0, rows, :]
            o_ref[0, rows, :] = _rms(v, g_ref[...]) if final else v
            return c

        lax.fori_loop(0, L // FIN_ROWS, fin, 0)


def _scatter(idx_flat, gate_flat, ye, x2, norm_g, final):
    E, B, cap, D = ye.shape
    L = x2.shape[1]
    return pl.pallas_call(
        functools.partial(_scatter_kernel, E=E, cap=cap, L=L, final=final),
        grid_spec=pltpu.PrefetchScalarGridSpec(
            num_scalar_prefetch=2,
            grid=(B, E),
            in_specs=[pl.BlockSpec((1, 1, cap // ROW_UNROLL, ROW_UNROLL, D),
                                   lambda b, e, idx, gt: (e, b, 0, 0, 0)),
                      pl.BlockSpec((1, L, D), lambda b, e, idx, gt: (b, 0, 0),
                                   pipeline_mode=pl.Buffered(1)),
                      pl.BlockSpec((1, D), lambda b, e, idx, gt: (0, 0))],
            out_specs=pl.BlockSpec((1, L, D), lambda b, e, idx, gt: (b, 0, 0)),
        ),
        out_shape=jax.ShapeDtypeStruct((B, L, D), F32),
        compiler_params=_cparams(("arbitrary", "arbitrary")),
        name="expert_scatter_add",
    )(idx_flat, gate_flat, ye.reshape(E, B, cap // ROW_UNROLL, ROW_UNROLL, D), x2,
      norm_g.astype(F32)[None, :])


FFN_SLABS = 2


def _moe(x2, h2, aff, w_gate, w_up, w_down, norm_g, tt, final):
    B, L, D = h2.shape
    E = aff.shape[-1]
    cap = EC_CAPACITY_FACTOR * L // E
    idx, gate = _select(aff, cap)
    idx_flat = idx.reshape(-1)
    xe = _gather(idx_flat, h2, E, cap)
    ye = _expert_ffn(xe.reshape(E, B * cap, D), w_gate, w_up, w_down, tt, FFN_SLABS)
    return _scatter(idx_flat, gate.reshape(-1), ye.reshape(E, B, cap, D), x2, norm_g, final)


def kernel(x, attn_norm_g, w_in, conv_w, conv_b, filt_w1, filt_b1, filt_w2, filt_b2, filt_w3, filt_freq, hyena_skip, hyena_norm_g, lambda_q1, lambda_k1, lambda_q2, lambda_k2, subln_g, w_out, ffn_norm_g, w_router, w_gate, w_up, w_down, final_norm_g):
    B, L, D = x.shape
    depth = w_in.shape[0]
    width = hyena_skip.shape[-1]
    hyena_cols = (HYENA_ORDER + 1) * width
    E = w_router.shape[-1]
    tm = min(512, B * L)
    tq = min(512, L)
    consts = _dft_constants(L)
    x2d = x.reshape(B * L, D)
    for l in range(depth):
        lam_init = 0.8 - 0.6 * math.exp(-0.3 * l)
        ph, qkv = _in_proj(x2d, attn_norm_g[l], w_in[l], hyena_cols, tm)
        kf = _hyena_filter_spectra(L, width, filt_w1[l], filt_b1[l], filt_w2[l], filt_b2[l],
                                   filt_w3[l], filt_freq[l], consts)
        hy = _hyena_mixer(ph.reshape(B, L, hyena_cols), conv_w[l], conv_b[l], kf, hyena_skip[l],
                          hyena_norm_g[l], consts)
        at = _diff_attention(qkv.reshape(B, L, -1), lambda_q1[l], lambda_k1[l], lambda_q2[l],
                             lambda_k2[l], subln_g[l], lam_init, tq, tq)
        x2, h2, aff = _out_proj(x2d, hy.reshape(B * L, width), at.reshape(B * L, -1), w_out[l],
                                ffn_norm_g[l], w_router[l], tm // 2)
        cap = EC_CAPACITY_FACTOR * L // E
        x2d = _moe(x2.reshape(B, L, D), h2.reshape(B, L, D), aff.reshape(B, L, E), w_gate[l], w_up[l],
                   w_down[l], final_norm_g, tt=min(512, B * cap), final=l + 1 == depth).reshape(B * L, D)
    return x2d.reshape(B, L, D)
```

```python
import functools
import math

import jax
import jax.numpy as jnp
import numpy as np
from jax import lax
from jax.experimental import pallas as pl
from jax.experimental.pallas import tpu as pltpu

F32 = jnp.float32
BF16 = jnp.bfloat16
HIGHEST = lax.Precision.HIGHEST

NORM_EPS = 1e-6
HYENA_ORDER = 2
HYENA_GROUPS = 8
ATTN_HEADS = 4
N_EXPERTS = 16
EC_CAPACITY_FACTOR = 2
FILTER_EMB_DIM = 33
FILTER_EMB_PAD = 40
DECAY_TARGET = 1e-2
FAST_DECAY_PCT = 0.3
SLOW_DECAY_PCT = 1.5

LANES = 128
FFT_FAST = 128
V7X_VMEM_LIMIT = 56 * 1024 * 1024


def _cparams(sem, vmem=V7X_VMEM_LIMIT):
    return pltpu.CompilerParams(dimension_semantics=sem, vmem_limit_bytes=vmem)


SUBLANES = 8


def _dft_constants(L):
    n2 = FFT_FAST
    n = 2 * L
    n1 = n // n2
    s = n1 // 2
    k1 = np.arange(n1)
    th1 = 2.0 * np.pi * np.outer(k1, np.arange(s)) / n1
    f1 = np.empty((2 * n1, s), np.float64)
    f1[0::2], f1[1::2] = np.cos(th1), -np.sin(th1)
    g1 = np.empty((s, 2 * n1), np.float64)
    g1[:, 0::2], g1[:, 1::2] = np.cos(th1.T), -np.sin(th1.T)
    eye = np.eye(SUBLANES)
    fwd1 = np.kron(f1, eye)
    inv1 = np.kron(g1, eye)
    a = np.arange(n2)
    ph = 2.0 * np.pi * (np.outer(a, a)[None] / n2 + (k1[:, None, None] * a[None, None, :]) / n)
    tr, ti = np.cos(ph), -np.sin(ph)
    fwd2 = np.concatenate([np.concatenate([tr, -ti], 2), np.concatenate([ti, tr], 2)], 1)
    pht = np.transpose(ph, (0, 2, 1))
    ur, ui = np.cos(pht) / n, np.sin(pht) / n
    inv2 = np.concatenate([np.concatenate([ur, -ui], 2), np.concatenate([ui, ur], 2)], 1)
    return tuple(jnp.asarray(m.astype(np.float32)).astype(BF16) for m in (fwd1, fwd2, inv2, inv1))


def _filter_hidden_kernel(zf_ref, w1_ref, b1_ref, w2_ref, b2_ref, fq_ref, o_ref):
    fq = fq_ref[...]
    hid = jnp.sin(fq * (jnp.dot(zf_ref[...], w1_ref[...], precision=HIGHEST,
                                preferred_element_type=F32) + b1_ref[...]))
    o_ref[...] = jnp.sin(fq * (jnp.dot(hid, w2_ref[...], precision=HIGHEST,
                                       preferred_element_type=F32) + b2_ref[...]))


def _filter_kernel(hid_ref, w3f_ref, w3b_ref, dl_ref, f1_ref, f2_ref, o_ref,
                   hf_ref, hb_ref, sf_ref, sb_ref, *, L):
    n1 = 2 * L // FFT_FAST
    s = n1 // 2
    hid = hid_ref[...]
    row = lax.broadcasted_iota(jnp.int32, (L, 1), 0)
    t = row.astype(F32) * (1.0 / (L - 1))
    win = jnp.exp(-t * dl_ref[...])
    hf = jnp.dot(hid, w3f_ref[...], precision=HIGHEST, preferred_element_type=F32) * win
    hb = jnp.dot(hid, w3b_ref[...], precision=HIGHEST, preferred_element_type=F32) * win
    hb = jnp.where(row == 0, 0.0, hb)
    norm = (jnp.sum(jnp.abs(hf), axis=0, keepdims=True)
            + jnp.sum(jnp.abs(hb), axis=0, keepdims=True))
    cb = hf.shape[1]
    hf_ref[...] = hf.reshape(s, FFT_FAST, cb)
    hb_ref[...] = hb.reshape(s, FFT_FAST, cb)

    def stage1(g, c):
        f0 = pl.multiple_of(g * SUBLANES, SUBLANES)
        xg = jnp.concatenate(
            [hf_ref[:, pl.ds(f0, SUBLANES), :].reshape(s * SUBLANES, cb),
             hb_ref[:, pl.ds(f0, SUBLANES), :].reshape(s * SUBLANES, cb)], axis=1).astype(BF16)
        a = jnp.dot(f1_ref[...], xg, preferred_element_type=F32)
        sf_ref[:, pl.ds(f0, SUBLANES), :] = a[:, :cb].reshape(2 * n1, SUBLANES, cb)
        sb_ref[:, pl.ds(f0, SUBLANES), :] = a[:, cb:].reshape(2 * n1, SUBLANES, cb)
        return c

    lax.fori_loop(0, FFT_FAST // SUBLANES, stage1, 0, unroll=4)
    inv_norm = 1.0 / norm

    def stage2(k1, c):
        blk = jnp.concatenate(
            [sf_ref[pl.ds(2 * k1, 2)].reshape(2 * FFT_FAST, cb),
             sb_ref[pl.ds(2 * k1, 2)].reshape(2 * FFT_FAST, cb)], axis=1).astype(BF16)
        x = jnp.dot(f2_ref[k1], blk, preferred_element_type=F32)
        xf, xb = x[:, :cb], x[:, cb:]
        re = (xf[:FFT_FAST] + xb[:FFT_FAST]) * inv_norm
        im = (xf[FFT_FAST:] - xb[FFT_FAST:]) * inv_norm
        r0 = pl.multiple_of(k1 * 2 * FFT_FAST, 2 * FFT_FAST)
        o_ref[0, pl.ds(r0, 2 * FFT_FAST), :] = jnp.concatenate([re, im], axis=0).astype(o_ref.dtype)
        return c

    lax.fori_loop(0, n1, stage2, 0, unroll=8)


def _hyena_filter_spectra(L, width, w1, b1, w2, b2, w3, freq, consts):
    fwd1, fwd2, _, _ = consts
    n1 = 2 * L // FFT_FAST
    cb = LANES
    ncb = width // cb
    hidden = w1.shape[1]
    tt = np.linspace(0.0, 1.0, L)[:, None]
    ww = 2.0 * np.pi * np.arange(L)[:, None] / L
    nb = (FILTER_EMB_DIM - 1) // 2
    bands = np.linspace(1e-4, nb - 1, nb)[None, :]
    feats = np.concatenate([tt, np.cos(bands * ww), -np.sin(bands * ww)], axis=-1)
    feats = np.pad(feats, ((0, 0), (0, FILTER_EMB_PAD - FILTER_EMB_DIM))).astype(np.float32)
    w1p = jnp.pad(w1.astype(F32), ((0, FILTER_EMB_PAD - FILTER_EMB_DIM), (0, 0)))
    min_decay = math.log(DECAY_TARGET) / SLOW_DECAY_PCT
    max_decay = math.log(DECAY_TARGET) / FAST_DECAY_PCT
    deltas = np.abs(np.linspace(min_decay, max_decay, width)).astype(np.float32)[None, :]

    s = n1 // 2
    one = lambda shape: pl.BlockSpec(shape, lambda i: (0,) * len(shape))
    hid = pl.pallas_call(
        _filter_hidden_kernel,
        grid=(1,),
        in_specs=[one((L, FILTER_EMB_PAD)), one((FILTER_EMB_PAD, hidden)), one((1, hidden)),
                  one((hidden, hidden)), one((1, hidden)), one((1, hidden))],
        out_specs=one((L, hidden)),
        out_shape=jax.ShapeDtypeStruct((L, hidden), F32),
        compiler_params=_cparams(("arbitrary",)),
        name="hyena_filter_hidden",
    )(jnp.asarray(feats), w1p, b1.astype(F32)[None, :], w2.astype(F32), b2.astype(F32)[None, :],
      freq.astype(F32)[None, :])
    full = lambda shape: pl.BlockSpec(shape, lambda n, c: (0,) * len(shape))
    return pl.pallas_call(
        functools.partial(_filter_kernel, L=L),
        grid=(HYENA_ORDER, ncb),
        in_specs=[
            full((L, hidden)),
            pl.BlockSpec((hidden, cb), lambda n, c: (0, n * ncb + c)),
            pl.BlockSpec((hidden, cb), lambda n, c: (0, HYENA_ORDER * ncb + n * ncb + c)),
            pl.BlockSpec((1, cb), lambda n, c: (0, c)),
            full(fwd1.shape), full(fwd2.shape),
        ],
        out_specs=pl.BlockSpec((1, n1 * 2 * FFT_FAST, cb), lambda n, c: (n, 0, c)),
        out_shape=jax.ShapeDtypeStruct((HYENA_ORDER, n1 * 2 * FFT_FAST, width), BF16),
        scratch_shapes=[pltpu.VMEM((s, FFT_FAST, cb), F32), pltpu.VMEM((s, FFT_FAST, cb), F32),
                        pltpu.VMEM((2 * n1, FFT_FAST, cb), F32), pltpu.VMEM((2 * n1, FFT_FAST, cb), F32)],
        compiler_params=_cparams(("arbitrary", "arbitrary")),
        name="hyena_filter_spectra",
    )(hid, w3.astype(F32), w3.astype(F32), jnp.asarray(deltas), fwd1, fwd2)


def _short_conv(p, w, b, L):
    row = lax.broadcasted_iota(jnp.int32, p.shape, 0)
    prev = jnp.where(row == 0, 0.0, pltpu.roll(p, 1, 0))
    nxt = jnp.where(row == L - 1, 0.0, pltpu.roll(p, L - 1, 0))
    return b + prev * w[0:1] + p * w[1:2] + nxt * w[2:3]


def _hyena_kernel(x1_ref, x2_ref, v_ref, w1_ref, w2_ref, wv_ref, b1_ref, b2_ref, bv_ref,
                  skip_ref, ng_ref, kf_ref, f1_ref, f2_ref, f2i_ref, g1_ref, gm_ref,
                  o_ref, z_ref, ga_ref, gb_ref, s_ref, *, L):
    n1 = 2 * L // FFT_FAST
    s = n1 // 2
    cb = o_ref.shape[-1]
    tiled = lambda a: a.reshape(s, FFT_FAST, cb)
    z_ref[...] = tiled(_short_conv(v_ref[0].astype(F32), wv_ref[...], bv_ref[...], L))
    ga_ref[...] = tiled(_short_conv(x1_ref[0].astype(F32), w1_ref[...], b1_ref[...], L))
    gb_ref[...] = tiled(_short_conv(x2_ref[0].astype(F32), w2_ref[...], b2_ref[...], L))
    groups = FFT_FAST // SUBLANES

    for n, gate_ref in enumerate((ga_ref, gb_ref)):
        skip = skip_ref[n:n + 1, :].reshape(1, 1, cb)

        def fwd1(gp, c):
            fa = pl.multiple_of(gp * 2 * SUBLANES, 2 * SUBLANES)
            fb = fa + SUBLANES
            xg = jnp.concatenate(
                [z_ref[:, pl.ds(fa, SUBLANES), :].reshape(s * SUBLANES, cb),
                 z_ref[:, pl.ds(fb, SUBLANES), :].reshape(s * SUBLANES, cb)], axis=1).astype(BF16)
            a = jnp.dot(f1_ref[...], xg, preferred_element_type=F32)
            s_ref[:, pl.ds(fa, SUBLANES), :] = a[:, :cb].reshape(2 * n1, SUBLANES, cb)
            s_ref[:, pl.ds(fb, SUBLANES), :] = a[:, cb:].reshape(2 * n1, SUBLANES, cb)
            return c

        lax.fori_loop(0, groups // 2, fwd1, 0, unroll=4)

        def mid(k1, c):
            blk = s_ref[pl.ds(2 * k1, 2)].reshape(2 * FFT_FAST, cb).astype(BF16)
            x = jnp.dot(f2_ref[k1], blk, preferred_element_type=F32)
            r0 = pl.multiple_of(k1 * 2 * FFT_FAST, 2 * FFT_FAST)
            kk = kf_ref[n, pl.ds(r0, 2 * FFT_FAST), :].astype(F32)
            xr, xi = x[:FFT_FAST], x[FFT_FAST:]
            kr, ki = kk[:FFT_FAST], kk[FFT_FAST:]
            y = jnp.concatenate([xr * kr - xi * ki, xr * ki + xi * kr], axis=0).astype(BF16)
            s_ref[pl.ds(2 * k1, 2)] = jnp.dot(f2i_ref[k1], y, preferred_element_type=F32).reshape(
                2, FFT_FAST, cb)
            return c

        lax.fori_loop(0, n1, mid, 0, unroll=16)

        def inv1(gp, c):
            fa = pl.multiple_of(gp * 2 * SUBLANES, 2 * SUBLANES)
            fb = fa + SUBLANES
            bg = jnp.concatenate(
                [s_ref[:, pl.ds(fa, SUBLANES), :].reshape(2 * n1 * SUBLANES, cb),
                 s_ref[:, pl.ds(fb, SUBLANES), :].reshape(2 * n1 * SUBLANES, cb)], axis=1).astype(BF16)
            y = jnp.dot(g1_ref[...], bg, preferred_element_type=F32)
            for f0, yh in ((fa, y[:, :cb]), (fb, y[:, cb:])):
                zf = z_ref[:, pl.ds(f0, SUBLANES), :]
                gf = gate_ref[:, pl.ds(f0, SUBLANES), :]
                z_ref[:, pl.ds(f0, SUBLANES), :] = gf * (yh.reshape(s, SUBLANES, cb) + skip * zf)
            return c

        lax.fori_loop(0, groups // 2, inv1, 0, unroll=4)

    z = z_ref[...].reshape(L, cb)
    zz = z * z
    hi = zz.astype(BF16)
    lo = (zz - hi.astype(F32)).astype(BF16)
    ms = (jnp.dot(hi, gm_ref[...], preferred_element_type=F32)
          + jnp.dot(lo, gm_ref[...], preferred_element_type=F32))
    o_ref[0] = (z * lax.rsqrt(ms + NORM_EPS) * ng_ref[...]).astype(o_ref.dtype)


def _hyena_mixer(proj_h, conv_w, conv_b, kf, skip, norm_g, consts):
    B, L, cols = proj_h.shape
    width = cols // (HYENA_ORDER + 1)
    cb = LANES
    ncb = width // cb
    gsz = width // HYENA_GROUPS
    n1 = 2 * L // FFT_FAST
    fwd1, fwd2, inv2, inv1 = consts
    gm = np.kron(np.eye(cb // gsz), np.full((gsz, gsz), 1.0 / gsz)).astype(np.float32)

    def stream(i):
        return pl.BlockSpec((1, L, cb), lambda c, b: (b, 0, i * ncb + c))

    def cvec(rows, i):
        return pl.BlockSpec((rows, cb), lambda c, b: (0, i * ncb + c))

    def const(shape):
        return pl.BlockSpec(shape, lambda c, b: (0,) * len(shape), pipeline_mode=pl.Buffered(1))

    cw = conv_w.astype(F32)
    cbias = conv_b.astype(F32)[None, :]
    return pl.pallas_call(
        functools.partial(_hyena_kernel, L=L),
        grid=(ncb, B),
        in_specs=[
            stream(0), stream(1), stream(2),
            cvec(3, 0), cvec(3, 1), cvec(3, 2),
            cvec(1, 0), cvec(1, 1), cvec(1, 2),
            pl.BlockSpec((HYENA_ORDER, cb), lambda c, b: (0, c)),
            pl.BlockSpec((1, cb), lambda c, b: (0, c)),
            pl.BlockSpec((HYENA_ORDER, n1 * 2 * FFT_FAST, cb), lambda c, b: (0, 0, c),
                         pipeline_mode=pl.Buffered(1)),
            const(fwd1.shape), const(fwd2.shape), const(inv2.shape), const(inv1.shape),
            const(gm.shape),
        ],
        out_specs=pl.BlockSpec((1, L, cb), lambda c, b: (b, 0, c)),
        out_shape=jax.ShapeDtypeStruct((B, L, width), BF16),
        scratch_shapes=[pltpu.VMEM((n1 // 2, FFT_FAST, cb), F32), pltpu.VMEM((n1 // 2, FFT_FAST, cb), F32),
                        pltpu.VMEM((n1 // 2, FFT_FAST, cb), F32), pltpu.VMEM((2 * n1, FFT_FAST, cb), F32)],
        compiler_params=_cparams(("arbitrary", "arbitrary")),
        name="hyena_mixer",
    )(proj_h, proj_h, proj_h, cw, cw, cw, cbias, cbias, cbias,
      skip.astype(F32), norm_g.astype(F32)[None, :], kf,
      fwd1, fwd2, inv2, inv1, jnp.asarray(gm).astype(BF16))


def _rms(x, g):
    return x * lax.rsqrt(jnp.mean(x * x, axis=-1, keepdims=True) + NORM_EPS) * g


def _in_proj_kernel(x_ref, g_ref, wh_ref, wa_ref, oh_ref, oa_ref):
    hn = _rms(x_ref[...], g_ref[...]).astype(BF16)
    oh_ref[...] = jnp.dot(hn, wh_ref[...], preferred_element_type=F32).astype(oh_ref.dtype)
    oa_ref[...] = jnp.dot(hn, wa_ref[...], preferred_element_type=F32).astype(oa_ref.dtype)


def _in_proj(x2d, g, w_in, hyena_cols, tm):
    T, D = x2d.shape
    attn_cols = w_in.shape[1] - hyena_cols
    wh = w_in[:, :hyena_cols].astype(BF16)
    wa = w_in[:, hyena_cols:].astype(BF16)
    return pl.pallas_call(
        _in_proj_kernel,
        grid=(T // tm,),
        in_specs=[
            pl.BlockSpec((tm, D), lambda i: (i, 0)),
            pl.BlockSpec((1, D), lambda i: (0, 0)),
            pl.BlockSpec((D, hyena_cols), lambda i: (0, 0)),
            pl.BlockSpec((D, attn_cols), lambda i: (0, 0)),
        ],
        out_specs=[pl.BlockSpec((tm, hyena_cols), lambda i: (i, 0)),
                   pl.BlockSpec((tm, attn_cols), lambda i: (i, 0))],
        out_shape=[jax.ShapeDtypeStruct((T, hyena_cols), BF16),
                   jax.ShapeDtypeStruct((T, attn_cols), BF16)],
        compiler_params=_cparams(("arbitrary",)),
        name="in_proj",
    )(x2d, g.astype(F32)[None, :], wh, wa)


SCORE_BOUND = 60.0


def _attn_kernel(slope_ref, q_ref, k_ref, v_ref, lq1_ref, lk1_ref, lq2_ref, lk2_ref, sg_ref, o_ref,
                 kmax_ref, *, L, tq, tk, dh, lam_init):
    h = pl.program_id(1)
    qi = pl.program_id(2)
    slope = slope_ref[h]
    q = q_ref[0]
    lane = lax.broadcasted_iota(jnp.int32, q.shape, 1)
    qs = q * jnp.asarray(dh ** -0.5, q.dtype)
    q1 = jnp.where(lane < dh, qs, jnp.zeros_like(qs))
    q2 = jnp.where(lane >= dh, qs, jnp.zeros_like(qs))
    rel = (lax.broadcasted_iota(jnp.int32, (tq, tk), 0)
           - lax.broadcasted_iota(jnp.int32, (tq, tk), 1)).astype(F32)
    q0 = (qi * tq).astype(F32)

    def update(s, m, l, acc, vb):
        m_new = jnp.maximum(m, jnp.max(s, axis=-1, keepdims=True))
        alpha = jnp.exp(m - m_new)
        p = jnp.exp(s - m_new)
        l = alpha * l + jnp.sum(p, axis=-1, keepdims=True)
        acc = alpha * acc + jnp.dot(p.astype(BF16), vb, preferred_element_type=F32)
        return m_new, l, acc

    def body(j, carry):
        m1, l1, a1, m2, l2, a2 = carry
        k0 = pl.multiple_of(j * tk, tk)
        kb = k_ref[0, pl.ds(k0, tk), :]
        vb = v_ref[0, pl.ds(k0, tk), :]
        pen = slope * jnp.abs(rel + (q0 - k0.astype(F32)))
        nt = (((1,), (1,)), ((), ()))
        s1 = lax.dot_general(q1, kb, nt, preferred_element_type=F32) - pen
        s2 = lax.dot_general(q2, kb, nt, preferred_element_type=F32) - pen
        m1, l1, a1 = update(s1, m1, l1, a1, vb)
        m2, l2, a2 = update(s2, m2, l2, a2, vb)
        return m1, l1, a1, m2, l2, a2

    def online_softmax():
        neg = jnp.full((tq, 1), -1e30, F32)
        zero = jnp.zeros((tq, 1), F32)
        zacc = jnp.zeros((tq, 2 * dh), F32)
        m1, l1, a1, m2, l2, a2 = lax.fori_loop(0, L // tk, body, (neg, zero, zacc, neg, zero, zacc))
        return a1 / l1, a2 / l2

    @pl.when(qi == 0)
    def _():
        kmax_ref[...] = jnp.max(jnp.abs(k_ref[0].astype(F32)), axis=0, keepdims=True)

    w = jnp.abs(qs.astype(F32)) * kmax_ref[...]
    ub = jnp.maximum(jnp.sum(jnp.where(lane < dh, w, 0.0), axis=-1, keepdims=True),
                     jnp.sum(jnp.where(lane >= dh, w, 0.0), axis=-1, keepdims=True))
    bounded = jnp.max(ub) <= SCORE_BOUND

    def plain_softmax():
        ones = jnp.ones((tk, 2 * dh), BF16)

        nt = (((1,), (1,)), ((), ()))
        a1 = a2 = None
        for j in range(L // tk):
            kb = k_ref[0, j * tk:(j + 1) * tk, :]
            va = jnp.concatenate([v_ref[0, j * tk:(j + 1) * tk, :], ones], axis=1)
            pen = slope * jnp.abs(rel + (q0 - float(j * tk)))
            p1 = jnp.exp(lax.dot_general(q1, kb, nt, preferred_element_type=F32) - pen)
            p2 = jnp.exp(lax.dot_general(q2, kb, nt, preferred_element_type=F32) - pen)
            d1 = jnp.dot(p1.astype(BF16), va, preferred_element_type=F32)
            d2 = jnp.dot(p2.astype(BF16), va, preferred_element_type=F32)
            a1 = d1 if a1 is None else a1 + d1
            a2 = d2 if a2 is None else a2 + d2
        return a1[:, :2 * dh] / a1[:, 2 * dh:], a2[:, :2 * dh] / a2[:, 2 * dh:]

    o1, o2 = lax.cond(bounded, plain_softmax, online_softmax)
    lam = (jnp.exp(jnp.sum(lq1_ref[...] * lk1_ref[...], axis=-1, keepdims=True))
           - jnp.exp(jnp.sum(lq2_ref[...] * lk2_ref[...], axis=-1, keepdims=True)) + lam_init)
    o = o1 - lam * o2
    o = _rms(o, sg_ref[...]) * (1.0 - lam_init)
    o_ref[0] = o.astype(o_ref.dtype)


def _diff_attention(qkv, lq1, lk1, lq2, lk2, subln_g, lam_init, tq, tk):
    B, L, cols = qkv.shape
    W = cols // 3
    H = ATTN_HEADS
    hw = W // H
    dh = hw // 2
    slopes = jnp.asarray([2.0 ** (-8.0 * (i + 1) / H) for i in range(H)], F32)
    vec = lambda a: a.astype(F32)[None, :]
    small = lambda n: pl.BlockSpec((1, n), lambda b, h, i: (0, 0))
    return pl.pallas_call(
        functools.partial(_attn_kernel, L=L, tq=tq, tk=tk, dh=dh, lam_init=lam_init),
        grid=(B, H, L // tq),
        scratch_shapes=[pltpu.VMEM((1, hw), F32)],
        in_specs=[
            pl.BlockSpec(memory_space=pltpu.SMEM),
            pl.BlockSpec((1, tq, hw), lambda b, h, i: (b, i, h)),
            pl.BlockSpec((1, L, hw), lambda b, h, i: (b, 0, H + h)),
            pl.BlockSpec((1, L, hw), lambda b, h, i: (b, 0, 2 * H + h)),
            small(dh), small(dh), small(dh), small(dh), small(hw),
        ],
        out_specs=pl.BlockSpec((1, tq, hw), lambda b, h, i: (b, i, h)),
        out_shape=jax.ShapeDtypeStruct((B, L, W), BF16),
        compiler_params=_cparams(("arbitrary", "arbitrary", "arbitrary")),
        name="diff_attention",
    )(slopes, qkv, qkv, qkv, vec(lq1), vec(lk1), vec(lq2), vec(lk2), vec(subln_g))


def _out_proj_kernel(x_ref, hy_ref, at_ref, wh_ref, wa_ref, g_ref, wr_ref, x2_ref, h2_ref, aff_ref):
    x2 = (x_ref[...] + jnp.dot(hy_ref[...], wh_ref[...], preferred_element_type=F32)
          + jnp.dot(at_ref[...], wa_ref[...], preferred_element_type=F32))
    x2_ref[...] = x2
    h2 = _rms(x2, g_ref[...])
    h2_ref[...] = h2
    wr = wr_ref[...]
    h_hi, w_hi = h2.astype(BF16), wr.astype(BF16)
    h_lo = (h2 - h_hi.astype(F32)).astype(BF16)
    w_lo = (wr - w_hi.astype(F32)).astype(BF16)
    logits = (jnp.dot(h_hi, w_hi, preferred_element_type=F32)
              + (jnp.dot(h_hi, w_lo, preferred_element_type=F32)
                 + jnp.dot(h_lo, w_hi, preferred_element_type=F32)))
    logits = logits.T[:aff_ref.shape[1]]
    e = jnp.exp(logits - jnp.max(logits, axis=0, keepdims=True))
    aff_ref[0] = e / jnp.sum(e, axis=0, keepdims=True)


def _out_proj(x2d, hy2d, at2d, w_out, g, w_router, tm, L):
    T, D = x2d.shape
    wdt = hy2d.shape[1]
    E = w_router.shape[1]
    wh = w_out[:wdt].astype(BF16)
    wa = w_out[wdt:].astype(BF16)
    per = L // tm
    row = lambda n: pl.BlockSpec((tm, n), lambda i: (i, 0))
    return pl.pallas_call(
        _out_proj_kernel,
        grid=(T // tm,),
        in_specs=[row(D), row(wdt), row(at2d.shape[1]),
                  pl.BlockSpec(wh.shape, lambda i: (0, 0)), pl.BlockSpec(wa.shape, lambda i: (0, 0)),
                  pl.BlockSpec((1, D), lambda i: (0, 0)), pl.BlockSpec((D, LANES), lambda i: (0, 0))],
        out_specs=[row(D), row(D), pl.BlockSpec((1, E, tm), lambda i: (i // per, 0, i % per))],
        out_shape=[jax.ShapeDtypeStruct((T, D), F32), jax.ShapeDtypeStruct((T, D), F32),
                   jax.ShapeDtypeStruct((T // L, E, L), F32)],
        compiler_params=_cparams(("arbitrary",)),
        name="out_proj_router",
    )(x2d, hy2d, at2d, wh, wa, g.astype(F32)[None, :],
      jnp.pad(w_router.astype(F32), ((0, 0), (0, LANES - E))))


SEL_CHUNK = 256
SLOT_LO = 32
TOK_LO = 64


def _prefix_count(mask, tri_ref, L):
    parts = []
    carry = jnp.zeros((mask.shape[0], 1), F32)
    for blk in range(L // SEL_CHUNK):
        m = mask[:, blk * SEL_CHUNK:(blk + 1) * SEL_CHUNK]
        parts.append(jnp.dot(m.astype(BF16), tri_ref[...], preferred_element_type=F32) + carry)
        carry = carry + jnp.sum(m, axis=1, keepdims=True)
    return jnp.concatenate(parts, axis=1)


def _select_kernel(aff_ref, tri_ref, idx_ref, gate_ref, *, L, E, cap):
    aff = aff_ref[0]

    def step(i, prefix):
        cand = prefix | jnp.left_shift(jnp.int32(1), 30 - i)
        cnt = jnp.sum((aff >= pltpu.bitcast(cand, F32)).astype(jnp.int32), axis=1, keepdims=True)
        return jnp.where(cnt >= cap, cand, prefix)

    tau = pltpu.bitcast(lax.fori_loop(0, 31, step, jnp.zeros((E, 1), jnp.int32)), F32)
    gt = aff > tau
    eq = aff == tau
    need = cap - jnp.sum(gt.astype(jnp.int32), axis=1, keepdims=True)
    tie_rank = _prefix_count(eq.astype(F32), tri_ref, L) - eq.astype(F32)
    sel = gt | (eq & (tie_rank < need.astype(F32)))
    rank = _prefix_count(sel.astype(F32), tri_ref, L)
    slot = jnp.where(sel, rank - 1.0, -1.0)

    n_hi = cap // SLOT_LO
    s_hi = jnp.floor(slot * (1.0 / SLOT_LO))
    s_lo = slot - SLOT_LO * s_hi
    tok = lax.broadcasted_iota(jnp.int32, (1, L), 1).astype(F32)
    t_hi = jnp.floor(tok * (1.0 / TOK_LO))
    t_lo = tok - TOK_LO * t_hi
    g_hi = aff.astype(BF16)
    r1 = aff - g_hi.astype(F32)
    g_mid = r1.astype(BF16)
    g_lo = (r1 - g_mid.astype(F32)).astype(BF16)
    hi_ids = lax.broadcasted_iota(jnp.int32, (n_hi, 1), 0).astype(F32)
    lo_ids = lax.broadcasted_iota(jnp.int32, (SLOT_LO, 1), 0).astype(F32)
    nt = (((1,), (1,)), ((), ()))
    for e in range(E):
        in_hi = s_hi[e:e + 1, :] == hi_ids
        pieces = [t_hi, t_lo, g_hi[e:e + 1, :].astype(F32), g_mid[e:e + 1, :].astype(F32),
                  g_lo[e:e + 1, :].astype(F32)]
        lhs = jnp.concatenate([jnp.where(in_hi, p, 0.0) for p in pieces], axis=0).astype(BF16)
        in_lo = (s_lo[e:e + 1, :] == lo_ids).astype(BF16)
        out = lax.dot_general(lhs, in_lo, nt, preferred_element_type=F32)
        parts = [out[k * n_hi:(k + 1) * n_hi] for k in range(5)]
        idx_ref[0, e] = (parts[0] * TOK_LO + parts[1]).astype(jnp.int32)
        gate_ref[0, e] = parts[2] + (parts[3] + parts[4])


def _select(aff_t, cap):
    B, E, L = aff_t.shape
    n_hi = cap // SLOT_LO
    tri = np.triu(np.ones((SEL_CHUNK, SEL_CHUNK), np.float32))
    slots = pl.BlockSpec((1, E, n_hi, SLOT_LO), lambda b: (b, 0, 0, 0))
    idx, gate = pl.pallas_call(
        functools.partial(_select_kernel, L=L, E=E, cap=cap),
        grid=(B,),
        in_specs=[pl.BlockSpec((1, E, L), lambda b: (b, 0, 0)),
                  pl.BlockSpec(tri.shape, lambda b: (0, 0))],
        out_specs=[slots, slots],
        out_shape=[jax.ShapeDtypeStruct((B, E, n_hi, SLOT_LO), jnp.int32),
                   jax.ShapeDtypeStruct((B, E, n_hi, SLOT_LO), F32)],
        compiler_params=_cparams(("arbitrary",)),
        name="expert_choice_select",
    )(aff_t, jnp.asarray(tri).astype(BF16))
    return idx.reshape(B, E, cap), gate.reshape(B, E, cap)


ROW_UNROLL = 8


def _gather_kernel(idx_ref, h_ref, o_ref, buf_ref, *, E, cap):
    b = pl.program_id(0)
    e = pl.program_id(1)
    base = (b * E + e) * cap

    def body(i, c):
        for u in range(SUBLANES):
            t = idx_ref[base + i * SUBLANES + u]
            buf_ref[i, pl.ds(u, 1), :] = h_ref[0, pl.ds(t, 1), :]
        return c

    lax.fori_loop(0, cap // SUBLANES, body, 0)
    o_ref[0, 0] = buf_ref[...].reshape(cap, buf_ref.shape[-1]).astype(o_ref.dtype)


def _gather(idx_flat, h2, E, cap):
    B, L, D = h2.shape
    return pl.pallas_call(
        functools.partial(_gather_kernel, E=E, cap=cap),
        grid_spec=pltpu.PrefetchScalarGridSpec(
            num_scalar_prefetch=1,
            grid=(B, E),
            in_specs=[pl.BlockSpec((1, L, D), lambda b, e, idx: (b, 0, 0))],
            out_specs=pl.BlockSpec((1, 1, cap, D), lambda b, e, idx: (e, b, 0, 0)),
            scratch_shapes=[pltpu.VMEM((cap // SUBLANES, SUBLANES, D), F32)],
        ),
        out_shape=jax.ShapeDtypeStruct((E, B, cap, D), BF16),
        compiler_params=_cparams(("arbitrary", "arbitrary")),
        name="expert_gather",
    )(idx_flat, h2)


MXU_TILE = 256


def _ffn_kernel(x_ref, *refs, nchunk):
    wg, wu, wd = refs[:nchunk], refs[nchunk:2 * nchunk], refs[2 * nchunk:3 * nchunk]
    rest = refs[3 * nchunk:]
    x = x_ref[0]
    y = rest[0][0] if len(rest) == 2 else None
    for k in range(nchunk):
        a = jnp.dot(x, wg[k][0].astype(BF16), preferred_element_type=F32)
        u = jnp.dot(x, wu[k][0].astype(BF16), preferred_element_type=F32)
        hmid = (a * jax.nn.sigmoid(a) * u).astype(BF16)
        d = jnp.dot(hmid, wd[k][0].astype(BF16), preferred_element_type=F32)
        y = d if y is None else y + d
    rest[-1][0] = y


def _expert_ffn(xe, w_gate, w_up, w_down, tt, slabs):
    E, T, D = xe.shape
    nchunks = w_gate.shape[2] // MXU_TILE
    per = -(-nchunks // slabs)
    y = None
    for s in range(slabs):
        chunks = range(s * per, min((s + 1) * per, nchunks))
        tile = pl.BlockSpec((1, tt, D), lambda e, i: (e, i, 0))
        col = [pl.BlockSpec((1, D, MXU_TILE), lambda e, i, c=c: (e, 0, c)) for c in chunks]
        row = [pl.BlockSpec((1, MXU_TILE, D), lambda e, i, c=c: (e, c, 0)) for c in chunks]
        n = len(chunks)
        in_specs = [tile] + col + col + row
        args = [xe] + [w_gate] * n + [w_up] * n + [w_down] * n
        if y is not None:
            in_specs.append(tile)
            args.append(y)
        y = pl.pallas_call(
            functools.partial(_ffn_kernel, nchunk=n),
            grid=(E, T // tt),
            in_specs=in_specs,
            out_specs=tile,
            out_shape=jax.ShapeDtypeStruct((E, T, D), F32),
            compiler_params=_cparams(("arbitrary", "arbitrary")),
            name=f"expert_ffn_slab{s}",
        )(*args)
    return y


FIN_ROWS = 512


def _scatter_kernel(idx_ref, gate_ref, y_ref, x_ref, g_ref, o_ref, *, E, cap, L, final):
    b = pl.program_id(0)
    e = pl.program_id(1)
    base = (b * E + e) * cap

    @pl.when(e == 0)
    def _():
        o_ref[...] = jnp.zeros_like(o_ref)

    def body(i, c):
        j0 = i * ROW_UNROLL
        toks = [idx_ref[base + j0 + u] for u in range(ROW_UNROLL)]
        vals = [o_ref[0, pl.ds(toks[u], 1), :]
                + gate_ref[base + j0 + u] * y_ref[0, 0, i, pl.ds(u, 1), :] for u in range(ROW_UNROLL)]
        for u in range(ROW_UNROLL):
            o_ref[0, pl.ds(toks[u], 1), :] = vals[u]
        return c

    lax.fori_loop(0, cap // ROW_UNROLL, body, 0)

    @pl.when(e == E - 1)
    def _():
        def fin(r, c):
            rows = pl.ds(pl.multiple_of(r * FIN_ROWS, FIN_ROWS), FIN_ROWS)
            v = x_ref[0, rows, :] + o_ref[0, rows, :]
            o_ref[0, rows, :] = _rms(v, g_ref[...]) if final else v
            return c

        lax.fori_loop(0, L // FIN_ROWS, fin, 0)


def _scatter(idx_flat, gate_flat, ye, x2, norm_g, final):
    E, B, cap, D = ye.shape
    L = x2.shape[1]
    return pl.pallas_call(
        functools.partial(_scatter_kernel, E=E, cap=cap, L=L, final=final),
        grid_spec=pltpu.PrefetchScalarGridSpec(
            num_scalar_prefetch=2,
            grid=(B, E),
            in_specs=[pl.BlockSpec((1, 1, cap // ROW_UNROLL, ROW_UNROLL, D),
                                   lambda b, e, idx, gt: (e, b, 0, 0, 0)),
                      pl.BlockSpec((1, L, D), lambda b, e, idx, gt: (b, 0, 0),
                                   pipeline_mode=pl.Buffered(1)),
                      pl.BlockSpec((1, D), lambda b, e, idx, gt: (0, 0))],
            out_specs=pl.BlockSpec((1, L, D), lambda b, e, idx, gt: (b, 0, 0)),
        ),
        out_shape=jax.ShapeDtypeStruct((B, L, D), F32),
        compiler_params=_cparams(("arbitrary", "arbitrary")),
        name="expert_scatter_add",
    )(idx_flat, gate_flat, ye.reshape(E, B, cap // ROW_UNROLL, ROW_UNROLL, D), x2,
      norm_g.astype(F32)[None, :])


FFN_SLABS = 2


def _moe(x2, h2, aff_t, w_gate, w_up, w_down, norm_g, tt, final):
    B, L, D = h2.shape
    E = aff_t.shape[1]
    cap = EC_CAPACITY_FACTOR * L // E
    idx, gate = _select(aff_t, cap)
    idx_flat = idx.reshape(-1)
    xe = _gather(idx_flat, h2, E, cap)
    ye = _expert_ffn(xe.reshape(E, B * cap, D), w_gate, w_up, w_down, tt, FFN_SLABS)
    return _scatter(idx_flat, gate.reshape(-1), ye.reshape(E, B, cap, D), x2, norm_g, final)


def kernel(x, attn_norm_g, w_in, conv_w, conv_b, filt_w1, filt_b1, filt_w2, filt_b2, filt_w3, filt_freq, hyena_skip, hyena_norm_g, lambda_q1, lambda_k1, lambda_q2, lambda_k2, subln_g, w_out, ffn_norm_g, w_router, w_gate, w_up, w_down, final_norm_g):
    B, L, D = x.shape
    depth = w_in.shape[0]
    width = hyena_skip.shape[-1]
    hyena_cols = (HYENA_ORDER + 1) * width
    E = w_router.shape[-1]
    tm = min(512, B * L)
    tq = min(512, L)
    consts = _dft_constants(L)
    x2d = x.reshape(B * L, D)
    for l in range(depth):
        lam_init = 0.8 - 0.6 * math.exp(-0.3 * l)
        ph, qkv = _in_proj(x2d, attn_norm_g[l], w_in[l], hyena_cols, tm)
        kf = _hyena_filter_spectra(L, width, filt_w1[l], filt_b1[l], filt_w2[l], filt_b2[l],
                                   filt_w3[l], filt_freq[l], consts)
        hy = _hyena_mixer(ph.reshape(B, L, hyena_cols), conv_w[l], conv_b[l], kf, hyena_skip[l],
                          hyena_norm_g[l], consts)
        at = _diff_attention(qkv.reshape(B, L, -1), lambda_q1[l], lambda_k1[l], lambda_q2[l],
                             lambda_k2[l], subln_g[l], lam_init, min(2 * tq, L), tq)
        x2, h2, aff_t = _out_proj(x2d, hy.reshape(B * L, width), at.reshape(B * L, -1), w_out[l],
                                  ffn_norm_g[l], w_router[l], min(tm // 2, L), L)
        cap = EC_CAPACITY_FACTOR * L // E
        x2d = _moe(x2.reshape(B, L, D), h2.reshape(B, L, D), aff_t, w_gate[l], w_up[l],
                   w_down[l], final_norm_g, tt=min(512, B * cap), final=l + 1 == depth).reshape(B * L, D)
    return x2d.reshape(B, L, D)
```

```python
import functools
import math

import jax
import jax.numpy as jnp
import numpy as np
from jax import lax
from jax.experimental import pallas as pl
from jax.experimental.pallas import tpu as pltpu

F32 = jnp.float32
BF16 = jnp.bfloat16
HIGHEST = lax.Precision.HIGHEST

NORM_EPS = 1e-6
HYENA_ORDER = 2
HYENA_GROUPS = 8
ATTN_HEADS = 4
N_EXPERTS = 16
EC_CAPACITY_FACTOR = 2
FILTER_EMB_DIM = 33
FILTER_EMB_PAD = 40
DECAY_TARGET = 1e-2
FAST_DECAY_PCT = 0.3
SLOW_DECAY_PCT = 1.5

LANES = 128
FFT_FAST = 128
V7X_VMEM_LIMIT = 56 * 1024 * 1024


def _cparams(sem, vmem=V7X_VMEM_LIMIT):
    return pltpu.CompilerParams(dimension_semantics=sem, vmem_limit_bytes=vmem)


SUBLANES = 8


def _half_spectrum(L):
    nk = (2 * L // FFT_FAST) // 2 + 1
    return nk, max(u for u in range(1, 17) if nk % u == 0)


def _dft_constants(L):
    n2 = FFT_FAST
    n = 2 * L
    n1 = n // n2
    s = n1 // 2
    nk, _ = _half_spectrum(L)
    k1 = np.arange(nk)
    th1 = 2.0 * np.pi * np.outer(k1, np.arange(s)) / n1
    f1 = np.empty((2 * nk, s), np.float64)
    f1[0::2], f1[1::2] = np.cos(th1), -np.sin(th1)
    wgt = np.where((k1 == 0) | (k1 == n1 // 2), 1.0, 2.0)[None, :]
    g1 = np.empty((s, 2 * nk), np.float64)
    g1[:, 0::2], g1[:, 1::2] = wgt * np.cos(th1.T), -wgt * np.sin(th1.T)
    eye = np.eye(SUBLANES)
    fwd1 = np.kron(f1, eye)
    inv1 = np.kron(g1, eye)
    a = np.arange(n2)
    ph = 2.0 * np.pi * (np.outer(a, a)[None] / n2 + (k1[:, None, None] * a[None, None, :]) / n)
    tr, ti = np.cos(ph), -np.sin(ph)
    fwd2 = np.concatenate([np.concatenate([tr, -ti], 2), np.concatenate([ti, tr], 2)], 1)
    pht = np.transpose(ph, (0, 2, 1))
    ur, ui = np.cos(pht) / n, np.sin(pht) / n
    inv2 = np.concatenate([np.concatenate([ur, -ui], 2), np.concatenate([ui, ur], 2)], 1)
    return tuple(jnp.asarray(m.astype(np.float32)).astype(BF16) for m in (fwd1, fwd2, inv2, inv1))


def _filter_hidden_kernel(zf_ref, w1_ref, b1_ref, w2_ref, b2_ref, fq_ref, o_ref):
    fq = fq_ref[...]
    hid = jnp.sin(fq * (jnp.dot(zf_ref[...], w1_ref[...], precision=HIGHEST,
                                preferred_element_type=F32) + b1_ref[...]))
    o_ref[...] = jnp.sin(fq * (jnp.dot(hid, w2_ref[...], precision=HIGHEST,
                                       preferred_element_type=F32) + b2_ref[...]))


def _filter_kernel(hid_ref, w3f_ref, w3b_ref, dl_ref, f1_ref, f2_ref, o_ref,
                   hf_ref, hb_ref, sf_ref, sb_ref, *, L):
    s = L // FFT_FAST
    nk, unroll = _half_spectrum(L)
    hid = hid_ref[...]
    row = lax.broadcasted_iota(jnp.int32, (L, 1), 0)
    t = row.astype(F32) * (1.0 / (L - 1))
    win = jnp.exp(-t * dl_ref[...])
    hf = jnp.dot(hid, w3f_ref[...], precision=HIGHEST, preferred_element_type=F32) * win
    hb = jnp.dot(hid, w3b_ref[...], precision=HIGHEST, preferred_element_type=F32) * win
    hb = jnp.where(row == 0, 0.0, hb)
    norm = (jnp.sum(jnp.abs(hf), axis=0, keepdims=True)
            + jnp.sum(jnp.abs(hb), axis=0, keepdims=True))
    cb = hf.shape[1]
    hf_ref[...] = hf.reshape(s, FFT_FAST, cb)
    hb_ref[...] = hb.reshape(s, FFT_FAST, cb)

    def stage1(g, c):
        f0 = pl.multiple_of(g * SUBLANES, SUBLANES)
        xg = jnp.concatenate(
            [hf_ref[:, pl.ds(f0, SUBLANES), :].reshape(s * SUBLANES, cb),
             hb_ref[:, pl.ds(f0, SUBLANES), :].reshape(s * SUBLANES, cb)], axis=1).astype(BF16)
        a = jnp.dot(f1_ref[...], xg, preferred_element_type=F32)
        sf_ref[:, pl.ds(f0, SUBLANES), :] = a[:, :cb].reshape(2 * nk, SUBLANES, cb)
        sb_ref[:, pl.ds(f0, SUBLANES), :] = a[:, cb:].reshape(2 * nk, SUBLANES, cb)
        return c

    lax.fori_loop(0, FFT_FAST // SUBLANES, stage1, 0, unroll=4)
    inv_norm = 1.0 / norm

    def stage2(k1, c):
        blk = jnp.concatenate(
            [sf_ref[pl.ds(2 * k1, 2)].reshape(2 * FFT_FAST, cb),
             sb_ref[pl.ds(2 * k1, 2)].reshape(2 * FFT_FAST, cb)], axis=1).astype(BF16)
        x = jnp.dot(f2_ref[k1], blk, preferred_element_type=F32)
        xf, xb = x[:, :cb], x[:, cb:]
        re = (xf[:FFT_FAST] + xb[:FFT_FAST]) * inv_norm
        im = (xf[FFT_FAST:] - xb[FFT_FAST:]) * inv_norm
        r0 = pl.multiple_of(k1 * 2 * FFT_FAST, 2 * FFT_FAST)
        o_ref[0, pl.ds(r0, 2 * FFT_FAST), :] = jnp.concatenate([re, im], axis=0).astype(o_ref.dtype)
        return c

    lax.fori_loop(0, nk, stage2, 0, unroll=unroll)


def _hyena_filter_spectra(L, width, w1, b1, w2, b2, w3, freq, consts):
    fwd1, fwd2, _, _ = consts
    n1 = 2 * L // FFT_FAST
    nk, _ = _half_spectrum(L)
    cb = LANES
    ncb = width // cb
    hidden = w1.shape[1]
    tt = np.linspace(0.0, 1.0, L)[:, None]
    ww = 2.0 * np.pi * np.arange(L)[:, None] / L
    nb = (FILTER_EMB_DIM - 1) // 2
    bands = np.linspace(1e-4, nb - 1, nb)[None, :]
    feats = np.concatenate([tt, np.cos(bands * ww), -np.sin(bands * ww)], axis=-1)
    feats = np.pad(feats, ((0, 0), (0, FILTER_EMB_PAD - FILTER_EMB_DIM))).astype(np.float32)
    w1p = jnp.pad(w1.astype(F32), ((0, FILTER_EMB_PAD - FILTER_EMB_DIM), (0, 0)))
    min_decay = math.log(DECAY_TARGET) / SLOW_DECAY_PCT
    max_decay = math.log(DECAY_TARGET) / FAST_DECAY_PCT
    deltas = np.abs(np.linspace(min_decay, max_decay, width)).astype(np.float32)[None, :]

    s = n1 // 2
    one = lambda shape: pl.BlockSpec(shape, lambda i: (0,) * len(shape))
    hid = pl.pallas_call(
        _filter_hidden_kernel,
        grid=(1,),
        in_specs=[one((L, FILTER_EMB_PAD)), one((FILTER_EMB_PAD, hidden)), one((1, hidden)),
                  one((hidden, hidden)), one((1, hidden)), one((1, hidden))],
        out_specs=one((L, hidden)),
        out_shape=jax.ShapeDtypeStruct((L, hidden), F32),
        compiler_params=_cparams(("arbitrary",)),
        name="hyena_filter_hidden",
    )(jnp.asarray(feats), w1p, b1.astype(F32)[None, :], w2.astype(F32), b2.astype(F32)[None, :],
      freq.astype(F32)[None, :])
    full = lambda shape: pl.BlockSpec(shape, lambda n, c: (0,) * len(shape))
    return pl.pallas_call(
        functools.partial(_filter_kernel, L=L),
        grid=(HYENA_ORDER, ncb),
        in_specs=[
            full((L, hidden)),
            pl.BlockSpec((hidden, cb), lambda n, c: (0, n * ncb + c)),
            pl.BlockSpec((hidden, cb), lambda n, c: (0, HYENA_ORDER * ncb + n * ncb + c)),
            pl.BlockSpec((1, cb), lambda n, c: (0, c)),
            full(fwd1.shape), full(fwd2.shape),
        ],
        out_specs=pl.BlockSpec((1, nk * 2 * FFT_FAST, cb), lambda n, c: (n, 0, c)),
        out_shape=jax.ShapeDtypeStruct((HYENA_ORDER, nk * 2 * FFT_FAST, width), BF16),
        scratch_shapes=[pltpu.VMEM((s, FFT_FAST, cb), F32), pltpu.VMEM((s, FFT_FAST, cb), F32),
                        pltpu.VMEM((2 * nk, FFT_FAST, cb), F32), pltpu.VMEM((2 * nk, FFT_FAST, cb), F32)],
        compiler_params=_cparams(("arbitrary", "arbitrary")),
        name="hyena_filter_spectra",
    )(hid, w3.astype(F32), w3.astype(F32), jnp.asarray(deltas), fwd1, fwd2)


def _short_conv(p, w, b, L):
    row = lax.broadcasted_iota(jnp.int32, p.shape, 0)
    prev = jnp.where(row == 0, 0.0, pltpu.roll(p, 1, 0))
    nxt = jnp.where(row == L - 1, 0.0, pltpu.roll(p, L - 1, 0))
    return b + prev * w[0:1] + p * w[1:2] + nxt * w[2:3]


def _hyena_kernel(x1_ref, x2_ref, v_ref, w1_ref, w2_ref, wv_ref, b1_ref, b2_ref, bv_ref,
                  skip_ref, ng_ref, kf_ref, f1_ref, f2_ref, f2i_ref, g1_ref, gm_ref,
                  o_ref, z_ref, ga_ref, gb_ref, s_ref, *, L):
    s = L // FFT_FAST
    nk, unroll = _half_spectrum(L)
    cb = o_ref.shape[-1]
    tiled = lambda a: a.reshape(s, FFT_FAST, cb)
    z_ref[...] = tiled(_short_conv(v_ref[0].astype(F32), wv_ref[...], bv_ref[...], L))
    ga_ref[...] = tiled(_short_conv(x1_ref[0].astype(F32), w1_ref[...], b1_ref[...], L))
    gb_ref[...] = tiled(_short_conv(x2_ref[0].astype(F32), w2_ref[...], b2_ref[...], L))
    groups = FFT_FAST // SUBLANES

    for n, gate_ref in enumerate((ga_ref, gb_ref)):
        skip = skip_ref[n:n + 1, :].reshape(1, 1, cb)

        def fwd1(gp, c):
            fa = pl.multiple_of(gp * 2 * SUBLANES, 2 * SUBLANES)
            fb = fa + SUBLANES
            xg = jnp.concatenate(
                [z_ref[:, pl.ds(fa, SUBLANES), :].reshape(s * SUBLANES, cb),
                 z_ref[:, pl.ds(fb, SUBLANES), :].reshape(s * SUBLANES, cb)], axis=1).astype(BF16)
            a = jnp.dot(f1_ref[...], xg, preferred_element_type=F32)
            s_ref[:, pl.ds(fa, SUBLANES), :] = a[:, :cb].reshape(2 * nk, SUBLANES, cb)
            s_ref[:, pl.ds(fb, SUBLANES), :] = a[:, cb:].reshape(2 * nk, SUBLANES, cb)
            return c

        lax.fori_loop(0, groups // 2, fwd1, 0, unroll=4)

        def mid(k1, c):
            blk = s_ref[pl.ds(2 * k1, 2)].reshape(2 * FFT_FAST, cb).astype(BF16)
            x = jnp.dot(f2_ref[k1], blk, preferred_element_type=F32)
            r0 = pl.multiple_of(k1 * 2 * FFT_FAST, 2 * FFT_FAST)
            kk = kf_ref[n, pl.ds(r0, 2 * FFT_FAST), :].astype(F32)
            xr, xi = x[:FFT_FAST], x[FFT_FAST:]
            kr, ki = kk[:FFT_FAST], kk[FFT_FAST:]
            y = jnp.concatenate([xr * kr - xi * ki, xr * ki + xi * kr], axis=0).astype(BF16)
            s_ref[pl.ds(2 * k1, 2)] = jnp.dot(f2i_ref[k1], y, preferred_element_type=F32).reshape(
                2, FFT_FAST, cb)
            return c

        lax.fori_loop(0, nk, mid, 0, unroll=unroll)

        def inv1(gp, c):
            fa = pl.multiple_of(gp * 2 * SUBLANES, 2 * SUBLANES)
            fb = fa + SUBLANES
            bg = jnp.concatenate(
                [s_ref[:, pl.ds(fa, SUBLANES), :].reshape(2 * nk * SUBLANES, cb),
                 s_ref[:, pl.ds(fb, SUBLANES), :].reshape(2 * nk * SUBLANES, cb)], axis=1).astype(BF16)
            y = jnp.dot(g1_ref[...], bg, preferred_element_type=F32)
            for f0, yh in ((fa, y[:, :cb]), (fb, y[:, cb:])):
                zf = z_ref[:, pl.ds(f0, SUBLANES), :]
                gf = gate_ref[:, pl.ds(f0, SUBLANES), :]
                z_ref[:, pl.ds(f0, SUBLANES), :] = gf * (yh.reshape(s, SUBLANES, cb) + skip * zf)
            return c

        lax.fori_loop(0, groups // 2, inv1, 0, unroll=4)

    z = z_ref[...].reshape(L, cb)
    zz = z * z
    hi = zz.astype(BF16)
    lo = (zz - hi.astype(F32)).astype(BF16)
    ms = (jnp.dot(hi, gm_ref[...], preferred_element_type=F32)
          + jnp.dot(lo, gm_ref[...], preferred_element_type=F32))
    o_ref[0] = (z * lax.rsqrt(ms + NORM_EPS) * ng_ref[...]).astype(o_ref.dtype)


def _hyena_mixer(proj_h, conv_w, conv_b, kf, skip, norm_g, consts):
    B, L, cols = proj_h.shape
    width = cols // (HYENA_ORDER + 1)
    cb = LANES
    ncb = width // cb
    gsz = width // HYENA_GROUPS
    n1 = 2 * L // FFT_FAST
    nk, _ = _half_spectrum(L)
    fwd1, fwd2, inv2, inv1 = consts
    gm = np.kron(np.eye(cb // gsz), np.full((gsz, gsz), 1.0 / gsz)).astype(np.float32)

    def stream(i):
        return pl.BlockSpec((1, L, cb), lambda c, b: (b, 0, i * ncb + c))

    def cvec(rows, i):
        return pl.BlockSpec((rows, cb), lambda c, b: (0, i * ncb + c))

    def const(shape):
        return pl.BlockSpec(shape, lambda c, b: (0,) * len(shape), pipeline_mode=pl.Buffered(1))

    cw = conv_w.astype(F32)
    cbias = conv_b.astype(F32)[None, :]
    return pl.pallas_call(
        functools.partial(_hyena_kernel, L=L),
        grid=(ncb, B),
        in_specs=[
            stream(0), stream(1), stream(2),
            cvec(3, 0), cvec(3, 1), cvec(3, 2),
            cvec(1, 0), cvec(1, 1), cvec(1, 2),
            pl.BlockSpec((HYENA_ORDER, cb), lambda c, b: (0, c)),
            pl.BlockSpec((1, cb), lambda c, b: (0, c)),
            pl.BlockSpec((HYENA_ORDER, nk * 2 * FFT_FAST, cb), lambda c, b: (0, 0, c),
                         pipeline_mode=pl.Buffered(1)),
            const(fwd1.shape), const(fwd2.shape), const(inv2.shape), const(inv1.shape),
            const(gm.shape),
        ],
        out_specs=pl.BlockSpec((1, L, cb), lambda c, b: (b, 0, c)),
        out_shape=jax.ShapeDtypeStruct((B, L, width), BF16),
        scratch_shapes=[pltpu.VMEM((n1 // 2, FFT_FAST, cb), F32), pltpu.VMEM((n1 // 2, FFT_FAST, cb), F32),
                        pltpu.VMEM((n1 // 2, FFT_FAST, cb), F32), pltpu.VMEM((2 * nk, FFT_FAST, cb), F32)],
        compiler_params=_cparams(("arbitrary", "arbitrary")),
        name="hyena_mixer",
    )(proj_h, proj_h, proj_h, cw, cw, cw, cbias, cbias, cbias,
      skip.astype(F32), norm_g.astype(F32)[None, :], kf,
      fwd1, fwd2, inv2, inv1, jnp.asarray(gm).astype(BF16))


def _rms(x, g):
    return x * lax.rsqrt(jnp.mean(x * x, axis=-1, keepdims=True) + NORM_EPS) * g


def _in_proj_kernel(x_ref, g_ref, wh_ref, wa_ref, oh_ref, oa_ref):
    hn = _rms(x_ref[...], g_ref[...]).astype(BF16)
    oh_ref[...] = jnp.dot(hn, wh_ref[...], preferred_element_type=F32).astype(oh_ref.dtype)
    oa_ref[...] = jnp.dot(hn, wa_ref[...], preferred_element_type=F32).astype(oa_ref.dtype)


def _in_proj(x2d, g, w_in, hyena_cols, tm):
    T, D = x2d.shape
    attn_cols = w_in.shape[1] - hyena_cols
    wh = w_in[:, :hyena_cols].astype(BF16)
    wa = w_in[:, hyena_cols:].astype(BF16)
    return pl.pallas_call(
        _in_proj_kernel,
        grid=(T // tm,),
        in_specs=[
            pl.BlockSpec((tm, D), lambda i: (i, 0)),
            pl.BlockSpec((1, D), lambda i: (0, 0)),
            pl.BlockSpec((D, hyena_cols), lambda i: (0, 0)),
            pl.BlockSpec((D, attn_cols), lambda i: (0, 0)),
        ],
        out_specs=[pl.BlockSpec((tm, hyena_cols), lambda i: (i, 0)),
                   pl.BlockSpec((tm, attn_cols), lambda i: (i, 0))],
        out_shape=[jax.ShapeDtypeStruct((T, hyena_cols), BF16),
                   jax.ShapeDtypeStruct((T, attn_cols), BF16)],
        compiler_params=_cparams(("arbitrary",)),
        name="in_proj",
    )(x2d, g.astype(F32)[None, :], wh, wa)


SCORE_BOUND = 60.0


def _attn_kernel(slope_ref, q_ref, k_ref, v_ref, lq1_ref, lk1_ref, lq2_ref, lk2_ref, sg_ref, o_ref,
                 kmax_ref, *, L, tq, tk, dh, lam_init):
    h = pl.program_id(1)
    qi = pl.program_id(2)
    slope = slope_ref[h]
    q = q_ref[0]
    lane = lax.broadcasted_iota(jnp.int32, q.shape, 1)
    qs = q * jnp.asarray(dh ** -0.5, q.dtype)
    q1 = jnp.where(lane < dh, qs, jnp.zeros_like(qs))
    q2 = jnp.where(lane >= dh, qs, jnp.zeros_like(qs))
    rel = (lax.broadcasted_iota(jnp.int32, (tq, tk), 0)
           - lax.broadcasted_iota(jnp.int32, (tq, tk), 1)).astype(F32)
    q0 = (qi * tq).astype(F32)

    def update(s, m, l, acc, vb):
        m_new = jnp.maximum(m, jnp.max(s, axis=-1, keepdims=True))
        alpha = jnp.exp(m - m_new)
        p = jnp.exp(s - m_new)
        l = alpha * l + jnp.sum(p, axis=-1, keepdims=True)
        acc = alpha * acc + jnp.dot(p.astype(BF16), vb, preferred_element_type=F32)
        return m_new, l, acc

    def body(j, carry):
        m1, l1, a1, m2, l2, a2 = carry
        k0 = pl.multiple_of(j * tk, tk)
        kb = k_ref[0, pl.ds(k0, tk), :]
        vb = v_ref[0, pl.ds(k0, tk), :]
        pen = slope * jnp.abs(rel + (q0 - k0.astype(F32)))
        nt = (((1,), (1,)), ((), ()))
        s1 = lax.dot_general(q1, kb, nt, preferred_element_type=F32) - pen
        s2 = lax.dot_general(q2, kb, nt, preferred_element_type=F32) - pen
        m1, l1, a1 = update(s1, m1, l1, a1, vb)
        m2, l2, a2 = update(s2, m2, l2, a2, vb)
        return m1, l1, a1, m2, l2, a2

    def online_softmax():
        neg = jnp.full((tq, 1), -1e30, F32)
        zero = jnp.zeros((tq, 1), F32)
        zacc = jnp.zeros((tq, 2 * dh), F32)
        m1, l1, a1, m2, l2, a2 = lax.fori_loop(0, L // tk, body, (neg, zero, zacc, neg, zero, zacc))
        return a1 / l1, a2 / l2

    @pl.when(qi == 0)
    def _():
        kmax_ref[...] = jnp.max(jnp.abs(k_ref[0].astype(F32)), axis=0, keepdims=True)

    w = jnp.abs(qs.astype(F32)) * kmax_ref[...]
    ub = jnp.maximum(jnp.sum(jnp.where(lane < dh, w, 0.0), axis=-1, keepdims=True),
                     jnp.sum(jnp.where(lane >= dh, w, 0.0), axis=-1, keepdims=True))
    bounded = jnp.max(ub) <= SCORE_BOUND

    def plain_softmax():
        ones = jnp.ones((tk, 2 * dh), BF16)

        nt = (((1,), (1,)), ((), ()))
        a1 = a2 = None
        for j in range(L // tk):
            kb = k_ref[0, j * tk:(j + 1) * tk, :]
            va = jnp.concatenate([v_ref[0, j * tk:(j + 1) * tk, :], ones], axis=1)
            pen = slope * jnp.abs(rel + (q0 - float(j * tk)))
            p1 = jnp.exp(lax.dot_general(q1, kb, nt, preferred_element_type=F32) - pen)
            p2 = jnp.exp(lax.dot_general(q2, kb, nt, preferred_element_type=F32) - pen)
            d1 = jnp.dot(p1.astype(BF16), va, preferred_element_type=F32)
            d2 = jnp.dot(p2.astype(BF16), va, preferred_element_type=F32)
            a1 = d1 if a1 is None else a1 + d1
            a2 = d2 if a2 is None else a2 + d2
        return a1[:, :2 * dh] / a1[:, 2 * dh:], a2[:, :2 * dh] / a2[:, 2 * dh:]

    o1, o2 = lax.cond(bounded, plain_softmax, online_softmax)
    lam = (jnp.exp(jnp.sum(lq1_ref[...] * lk1_ref[...], axis=-1, keepdims=True))
           - jnp.exp(jnp.sum(lq2_ref[...] * lk2_ref[...], axis=-1, keepdims=True)) + lam_init)
    o = o1 - lam * o2
    o = _rms(o, sg_ref[...]) * (1.0 - lam_init)
    o_ref[0] = o.astype(o_ref.dtype)


def _diff_attention(qkv, lq1, lk1, lq2, lk2, subln_g, lam_init, tq, tk):
    B, L, cols = qkv.shape
    W = cols // 3
    H = ATTN_HEADS
    hw = W // H
    dh = hw // 2
    slopes = jnp.asarray([2.0 ** (-8.0 * (i + 1) / H) for i in range(H)], F32)
    vec = lambda a: a.astype(F32)[None, :]
    small = lambda n: pl.BlockSpec((1, n), lambda b, h, i: (0, 0))
    return pl.pallas_call(
        functools.partial(_attn_kernel, L=L, tq=tq, tk=tk, dh=dh, lam_init=lam_init),
        grid=(B, H, L // tq),
        scratch_shapes=[pltpu.VMEM((1, hw), F32)],
        in_specs=[
            pl.BlockSpec(memory_space=pltpu.SMEM),
            pl.BlockSpec((1, tq, hw), lambda b, h, i: (b, i, h)),
            pl.BlockSpec((1, L, hw), lambda b, h, i: (b, 0, H + h)),
            pl.BlockSpec((1, L, hw), lambda b, h, i: (b, 0, 2 * H + h)),
            small(dh), small(dh), small(dh), small(dh), small(hw),
        ],
        out_specs=pl.BlockSpec((1, tq, hw), lambda b, h, i: (b, i, h)),
        out_shape=jax.ShapeDtypeStruct((B, L, W), BF16),
        compiler_params=_cparams(("arbitrary", "arbitrary", "arbitrary")),
        name="diff_attention",
    )(slopes, qkv, qkv, qkv, vec(lq1), vec(lk1), vec(lq2), vec(lk2), vec(subln_g))


def _out_proj_kernel(x_ref, hy_ref, at_ref, wh_ref, wa_ref, g_ref, wr_ref, x2_ref, h2_ref, aff_ref):
    x2 = (x_ref[...] + jnp.dot(hy_ref[...], wh_ref[...], preferred_element_type=F32)
          + jnp.dot(at_ref[...], wa_ref[...], preferred_element_type=F32))
    x2_ref[...] = x2
    h2 = _rms(x2, g_ref[...])
    h2_ref[...] = h2
    wr = wr_ref[...]
    h_hi, w_hi = h2.astype(BF16), wr.astype(BF16)
    h_lo = (h2 - h_hi.astype(F32)).astype(BF16)
    w_lo = (wr - w_hi.astype(F32)).astype(BF16)
    logits = (jnp.dot(h_hi, w_hi, preferred_element_type=F32)
              + (jnp.dot(h_hi, w_lo, preferred_element_type=F32)
                 + jnp.dot(h_lo, w_hi, preferred_element_type=F32)))
    logits = logits.T[:aff_ref.shape[1]]
    e = jnp.exp(logits - jnp.max(logits, axis=0, keepdims=True))
    aff_ref[0] = e / jnp.sum(e, axis=0, keepdims=True)


def _out_proj(x2d, hy2d, at2d, w_out, g, w_router, tm, L):
    T, D = x2d.shape
    wdt = hy2d.shape[1]
    E = w_router.shape[1]
    wh = w_out[:wdt].astype(BF16)
    wa = w_out[wdt:].astype(BF16)
    per = L // tm
    row = lambda n: pl.BlockSpec((tm, n), lambda i: (i, 0))
    return pl.pallas_call(
        _out_proj_kernel,
        grid=(T // tm,),
        in_specs=[row(D), row(wdt), row(at2d.shape[1]),
                  pl.BlockSpec(wh.shape, lambda i: (0, 0)), pl.BlockSpec(wa.shape, lambda i: (0, 0)),
                  pl.BlockSpec((1, D), lambda i: (0, 0)), pl.BlockSpec((D, LANES), lambda i: (0, 0))],
        out_specs=[row(D), row(D), pl.BlockSpec((1, E, tm), lambda i: (i // per, 0, i % per))],
        out_shape=[jax.ShapeDtypeStruct((T, D), F32), jax.ShapeDtypeStruct((T, D), F32),
                   jax.ShapeDtypeStruct((T // L, E, L), F32)],
        compiler_params=_cparams(("arbitrary",)),
        name="out_proj_router",
    )(x2d, hy2d, at2d, wh, wa, g.astype(F32)[None, :],
      jnp.pad(w_router.astype(F32), ((0, 0), (0, LANES - E))))


SEL_CHUNK = 256
SLOT_LO = 32
TOK_LO = 64


def _prefix_count(mask, tri_ref, L):
    parts = []
    carry = jnp.zeros((mask.shape[0], 1), F32)
    for blk in range(L // SEL_CHUNK):
        m = mask[:, blk * SEL_CHUNK:(blk + 1) * SEL_CHUNK]
        parts.append(jnp.dot(m.astype(BF16), tri_ref[...], preferred_element_type=F32) + carry)
        carry = carry + jnp.sum(m, axis=1, keepdims=True)
    return jnp.concatenate(parts, axis=1)


def _select_kernel(aff_ref, tri_ref, idx_ref, gate_ref, *, L, E, cap):
    aff = aff_ref[0]

    def step(i, prefix):
        cand = prefix | jnp.left_shift(jnp.int32(1), 30 - i)
        cnt = jnp.sum((aff >= pltpu.bitcast(cand, F32)).astype(jnp.int32), axis=1, keepdims=True)
        return jnp.where(cnt >= cap, cand, prefix)

    tau = pltpu.bitcast(lax.fori_loop(0, 31, step, jnp.zeros((E, 1), jnp.int32)), F32)
    gt = aff > tau
    eq = aff == tau
    need = cap - jnp.sum(gt.astype(jnp.int32), axis=1, keepdims=True)
    tie_rank = _prefix_count(eq.astype(F32), tri_ref, L) - eq.astype(F32)
    sel = gt | (eq & (tie_rank < need.astype(F32)))
    rank = _prefix_count(sel.astype(F32), tri_ref, L)
    slot = jnp.where(sel, rank - 1.0, -1.0)

    n_hi = cap // SLOT_LO
    s_hi = jnp.floor(slot * (1.0 / SLOT_LO))
    s_lo = slot - SLOT_LO * s_hi
    tok = lax.broadcasted_iota(jnp.int32, (1, L), 1).astype(F32)
    t_hi = jnp.floor(tok * (1.0 / TOK_LO))
    t_lo = tok - TOK_LO * t_hi
    g_hi = aff.astype(BF16)
    r1 = aff - g_hi.astype(F32)
    g_mid = r1.astype(BF16)
    g_lo = (r1 - g_mid.astype(F32)).astype(BF16)
    hi_ids = lax.broadcasted_iota(jnp.int32, (n_hi, 1), 0).astype(F32)
    lo_ids = lax.broadcasted_iota(jnp.int32, (SLOT_LO, 1), 0).astype(F32)
    nt = (((1,), (1,)), ((), ()))
    for e in range(E):
        in_hi = s_hi[e:e + 1, :] == hi_ids
        pieces = [t_hi, t_lo, g_hi[e:e + 1, :].astype(F32), g_mid[e:e + 1, :].astype(F32),
                  g_lo[e:e + 1, :].astype(F32)]
        lhs = jnp.concatenate([jnp.where(in_hi, p, 0.0) for p in pieces], axis=0).astype(BF16)
        in_lo = (s_lo[e:e + 1, :] == lo_ids).astype(BF16)
        out = lax.dot_general(lhs, in_lo, nt, preferred_element_type=F32)
        parts = [out[k * n_hi:(k + 1) * n_hi] for k in range(5)]
        idx_ref[0, e] = (parts[0] * TOK_LO + parts[1]).astype(jnp.int32)
        gate_ref[0, e] = parts[2] + (parts[3] + parts[4])


def _select(aff_t, cap):
    B, E, L = aff_t.shape
    n_hi = cap // SLOT_LO
    tri = np.triu(np.ones((SEL_CHUNK, SEL_CHUNK), np.float32))
    slots = pl.BlockSpec((1, E, n_hi, SLOT_LO), lambda b: (b, 0, 0, 0))
    idx, gate = pl.pallas_call(
        functools.partial(_select_kernel, L=L, E=E, cap=cap),
        grid=(B,),
        in_specs=[pl.BlockSpec((1, E, L), lambda b: (b, 0, 0)),
                  pl.BlockSpec(tri.shape, lambda b: (0, 0))],
        out_specs=[slots, slots],
        out_shape=[jax.ShapeDtypeStruct((B, E, n_hi, SLOT_LO), jnp.int32),
                   jax.ShapeDtypeStruct((B, E, n_hi, SLOT_LO), F32)],
        compiler_params=_cparams(("arbitrary",)),
        name="expert_choice_select",
    )(aff_t, jnp.asarray(tri).astype(BF16))
    return idx.reshape(B, E, cap), gate.reshape(B, E, cap)


ROW_UNROLL = 8


def _gather_kernel(idx_ref, h_ref, o_ref, buf_ref, *, E, cap):
    b = pl.program_id(0)
    e = pl.program_id(1)
    base = (b * E + e) * cap

    def body(i, c):
        for u in range(SUBLANES):
            t = idx_ref[base + i * SUBLANES + u]
            buf_ref[i, pl.ds(u, 1), :] = h_ref[0, pl.ds(t, 1), :]
        return c

    lax.fori_loop(0, cap // SUBLANES, body, 0)
    o_ref[0, 0] = buf_ref[...].reshape(cap, buf_ref.shape[-1]).astype(o_ref.dtype)


def _gather(idx_flat, h2, E, cap):
    B, L, D = h2.shape
    return pl.pallas_call(
        functools.partial(_gather_kernel, E=E, cap=cap),
        grid_spec=pltpu.PrefetchScalarGridSpec(
            num_scalar_prefetch=1,
            grid=(B, E),
            in_specs=[pl.BlockSpec((1, L, D), lambda b, e, idx: (b, 0, 0))],
            out_specs=pl.BlockSpec((1, 1, cap, D), lambda b, e, idx: (e, b, 0, 0)),
            scratch_shapes=[pltpu.VMEM((cap // SUBLANES, SUBLANES, D), F32)],
        ),
        out_shape=jax.ShapeDtypeStruct((E, B, cap, D), BF16),
        compiler_params=_cparams(("arbitrary", "arbitrary")),
        name="expert_gather",
    )(idx_flat, h2)


MXU_TILE = 256


def _ffn_kernel(x_ref, *refs, nchunk):
    wg, wu, wd = refs[:nchunk], refs[nchunk:2 * nchunk], refs[2 * nchunk:3 * nchunk]
    rest = refs[3 * nchunk:]
    x = x_ref[0]
    y = rest[0][0] if len(rest) == 2 else None
    for k in range(nchunk):
        a = jnp.dot(x, wg[k][0].astype(BF16), preferred_element_type=F32)
        u = jnp.dot(x, wu[k][0].astype(BF16), preferred_element_type=F32)
        hmid = (a * jax.nn.sigmoid(a) * u).astype(BF16)
        d = jnp.dot(hmid, wd[k][0].astype(BF16), preferred_element_type=F32)
        y = d if y is None else y + d
    rest[-1][0] = y


def _expert_ffn(xe, w_gate, w_up, w_down, tt, slabs):
    E, T, D = xe.shape
    nchunks = w_gate.shape[2] // MXU_TILE
    per = -(-nchunks // slabs)
    y = None
    for s in range(slabs):
        chunks = range(s * per, min((s + 1) * per, nchunks))
        tile = pl.BlockSpec((1, tt, D), lambda e, i: (e, i, 0))
        col = [pl.BlockSpec((1, D, MXU_TILE), lambda e, i, c=c: (e, 0, c)) for c in chunks]
        row = [pl.BlockSpec((1, MXU_TILE, D), lambda e, i, c=c: (e, c, 0)) for c in chunks]
        n = len(chunks)
        in_specs = [tile] + col + col + row
        args = [xe] + [w_gate] * n + [w_up] * n + [w_down] * n
        if y is not None:
            in_specs.append(tile)
            args.append(y)
        y = pl.pallas_call(
            functools.partial(_ffn_kernel, nchunk=n),
            grid=(E, T // tt),
            in_specs=in_specs,
            out_specs=tile,
            out_shape=jax.ShapeDtypeStruct((E, T, D), F32),
            compiler_params=_cparams(("arbitrary", "arbitrary")),
            name=f"expert_ffn_slab{s}",
        )(*args)
    return y


FIN_ROWS = 512


def _scatter_kernel(idx_ref, gate_ref, y_ref, x_ref, g_ref, o_ref, *, E, cap, L, final):
    b = pl.program_id(0)
    e = pl.program_id(1)
    base = (b * E + e) * cap

    @pl.when(e == 0)
    def _():
        o_ref[...] = jnp.zeros_like(o_ref)

    def body(i, c):
        j0 = i * ROW_UNROLL
        toks = [idx_ref[base + j0 + u] for u in range(ROW_UNROLL)]
        vals = [o_ref[0, pl.ds(toks[u], 1), :]
                + gate_ref[base + j0 + u] * y_ref[0, 0, i, pl.ds(u, 1), :] for u in range(ROW_UNROLL)]
        for u in range(ROW_UNROLL):
            o_ref[0, pl.ds(toks[u], 1), :] = vals[u]
        return c

    lax.fori_loop(0, cap // ROW_UNROLL, body, 0)

    @pl.when(e == E - 1)
    def _():
        def fin(r, c):
            rows = pl.ds(pl.multiple_of(r * FIN_ROWS, FIN_ROWS), FIN_ROWS)
            v = x_ref[0, rows, :] + o_ref[0, rows, :]
            o_ref[0, rows, :] = _rms(v, g_ref[...]) if final else v
            return c

        lax.fori_loop(0, L // FIN_ROWS, fin, 0)


def _scatter(idx_flat, gate_flat, ye, x2, norm_g, final):
    E, B, cap, D = ye.shape
    L = x2.shape[1]
    return pl.pallas_call(
        functools.partial(_scatter_kernel, E=E, cap=cap, L=L, final=final),
        grid_spec=pltpu.PrefetchScalarGridSpec(
            num_scalar_prefetch=2,
            grid=(B, E),
            in_specs=[pl.BlockSpec((1, 1, cap // ROW_UNROLL, ROW_UNROLL, D),
                                   lambda b, e, idx, gt: (e, b, 0, 0, 0)),
                      pl.BlockSpec((1, L, D), lambda b, e, idx, gt: (b, 0, 0),
                                   pipeline_mode=pl.Buffered(1)),
                      pl.BlockSpec((1, D), lambda b, e, idx, gt: (0, 0))],
            out_specs=pl.BlockSpec((1, L, D), lambda b, e, idx, gt: (b, 0, 0)),
        ),
        out_shape=jax.ShapeDtypeStruct((B, L, D), F32),
        compiler_params=_cparams(("arbitrary", "arbitrary")),
        name="expert_scatter_add",
    )(idx_flat, gate_flat, ye.reshape(E, B, cap // ROW_UNROLL, ROW_UNROLL, D), x2,
      norm_g.astype(F32)[None, :])


FFN_SLABS = 2


def _moe(x2, h2, aff_t, w_gate, w_up, w_down, norm_g, tt, final):
    B, L, D = h2.shape
    E = aff_t.shape[1]
    cap = EC_CAPACITY_FACTOR * L // E
    idx, gate = _select(aff_t, cap)
    idx_flat = idx.reshape(-1)
    xe = _gather(idx_flat, h2, E, cap)
    ye = _expert_ffn(xe.reshape(E, B * cap, D), w_gate, w_up, w_down, tt, FFN_SLABS)
    return _scatter(idx_flat, gate.reshape(-1), ye.reshape(E, B, cap, D), x2, norm_g, final)


def kernel(x, attn_norm_g, w_in, conv_w, conv_b, filt_w1, filt_b1, filt_w2, filt_b2, filt_w3, filt_freq, hyena_skip, hyena_norm_g, lambda_q1, lambda_k1, lambda_q2, lambda_k2, subln_g, w_out, ffn_norm_g, w_router, w_gate, w_up, w_down, final_norm_g):
    B, L, D = x.shape
    depth = w_in.shape[0]
    width = hyena_skip.shape[-1]
    hyena_cols = (HYENA_ORDER + 1) * width
    E = w_router.shape[-1]
    tm = min(512, B * L)
    tq = min(512, L)
    consts = _dft_constants(L)
    x2d = x.reshape(B * L, D)
    for l in range(depth):
        lam_init = 0.8 - 0.6 * math.exp(-0.3 * l)
        ph, qkv = _in_proj(x2d, attn_norm_g[l], w_in[l], hyena_cols, tm)
        kf = _hyena_filter_spectra(L, width, filt_w1[l], filt_b1[l], filt_w2[l], filt_b2[l],
                                   filt_w3[l], filt_freq[l], consts)
        hy = _hyena_mixer(ph.reshape(B, L, hyena_cols), conv_w[l], conv_b[l], kf, hyena_skip[l],
                          hyena_norm_g[l], consts)
        at = _diff_attention(qkv.reshape(B, L, -1), lambda_q1[l], lambda_k1[l], lambda_q2[l],
                             lambda_k2[l], subln_g[l], lam_init, min(2 * tq, L), tq)
        x2, h2, aff_t = _out_proj(x2d, hy.reshape(B * L, width), at.reshape(B * L, -1), w_out[l],
                                  ffn_norm_g[l], w_router[l], min(tm // 2, L), L)
        cap = EC_CAPACITY_FACTOR * L // E
        x2d = _moe(x2.reshape(B, L, D), h2.reshape(B, L, D), aff_t, w_gate[l], w_up[l],
                   w_down[l], final_norm_g, tt=min(512, B * cap), final=l + 1 == depth).reshape(B * L, D)
    return x2d.reshape(B, L, D)
```

```python
import functools
import math

import jax
import jax.numpy as jnp
import numpy as np
from jax import lax
from jax.experimental import pallas as pl
from jax.experimental.pallas import tpu as pltpu

F32 = jnp.float32
BF16 = jnp.bfloat16
HIGHEST = lax.Precision.HIGHEST

NORM_EPS = 1e-6
HYENA_ORDER = 2
HYENA_GROUPS = 8
ATTN_HEADS = 4
N_EXPERTS = 16
EC_CAPACITY_FACTOR = 2
FILTER_EMB_DIM = 33
FILTER_EMB_PAD = 40
DECAY_TARGET = 1e-2
FAST_DECAY_PCT = 0.3
SLOW_DECAY_PCT = 1.5

LANES = 128
FFT_FAST = 128
V7X_VMEM_LIMIT = 56 * 1024 * 1024


def _cparams(sem, vmem=V7X_VMEM_LIMIT):
    return pltpu.CompilerParams(dimension_semantics=sem, vmem_limit_bytes=vmem)


SUBLANES = 8


def _half_spectrum(L):
    nk = (2 * L // FFT_FAST) // 2 + 1
    return nk, max(u for u in range(1, 17) if nk % u == 0)


def _dft_constants(L):
    n2 = FFT_FAST
    n = 2 * L
    n1 = n // n2
    s = n1 // 2
    nk, _ = _half_spectrum(L)
    k1 = np.arange(nk)
    th1 = 2.0 * np.pi * np.outer(k1, np.arange(s)) / n1
    f1 = np.empty((2 * nk, s), np.float64)
    f1[0::2], f1[1::2] = np.cos(th1), -np.sin(th1)
    wgt = np.where((k1 == 0) | (k1 == n1 // 2), 1.0, 2.0)[None, :]
    g1 = np.empty((s, 2 * nk), np.float64)
    g1[:, 0::2], g1[:, 1::2] = wgt * np.cos(th1.T), -wgt * np.sin(th1.T)
    eye = np.eye(SUBLANES)
    fwd1 = np.kron(f1, eye)
    inv1 = np.kron(g1, eye)
    a = np.arange(n2)
    ph = 2.0 * np.pi * (np.outer(a, a)[None] / n2 + (k1[:, None, None] * a[None, None, :]) / n)
    tr, ti = np.cos(ph), -np.sin(ph)
    fwd2 = np.concatenate([np.concatenate([tr, -ti], 2), np.concatenate([ti, tr], 2)], 1)
    pht = np.transpose(ph, (0, 2, 1))
    ur, ui = np.cos(pht) / n, np.sin(pht) / n
    inv2 = np.concatenate([np.concatenate([ur, -ui], 2), np.concatenate([ui, ur], 2)], 1)
    return tuple(jnp.asarray(m.astype(np.float32)).astype(BF16) for m in (fwd1, fwd2, inv2, inv1))


def _filter_hidden_kernel(zf_ref, w1_ref, b1_ref, w2_ref, b2_ref, fq_ref, o_ref):
    fq = fq_ref[...]
    hid = jnp.sin(fq * (jnp.dot(zf_ref[...], w1_ref[...], precision=HIGHEST,
                                preferred_element_type=F32) + b1_ref[...]))
    o_ref[...] = jnp.sin(fq * (jnp.dot(hid, w2_ref[...], precision=HIGHEST,
                                       preferred_element_type=F32) + b2_ref[...]))


def _filter_kernel(hid_ref, w3f_ref, w3b_ref, dl_ref, f1_ref, f2_ref, o_ref,
                   hf_ref, hb_ref, sf_ref, sb_ref, *, L):
    s = L // FFT_FAST
    nk, unroll = _half_spectrum(L)
    hid = hid_ref[...]
    row = lax.broadcasted_iota(jnp.int32, (L, 1), 0)
    t = row.astype(F32) * (1.0 / (L - 1))
    win = jnp.exp(-t * dl_ref[...])
    hf = jnp.dot(hid, w3f_ref[...], precision=HIGHEST, preferred_element_type=F32) * win
    hb = jnp.dot(hid, w3b_ref[...], precision=HIGHEST, preferred_element_type=F32) * win
    hb = jnp.where(row == 0, 0.0, hb)
    norm = (jnp.sum(jnp.abs(hf), axis=0, keepdims=True)
            + jnp.sum(jnp.abs(hb), axis=0, keepdims=True))
    cb = hf.shape[1]
    hf_ref[...] = hf.reshape(s, FFT_FAST, cb)
    hb_ref[...] = hb.reshape(s, FFT_FAST, cb)

    def stage1(g, c):
        f0 = pl.multiple_of(g * SUBLANES, SUBLANES)
        xg = jnp.concatenate(
            [hf_ref[:, pl.ds(f0, SUBLANES), :].reshape(s * SUBLANES, cb),
             hb_ref[:, pl.ds(f0, SUBLANES), :].reshape(s * SUBLANES, cb)], axis=1).astype(BF16)
        a = jnp.dot(f1_ref[...], xg, preferred_element_type=F32)
        sf_ref[:, pl.ds(f0, SUBLANES), :] = a[:, :cb].reshape(2 * nk, SUBLANES, cb)
        sb_ref[:, pl.ds(f0, SUBLANES), :] = a[:, cb:].reshape(2 * nk, SUBLANES, cb)
        return c

    lax.fori_loop(0, FFT_FAST // SUBLANES, stage1, 0, unroll=4)
    inv_norm = 1.0 / norm

    def stage2(k1, c):
        blk = jnp.concatenate(
            [sf_ref[pl.ds(2 * k1, 2)].reshape(2 * FFT_FAST, cb),
             sb_ref[pl.ds(2 * k1, 2)].reshape(2 * FFT_FAST, cb)], axis=1).astype(BF16)
        x = jnp.dot(f2_ref[k1], blk, preferred_element_type=F32)
        xf, xb = x[:, :cb], x[:, cb:]
        re = (xf[:FFT_FAST] + xb[:FFT_FAST]) * inv_norm
        im = (xf[FFT_FAST:] - xb[FFT_FAST:]) * inv_norm
        r0 = pl.multiple_of(k1 * 2 * FFT_FAST, 2 * FFT_FAST)
        o_ref[0, pl.ds(r0, 2 * FFT_FAST), :] = jnp.concatenate([re, im], axis=0).astype(o_ref.dtype)
        return c

    lax.fori_loop(0, nk, stage2, 0, unroll=unroll)


def _hyena_filter_spectra(L, width, w1, b1, w2, b2, w3, freq, consts):
    fwd1, fwd2, _, _ = consts
    n1 = 2 * L // FFT_FAST
    nk, _ = _half_spectrum(L)
    cb = LANES
    ncb = width // cb
    hidden = w1.shape[1]
    tt = np.linspace(0.0, 1.0, L)[:, None]
    ww = 2.0 * np.pi * np.arange(L)[:, None] / L
    nb = (FILTER_EMB_DIM - 1) // 2
    bands = np.linspace(1e-4, nb - 1, nb)[None, :]
    feats = np.concatenate([tt, np.cos(bands * ww), -np.sin(bands * ww)], axis=-1)
    feats = np.pad(feats, ((0, 0), (0, FILTER_EMB_PAD - FILTER_EMB_DIM))).astype(np.float32)
    w1p = jnp.pad(w1.astype(F32), ((0, FILTER_EMB_PAD - FILTER_EMB_DIM), (0, 0)))
    min_decay = math.log(DECAY_TARGET) / SLOW_DECAY_PCT
    max_decay = math.log(DECAY_TARGET) / FAST_DECAY_PCT
    deltas = np.abs(np.linspace(min_decay, max_decay, width)).astype(np.float32)[None, :]

    s = n1 // 2
    one = lambda shape: pl.BlockSpec(shape, lambda i: (0,) * len(shape))
    hid = pl.pallas_call(
        _filter_hidden_kernel,
        grid=(1,),
        in_specs=[one((L, FILTER_EMB_PAD)), one((FILTER_EMB_PAD, hidden)), one((1, hidden)),
                  one((hidden, hidden)), one((1, hidden)), one((1, hidden))],
        out_specs=one((L, hidden)),
        out_shape=jax.ShapeDtypeStruct((L, hidden), F32),
        compiler_params=_cparams(("arbitrary",)),
        name="hyena_filter_hidden",
    )(jnp.asarray(feats), w1p, b1.astype(F32)[None, :], w2.astype(F32), b2.astype(F32)[None, :],
      freq.astype(F32)[None, :])
    full = lambda shape: pl.BlockSpec(shape, lambda n, c: (0,) * len(shape))
    return pl.pallas_call(
        functools.partial(_filter_kernel, L=L),
        grid=(HYENA_ORDER, ncb),
        in_specs=[
            full((L, hidden)),
            pl.BlockSpec((hidden, cb), lambda n, c: (0, n * ncb + c)),
            pl.BlockSpec((hidden, cb), lambda n, c: (0, HYENA_ORDER * ncb + n * ncb + c)),
            pl.BlockSpec((1, cb), lambda n, c: (0, c)),
            full(fwd1.shape), full(fwd2.shape),
        ],
        out_specs=pl.BlockSpec((1, nk * 2 * FFT_FAST, cb), lambda n, c: (n, 0, c)),
        out_shape=jax.ShapeDtypeStruct((HYENA_ORDER, nk * 2 * FFT_FAST, width), BF16),
        scratch_shapes=[pltpu.VMEM((s, FFT_FAST, cb), F32), pltpu.VMEM((s, FFT_FAST, cb), F32),
                        pltpu.VMEM((2 * nk, FFT_FAST, cb), F32), pltpu.VMEM((2 * nk, FFT_FAST, cb), F32)],
        compiler_params=_cparams(("arbitrary", "arbitrary")),
        name="hyena_filter_spectra",
    )(hid, w3.astype(F32), w3.astype(F32), jnp.asarray(deltas), fwd1, fwd2)


def _short_conv(p, w, b, L):
    row = lax.broadcasted_iota(jnp.int32, p.shape, 0)
    prev = jnp.where(row == 0, 0.0, pltpu.roll(p, 1, 0))
    nxt = jnp.where(row == L - 1, 0.0, pltpu.roll(p, L - 1, 0))
    return b + prev * w[0:1] + p * w[1:2] + nxt * w[2:3]


def _hyena_kernel(x1_ref, x2_ref, v_ref, w1_ref, w2_ref, wv_ref, b1_ref, b2_ref, bv_ref,
                  skip_ref, ng_ref, kf_ref, f1_ref, f2_ref, f2i_ref, g1_ref, gm_ref,
                  o_ref, z_ref, ga_ref, gb_ref, s_ref, *, L):
    s = L // FFT_FAST
    nk, unroll = _half_spectrum(L)
    cb = o_ref.shape[-1]
    tiled = lambda a: a.reshape(s, FFT_FAST, cb)
    z_ref[...] = tiled(_short_conv(v_ref[0].astype(F32), wv_ref[...], bv_ref[...], L))
    ga_ref[...] = tiled(_short_conv(x1_ref[0].astype(F32), w1_ref[...], b1_ref[...], L))
    gb_ref[...] = tiled(_short_conv(x2_ref[0].astype(F32), w2_ref[...], b2_ref[...], L))
    groups = FFT_FAST // SUBLANES

    for n, gate_ref in enumerate((ga_ref, gb_ref)):
        skip = skip_ref[n:n + 1, :].reshape(1, 1, cb)

        def fwd1(gp, c):
            fa = pl.multiple_of(gp * 2 * SUBLANES, 2 * SUBLANES)
            fb = fa + SUBLANES
            xg = jnp.concatenate(
                [z_ref[:, pl.ds(fa, SUBLANES), :].reshape(s * SUBLANES, cb),
                 z_ref[:, pl.ds(fb, SUBLANES), :].reshape(s * SUBLANES, cb)], axis=1).astype(BF16)
            a = jnp.dot(f1_ref[...], xg, preferred_element_type=F32)
            s_ref[:, pl.ds(fa, SUBLANES), :] = a[:, :cb].reshape(2 * nk, SUBLANES, cb)
            s_ref[:, pl.ds(fb, SUBLANES), :] = a[:, cb:].reshape(2 * nk, SUBLANES, cb)
            return c

        lax.fori_loop(0, groups // 2, fwd1, 0, unroll=4)

        def mid(k1, c):
            blk = s_ref[pl.ds(2 * k1, 2)].reshape(2 * FFT_FAST, cb).astype(BF16)
            x = jnp.dot(f2_ref[k1], blk, preferred_element_type=F32)
            r0 = pl.multiple_of(k1 * 2 * FFT_FAST, 2 * FFT_FAST)
            kk = kf_ref[n, pl.ds(r0, 2 * FFT_FAST), :].astype(F32)
            xr, xi = x[:FFT_FAST], x[FFT_FAST:]
            kr, ki = kk[:FFT_FAST], kk[FFT_FAST:]
            y = jnp.concatenate([xr * kr - xi * ki, xr * ki + xi * kr], axis=0).astype(BF16)
            s_ref[pl.ds(2 * k1, 2)] = jnp.dot(f2i_ref[k1], y, preferred_element_type=F32).reshape(
                2, FFT_FAST, cb)
            return c

        lax.fori_loop(0, nk, mid, 0, unroll=unroll)

        def inv1(gp, c):
            fa = pl.multiple_of(gp * 2 * SUBLANES, 2 * SUBLANES)
            fb = fa + SUBLANES
            bg = jnp.concatenate(
                [s_ref[:, pl.ds(fa, SUBLANES), :].reshape(2 * nk * SUBLANES, cb),
                 s_ref[:, pl.ds(fb, SUBLANES), :].reshape(2 * nk * SUBLANES, cb)], axis=1).astype(BF16)
            y = jnp.dot(g1_ref[...], bg, preferred_element_type=F32)
            for f0, yh in ((fa, y[:, :cb]), (fb, y[:, cb:])):
                zf = z_ref[:, pl.ds(f0, SUBLANES), :]
                gf = gate_ref[:, pl.ds(f0, SUBLANES), :]
                z_ref[:, pl.ds(f0, SUBLANES), :] = gf * (yh.reshape(s, SUBLANES, cb) + skip * zf)
            return c

        lax.fori_loop(0, groups // 2, inv1, 0, unroll=4)

    z = z_ref[...].reshape(L, cb)
    zz = z * z
    hi = zz.astype(BF16)
    lo = (zz - hi.astype(F32)).astype(BF16)
    ms = (jnp.dot(hi, gm_ref[...], preferred_element_type=F32)
          + jnp.dot(lo, gm_ref[...], preferred_element_type=F32))
    o_ref[0] = (z * lax.rsqrt(ms + NORM_EPS) * ng_ref[...]).astype(o_ref.dtype)


def _hyena_mixer(proj_h, conv_w, conv_b, kf, skip, norm_g, consts):
    B, L, cols = proj_h.shape
    width = cols // (HYENA_ORDER + 1)
    cb = 2 * LANES
    ncb = width // cb
    gsz = width // HYENA_GROUPS
    n1 = 2 * L // FFT_FAST
    nk, _ = _half_spectrum(L)
    fwd1, fwd2, inv2, inv1 = consts
    gm = np.kron(np.eye(cb // gsz), np.full((gsz, gsz), 1.0 / gsz)).astype(np.float32)

    def stream(i):
        return pl.BlockSpec((1, L, cb), lambda c, b: (b, 0, i * ncb + c))

    def cvec(rows, i):
        return pl.BlockSpec((rows, cb), lambda c, b: (0, i * ncb + c))

    def const(shape):
        return pl.BlockSpec(shape, lambda c, b: (0,) * len(shape), pipeline_mode=pl.Buffered(1))

    cw = conv_w.astype(F32)
    cbias = conv_b.astype(F32)[None, :]
    return pl.pallas_call(
        functools.partial(_hyena_kernel, L=L),
        grid=(ncb, B),
        in_specs=[
            stream(0), stream(1), stream(2),
            cvec(3, 0), cvec(3, 1), cvec(3, 2),
            cvec(1, 0), cvec(1, 1), cvec(1, 2),
            pl.BlockSpec((HYENA_ORDER, cb), lambda c, b: (0, c)),
            pl.BlockSpec((1, cb), lambda c, b: (0, c)),
            pl.BlockSpec((HYENA_ORDER, nk * 2 * FFT_FAST, cb), lambda c, b: (0, 0, c),
                         pipeline_mode=pl.Buffered(1)),
            const(fwd1.shape), const(fwd2.shape), const(inv2.shape), const(inv1.shape),
            const(gm.shape),
        ],
        out_specs=pl.BlockSpec((1, L, cb), lambda c, b: (b, 0, c)),
        out_shape=jax.ShapeDtypeStruct((B, L, width), BF16),
        scratch_shapes=[pltpu.VMEM((n1 // 2, FFT_FAST, cb), F32), pltpu.VMEM((n1 // 2, FFT_FAST, cb), F32),
                        pltpu.VMEM((n1 // 2, FFT_FAST, cb), F32), pltpu.VMEM((2 * nk, FFT_FAST, cb), F32)],
        compiler_params=_cparams(("arbitrary", "arbitrary")),
        name="hyena_mixer",
    )(proj_h, proj_h, proj_h, cw, cw, cw, cbias, cbias, cbias,
      skip.astype(F32), norm_g.astype(F32)[None, :], kf,
      fwd1, fwd2, inv2, inv1, jnp.asarray(gm).astype(BF16))


def _rms(x, g):
    return x * lax.rsqrt(jnp.mean(x * x, axis=-1, keepdims=True) + NORM_EPS) * g


def _in_proj_kernel(x_ref, g_ref, wh_ref, wa_ref, oh_ref, oa_ref):
    hn = _rms(x_ref[...], g_ref[...]).astype(BF16)
    oh_ref[...] = jnp.dot(hn, wh_ref[...], preferred_element_type=F32).astype(oh_ref.dtype)
    oa_ref[...] = jnp.dot(hn, wa_ref[...], preferred_element_type=F32).astype(oa_ref.dtype)


def _in_proj(x2d, g, w_in, hyena_cols, tm):
    T, D = x2d.shape
    attn_cols = w_in.shape[1] - hyena_cols
    wh = w_in[:, :hyena_cols].astype(BF16)
    wa = w_in[:, hyena_cols:].astype(BF16)
    return pl.pallas_call(
        _in_proj_kernel,
        grid=(T // tm,),
        in_specs=[
            pl.BlockSpec((tm, D), lambda i: (i, 0)),
            pl.BlockSpec((1, D), lambda i: (0, 0)),
            pl.BlockSpec((D, hyena_cols), lambda i: (0, 0)),
            pl.BlockSpec((D, attn_cols), lambda i: (0, 0)),
        ],
        out_specs=[pl.BlockSpec((tm, hyena_cols), lambda i: (i, 0)),
                   pl.BlockSpec((tm, attn_cols), lambda i: (i, 0))],
        out_shape=[jax.ShapeDtypeStruct((T, hyena_cols), BF16),
                   jax.ShapeDtypeStruct((T, attn_cols), BF16)],
        compiler_params=_cparams(("arbitrary",)),
        name="in_proj",
    )(x2d, g.astype(F32)[None, :], wh, wa)


SCORE_BOUND = 60.0


def _attn_kernel(slope_ref, q_ref, k_ref, v_ref, lq1_ref, lk1_ref, lq2_ref, lk2_ref, sg_ref, o_ref,
                 kmax_ref, *, L, tq, tk, dh, lam_init):
    h = pl.program_id(1)
    qi = pl.program_id(2)
    slope = slope_ref[h]
    q = q_ref[0]
    lane = lax.broadcasted_iota(jnp.int32, q.shape, 1)
    qs = q * jnp.asarray(dh ** -0.5, q.dtype)
    q1 = jnp.where(lane < dh, qs, jnp.zeros_like(qs))
    q2 = jnp.where(lane >= dh, qs, jnp.zeros_like(qs))
    rel = (lax.broadcasted_iota(jnp.int32, (tq, tk), 0)
           - lax.broadcasted_iota(jnp.int32, (tq, tk), 1)).astype(F32)
    q0 = (qi * tq).astype(F32)

    def update(s, m, l, acc, vb):
        m_new = jnp.maximum(m, jnp.max(s, axis=-1, keepdims=True))
        alpha = jnp.exp(m - m_new)
        p = jnp.exp(s - m_new)
        l = alpha * l + jnp.sum(p, axis=-1, keepdims=True)
        acc = alpha * acc + jnp.dot(p.astype(BF16), vb, preferred_element_type=F32)
        return m_new, l, acc

    def body(j, carry):
        m1, l1, a1, m2, l2, a2 = carry
        k0 = pl.multiple_of(j * tk, tk)
        kb = k_ref[0, pl.ds(k0, tk), :]
        vb = v_ref[0, pl.ds(k0, tk), :]
        pen = slope * jnp.abs(rel + (q0 - k0.astype(F32)))
        nt = (((1,), (1,)), ((), ()))
        s1 = lax.dot_general(q1, kb, nt, preferred_element_type=F32) - pen
        s2 = lax.dot_general(q2, kb, nt, preferred_element_type=F32) - pen
        m1, l1, a1 = update(s1, m1, l1, a1, vb)
        m2, l2, a2 = update(s2, m2, l2, a2, vb)
        return m1, l1, a1, m2, l2, a2

    def online_softmax():
        neg = jnp.full((tq, 1), -1e30, F32)
        zero = jnp.zeros((tq, 1), F32)
        zacc = jnp.zeros((tq, 2 * dh), F32)
        m1, l1, a1, m2, l2, a2 = lax.fori_loop(0, L // tk, body, (neg, zero, zacc, neg, zero, zacc))
        return a1 / l1, a2 / l2

    @pl.when(qi == 0)
    def _():
        kmax_ref[...] = jnp.max(jnp.abs(k_ref[0].astype(F32)), axis=0, keepdims=True)

    w = jnp.abs(qs.astype(F32)) * kmax_ref[...]
    ub = jnp.maximum(jnp.sum(jnp.where(lane < dh, w, 0.0), axis=-1, keepdims=True),
                     jnp.sum(jnp.where(lane >= dh, w, 0.0), axis=-1, keepdims=True))
    bounded = jnp.max(ub) <= SCORE_BOUND

    def plain_softmax():
        ones = jnp.ones((tk, 2 * dh), BF16)

        nt = (((1,), (1,)), ((), ()))
        a1 = a2 = None
        for j in range(L // tk):
            kb = k_ref[0, j * tk:(j + 1) * tk, :]
            va = jnp.concatenate([v_ref[0, j * tk:(j + 1) * tk, :], ones], axis=1)
            pen = slope * jnp.abs(rel + (q0 - float(j * tk)))
            p1 = jnp.exp(lax.dot_general(q1, kb, nt, preferred_element_type=F32) - pen)
            p2 = jnp.exp(lax.dot_general(q2, kb, nt, preferred_element_type=F32) - pen)
            d1 = jnp.dot(p1.astype(BF16), va, preferred_element_type=F32)
            d2 = jnp.dot(p2.astype(BF16), va, preferred_element_type=F32)
            a1 = d1 if a1 is None else a1 + d1
            a2 = d2 if a2 is None else a2 + d2
        return a1[:, :2 * dh] / a1[:, 2 * dh:], a2[:, :2 * dh] / a2[:, 2 * dh:]

    o1, o2 = lax.cond(bounded, plain_softmax, online_softmax)
    lam = (jnp.exp(jnp.sum(lq1_ref[...] * lk1_ref[...], axis=-1, keepdims=True))
           - jnp.exp(jnp.sum(lq2_ref[...] * lk2_ref[...], axis=-1, keepdims=True)) + lam_init)
    o = o1 - lam * o2
    o = _rms(o, sg_ref[...]) * (1.0 - lam_init)
    o_ref[0] = o.astype(o_ref.dtype)


def _diff_attention(qkv, lq1, lk1, lq2, lk2, subln_g, lam_init, tq, tk):
    B, L, cols = qkv.shape
    W = cols // 3
    H = ATTN_HEADS
    hw = W // H
    dh = hw // 2
    slopes = jnp.asarray([2.0 ** (-8.0 * (i + 1) / H) for i in range(H)], F32)
    vec = lambda a: a.astype(F32)[None, :]
    small = lambda n: pl.BlockSpec((1, n), lambda b, h, i: (0, 0))
    return pl.pallas_call(
        functools.partial(_attn_kernel, L=L, tq=tq, tk=tk, dh=dh, lam_init=lam_init),
        grid=(B, H, L // tq),
        scratch_shapes=[pltpu.VMEM((1, hw), F32)],
        in_specs=[
            pl.BlockSpec(memory_space=pltpu.SMEM),
            pl.BlockSpec((1, tq, hw), lambda b, h, i: (b, i, h)),
            pl.BlockSpec((1, L, hw), lambda b, h, i: (b, 0, H + h)),
            pl.BlockSpec((1, L, hw), lambda b, h, i: (b, 0, 2 * H + h)),
            small(dh), small(dh), small(dh), small(dh), small(hw),
        ],
        out_specs=pl.BlockSpec((1, tq, hw), lambda b, h, i: (b, i, h)),
        out_shape=jax.ShapeDtypeStruct((B, L, W), BF16),
        compiler_params=_cparams(("arbitrary", "arbitrary", "arbitrary")),
        name="diff_attention",
    )(slopes, qkv, qkv, qkv, vec(lq1), vec(lk1), vec(lq2), vec(lk2), vec(subln_g))


def _out_proj_kernel(x_ref, hy_ref, at_ref, wh_ref, wa_ref, g_ref, wr_ref, x2_ref, h2_ref, aff_ref):
    x2 = (x_ref[...] + jnp.dot(hy_ref[...], wh_ref[...], preferred_element_type=F32)
          + jnp.dot(at_ref[...], wa_ref[...], preferred_element_type=F32))
    x2_ref[...] = x2
    h2 = _rms(x2, g_ref[...])
    h2_ref[...] = h2
    wr = wr_ref[...]
    h_hi, w_hi = h2.astype(BF16), wr.astype(BF16)
    h_lo = (h2 - h_hi.astype(F32)).astype(BF16)
    w_lo = (wr - w_hi.astype(F32)).astype(BF16)
    logits = (jnp.dot(h_hi, w_hi, preferred_element_type=F32)
              + (jnp.dot(h_hi, w_lo, preferred_element_type=F32)
                 + jnp.dot(h_lo, w_hi, preferred_element_type=F32)))
    logits = logits.T[:aff_ref.shape[1]]
    e = jnp.exp(logits - jnp.max(logits, axis=0, keepdims=True))
    aff_ref[0] = e / jnp.sum(e, axis=0, keepdims=True)


def _out_proj(x2d, hy2d, at2d, w_out, g, w_router, tm, L):
    T, D = x2d.shape
    wdt = hy2d.shape[1]
    E = w_router.shape[1]
    wh = w_out[:wdt].astype(BF16)
    wa = w_out[wdt:].astype(BF16)
    per = L // tm
    row = lambda n: pl.BlockSpec((tm, n), lambda i: (i, 0))
    return pl.pallas_call(
        _out_proj_kernel,
        grid=(T // tm,),
        in_specs=[row(D), row(wdt), row(at2d.shape[1]),
                  pl.BlockSpec(wh.shape, lambda i: (0, 0)), pl.BlockSpec(wa.shape, lambda i: (0, 0)),
                  pl.BlockSpec((1, D), lambda i: (0, 0)), pl.BlockSpec((D, LANES), lambda i: (0, 0))],
        out_specs=[row(D), row(D), pl.BlockSpec((1, E, tm), lambda i: (i // per, 0, i % per))],
        out_shape=[jax.ShapeDtypeStruct((T, D), F32), jax.ShapeDtypeStruct((T, D), F32),
                   jax.ShapeDtypeStruct((T // L, E, L), F32)],
        compiler_params=_cparams(("arbitrary",)),
        name="out_proj_router",
    )(x2d, hy2d, at2d, wh, wa, g.astype(F32)[None, :],
      jnp.pad(w_router.astype(F32), ((0, 0), (0, LANES - E))))


SEL_CHUNK = 256
SLOT_LO = 32
TOK_LO = 64


def _prefix_count(mask, tri_ref, L):
    parts = []
    carry = jnp.zeros((mask.shape[0], 1), F32)
    for blk in range(L // SEL_CHUNK):
        m = mask[:, blk * SEL_CHUNK:(blk + 1) * SEL_CHUNK]
        parts.append(jnp.dot(m.astype(BF16), tri_ref[...], preferred_element_type=F32) + carry)
        carry = carry + jnp.sum(m, axis=1, keepdims=True)
    return jnp.concatenate(parts, axis=1)


def _select_kernel(aff_ref, tri_ref, idx_ref, gate_ref, *, L, E, cap):
    aff = aff_ref[0]

    def step(i, prefix):
        cand = prefix | jnp.left_shift(jnp.int32(1), 30 - i)
        cnt = jnp.sum((aff >= pltpu.bitcast(cand, F32)).astype(jnp.int32), axis=1, keepdims=True)
        return jnp.where(cnt >= cap, cand, prefix)

    tau = pltpu.bitcast(lax.fori_loop(0, 31, step, jnp.zeros((E, 1), jnp.int32)), F32)
    gt = aff > tau
    eq = aff == tau
    need = cap - jnp.sum(gt.astype(jnp.int32), axis=1, keepdims=True)
    tie_rank = _prefix_count(eq.astype(F32), tri_ref, L) - eq.astype(F32)
    sel = gt | (eq & (tie_rank < need.astype(F32)))
    rank = _prefix_count(sel.astype(F32), tri_ref, L)
    slot = jnp.where(sel, rank - 1.0, -1.0)

    n_hi = cap // SLOT_LO
    s_hi = jnp.floor(slot * (1.0 / SLOT_LO))
    s_lo = slot - SLOT_LO * s_hi
    tok = lax.broadcasted_iota(jnp.int32, (1, L), 1).astype(F32)
    t_hi = jnp.floor(tok * (1.0 / TOK_LO))
    t_lo = tok - TOK_LO * t_hi
    g_hi = aff.astype(BF16)
    r1 = aff - g_hi.astype(F32)
    g_mid = r1.astype(BF16)
    g_lo = (r1 - g_mid.astype(F32)).astype(BF16)
    hi_ids = lax.broadcasted_iota(jnp.int32, (n_hi, 1), 0).astype(F32)
    lo_ids = lax.broadcasted_iota(jnp.int32, (SLOT_LO, 1), 0).astype(F32)
    nt = (((1,), (1,)), ((), ()))
    for e in range(E):
        in_hi = s_hi[e:e + 1, :] == hi_ids
        pieces = [t_hi, t_lo, g_hi[e:e + 1, :].astype(F32), g_mid[e:e + 1, :].astype(F32),
                  g_lo[e:e + 1, :].astype(F32)]
        lhs = jnp.concatenate([jnp.where(in_hi, p, 0.0) for p in pieces], axis=0).astype(BF16)
        in_lo = (s_lo[e:e + 1, :] == lo_ids).astype(BF16)
        out = lax.dot_general(lhs, in_lo, nt, preferred_element_type=F32)
        parts = [out[k * n_hi:(k + 1) * n_hi] for k in range(5)]
        idx_ref[0, e] = (parts[0] * TOK_LO + parts[1]).astype(jnp.int32)
        gate_ref[0, e] = parts[2] + (parts[3] + parts[4])


def _select(aff_t, cap):
    B, E, L = aff_t.shape
    n_hi = cap // SLOT_LO
    tri = np.triu(np.ones((SEL_CHUNK, SEL_CHUNK), np.float32))
    slots = pl.BlockSpec((1, E, n_hi, SLOT_LO), lambda b: (b, 0, 0, 0))
    idx, gate = pl.pallas_call(
        functools.partial(_select_kernel, L=L, E=E, cap=cap),
        grid=(B,),
        in_specs=[pl.BlockSpec((1, E, L), lambda b: (b, 0, 0)),
                  pl.BlockSpec(tri.shape, lambda b: (0, 0))],
        out_specs=[slots, slots],
        out_shape=[jax.ShapeDtypeStruct((B, E, n_hi, SLOT_LO), jnp.int32),
                   jax.ShapeDtypeStruct((B, E, n_hi, SLOT_LO), F32)],
        compiler_params=_cparams(("arbitrary",)),
        name="expert_choice_select",
    )(aff_t, jnp.asarray(tri).astype(BF16))
    return idx.reshape(B, E, cap), gate.reshape(B, E, cap)


ROW_UNROLL = 8


def _gather_kernel(idx_ref, h_ref, o_ref, buf_ref, *, E, cap):
    b = pl.program_id(0)
    e = pl.program_id(1)
    base = (b * E + e) * cap

    def body(i, c):
        for u in range(SUBLANES):
            t = idx_ref[base + i * SUBLANES + u]
            buf_ref[i, pl.ds(u, 1), :] = h_ref[0, pl.ds(t, 1), :]
        return c

    lax.fori_loop(0, cap // SUBLANES, body, 0)
    o_ref[0, 0] = buf_ref[...].reshape(cap, buf_ref.shape[-1]).astype(o_ref.dtype)


def _gather(idx_flat, h2, E, cap):
    B, L, D = h2.shape
    return pl.pallas_call(
        functools.partial(_gather_kernel, E=E, cap=cap),
        grid_spec=pltpu.PrefetchScalarGridSpec(
            num_scalar_prefetch=1,
            grid=(B, E),
            in_specs=[pl.BlockSpec((1, L, D), lambda b, e, idx: (b, 0, 0))],
            out_specs=pl.BlockSpec((1, 1, cap, D), lambda b, e, idx: (e, b, 0, 0)),
            scratch_shapes=[pltpu.VMEM((cap // SUBLANES, SUBLANES, D), F32)],
        ),
        out_shape=jax.ShapeDtypeStruct((E, B, cap, D), BF16),
        compiler_params=_cparams(("arbitrary", "arbitrary")),
        name="expert_gather",
    )(idx_flat, h2)


MXU_TILE = 256


def _ffn_kernel(x_ref, *refs, nchunk):
    wg, wu, wd = refs[:nchunk], refs[nchunk:2 * nchunk], refs[2 * nchunk:3 * nchunk]
    rest = refs[3 * nchunk:]
    x = x_ref[0]
    y = rest[0][0] if len(rest) == 2 else None
    for k in range(nchunk):
        a = jnp.dot(x, wg[k][0].astype(BF16), preferred_element_type=F32)
        u = jnp.dot(x, wu[k][0].astype(BF16), preferred_element_type=F32)
        hmid = (a * jax.nn.sigmoid(a) * u).astype(BF16)
        d = jnp.dot(hmid, wd[k][0].astype(BF16), preferred_element_type=F32)
        y = d if y is None else y + d
    rest[-1][0] = y


def _expert_ffn(xe, w_gate, w_up, w_down, tt, slabs):
    E, T, D = xe.shape
    nchunks = w_gate.shape[2] // MXU_TILE
    per = -(-nchunks // slabs)
    y = None
    for s in range(slabs):
        chunks = range(s * per, min((s + 1) * per, nchunks))
        tile = pl.BlockSpec((1, tt, D), lambda e, i: (e, i, 0))
        col = [pl.BlockSpec((1, D, MXU_TILE), lambda e, i, c=c: (e, 0, c)) for c in chunks]
        row = [pl.BlockSpec((1, MXU_TILE, D), lambda e, i, c=c: (e, c, 0)) for c in chunks]
        n = len(chunks)
        in_specs = [tile] + col + col + row
        args = [xe] + [w_gate] * n + [w_up] * n + [w_down] * n
        if y is not None:
            in_specs.append(tile)
            args.append(y)
        y = pl.pallas_call(
            functools.partial(_ffn_kernel, nchunk=n),
            grid=(E, T // tt),
            in_specs=in_specs,
            out_specs=tile,
            out_shape=jax.ShapeDtypeStruct((E, T, D), F32),
            compiler_params=_cparams(("arbitrary", "arbitrary")),
            name=f"expert_ffn_slab{s}",
        )(*args)
    return y


FIN_ROWS = 512


def _scatter_kernel(idx_ref, gate_ref, y_ref, x_ref, g_ref, o_ref, *, E, cap, L, final):
    b = pl.program_id(0)
    e = pl.program_id(1)
    base = (b * E + e) * cap

    @pl.when(e == 0)
    def _():
        o_ref[...] = jnp.zeros_like(o_ref)

    def body(i, c):
        j0 = i * ROW_UNROLL
        toks = [idx_ref[base + j0 + u] for u in range(ROW_UNROLL)]
        vals = [o_ref[0, pl.ds(toks[u], 1), :]
                + gate_ref[base + j0 + u] * y_ref[0, 0, i, pl.ds(u, 1), :] for u in range(ROW_UNROLL)]
        for u in range(ROW_UNROLL):
            o_ref[0, pl.ds(toks[u], 1), :] = vals[u]
        return c

    lax.fori_loop(0, cap // ROW_UNROLL, body, 0)

    @pl.when(e == E - 1)
    def _():
        def fin(r, c):
            rows = pl.ds(pl.multiple_of(r * FIN_ROWS, FIN_ROWS), FIN_ROWS)
            v = x_ref[0, rows, :] + o_ref[0, rows, :]
            o_ref[0, rows, :] = _rms(v, g_ref[...]) if final else v
            return c

        lax.fori_loop(0, L // FIN_ROWS, fin, 0)


def _scatter(idx_flat, gate_flat, ye, x2, norm_g, final):
    E, B, cap, D = ye.shape
    L = x2.shape[1]
    return pl.pallas_call(
        functools.partial(_scatter_kernel, E=E, cap=cap, L=L, final=final),
        grid_spec=pltpu.PrefetchScalarGridSpec(
            num_scalar_prefetch=2,
            grid=(B, E),
            in_specs=[pl.BlockSpec((1, 1, cap // ROW_UNROLL, ROW_UNROLL, D),
                                   lambda b, e, idx, gt: (e, b, 0, 0, 0)),
                      pl.BlockSpec((1, L, D), lambda b, e, idx, gt: (b, 0, 0),
                                   pipeline_mode=pl.Buffered(1)),
                      pl.BlockSpec((1, D), lambda b, e, idx, gt: (0, 0))],
            out_specs=pl.BlockSpec((1, L, D), lambda b, e, idx, gt: (b, 0, 0)),
        ),
        out_shape=jax.ShapeDtypeStruct((B, L, D), F32),
        compiler_params=_cparams(("arbitrary", "arbitrary")),
        name="expert_scatter_add",
    )(idx_flat, gate_flat, ye.reshape(E, B, cap // ROW_UNROLL, ROW_UNROLL, D), x2,
      norm_g.astype(F32)[None, :])


FFN_SLABS = 2


def _moe(x2, h2, aff_t, w_gate, w_up, w_down, norm_g, tt, final):
    B, L, D = h2.shape
    E = aff_t.shape[1]
    cap = EC_CAPACITY_FACTOR * L // E
    idx, gate = _select(aff_t, cap)
    idx_flat = idx.reshape(-1)
    xe = _gather(idx_flat, h2, E, cap)
    ye = _expert_ffn(xe.reshape(E, B * cap, D), w_gate, w_up, w_down, tt, FFN_SLABS)
    return _scatter(idx_flat, gate.reshape(-1), ye.reshape(E, B, cap, D), x2, norm_g, final)


def kernel(x, attn_norm_g, w_in, conv_w, conv_b, filt_w1, filt_b1, filt_w2, filt_b2, filt_w3, filt_freq, hyena_skip, hyena_norm_g, lambda_q1, lambda_k1, lambda_q2, lambda_k2, subln_g, w_out, ffn_norm_g, w_router, w_gate, w_up, w_down, final_norm_g):
    B, L, D = x.shape
    depth = w_in.shape[0]
    width = hyena_skip.shape[-1]
    hyena_cols = (HYENA_ORDER + 1) * width
    E = w_router.shape[-1]
    tm = min(512, B * L)
    tq = min(512, L)
    consts = _dft_constants(L)
    x2d = x.reshape(B * L, D)
    for l in range(depth):
        lam_init = 0.8 - 0.6 * math.exp(-0.3 * l)
        ph, qkv = _in_proj(x2d, attn_norm_g[l], w_in[l], hyena_cols, tm)
        kf = _hyena_filter_spectra(L, width, filt_w1[l], filt_b1[l], filt_w2[l], filt_b2[l],
                                   filt_w3[l], filt_freq[l], consts)
        hy = _hyena_mixer(ph.reshape(B, L, hyena_cols), conv_w[l], conv_b[l], kf, hyena_skip[l],
                          hyena_norm_g[l], consts)
        at = _diff_attention(qkv.reshape(B, L, -1), lambda_q1[l], lambda_k1[l], lambda_q2[l],
                             lambda_k2[l], subln_g[l], lam_init, min(2 * tq, L), tq)
        x2, h2, aff_t = _out_proj(x2d, hy.reshape(B * L, width), at.reshape(B * L, -1), w_out[l],
                                  ffn_norm_g[l], w_router[l], min(tm // 2, L), L)
        cap = EC_CAPACITY_FACTOR * L // E
        x2d = _moe(x2.reshape(B, L, D), h2.reshape(B, L, D), aff_t, w_gate[l], w_up[l],
                   w_down[l], final_norm_g, tt=min(512, B * cap), final=l + 1 == depth).reshape(B * L, D)
    return x2d.reshape(B, L, D)
```

```python
import functools
import math

import jax
import jax.numpy as jnp
import numpy as np
from jax import lax
from jax.experimental import pallas as pl
from jax.experimental.pallas import tpu as pltpu

F32 = jnp.float32
BF16 = jnp.bfloat16
HIGHEST = lax.Precision.HIGHEST

NORM_EPS = 1e-6
HYENA_ORDER = 2
HYENA_GROUPS = 8
ATTN_HEADS = 4
N_EXPERTS = 16
EC_CAPACITY_FACTOR = 2
FILTER_EMB_DIM = 33
FILTER_EMB_PAD = 40
DECAY_TARGET = 1e-2
FAST_DECAY_PCT = 0.3
SLOW_DECAY_PCT = 1.5

LANES = 128
FFT_FAST = 128
V7X_VMEM_LIMIT = 56 * 1024 * 1024


def _cparams(sem, vmem=V7X_VMEM_LIMIT):
    return pltpu.CompilerParams(dimension_semantics=sem, vmem_limit_bytes=vmem)


SUBLANES = 8


def _half_spectrum(L):
    nk = (2 * L // FFT_FAST) // 2 + 1
    return nk, max(u for u in range(1, 17) if nk % u == 0)


def _dft_constants(L):
    n2 = FFT_FAST
    n = 2 * L
    n1 = n // n2
    s = n1 // 2
    nk, _ = _half_spectrum(L)
    k1 = np.arange(nk)
    th1 = 2.0 * np.pi * np.outer(k1, np.arange(s)) / n1
    f1 = np.empty((2 * nk, s), np.float64)
    f1[0::2], f1[1::2] = np.cos(th1), -np.sin(th1)
    wgt = np.where((k1 == 0) | (k1 == n1 // 2), 1.0, 2.0)[None, :]
    g1 = np.empty((s, 2 * nk), np.float64)
    g1[:, 0::2], g1[:, 1::2] = wgt * np.cos(th1.T), -wgt * np.sin(th1.T)
    eye = np.eye(SUBLANES)
    fwd1 = np.kron(f1, eye)
    inv1 = np.kron(g1, eye)
    a = np.arange(n2)
    ph = 2.0 * np.pi * (np.outer(a, a)[None] / n2 + (k1[:, None, None] * a[None, None, :]) / n)
    tr, ti = np.cos(ph), -np.sin(ph)
    fwd2 = np.concatenate([np.concatenate([tr, -ti], 2), np.concatenate([ti, tr], 2)], 1)
    pht = np.transpose(ph, (0, 2, 1))
    ur, ui = np.cos(pht) / n, np.sin(pht) / n
    inv2 = np.concatenate([np.concatenate([ur, -ui], 2), np.concatenate([ui, ur], 2)], 1)
    return tuple(jnp.asarray(m.astype(np.float32)).astype(BF16) for m in (fwd1, fwd2, inv2, inv1))


def _filter_hidden_kernel(zf_ref, w1_ref, b1_ref, w2_ref, b2_ref, fq_ref, o_ref):
    fq = fq_ref[...]
    hid = jnp.sin(fq * (jnp.dot(zf_ref[...], w1_ref[...], precision=HIGHEST,
                                preferred_element_type=F32) + b1_ref[...]))
    o_ref[...] = jnp.sin(fq * (jnp.dot(hid, w2_ref[...], precision=HIGHEST,
                                       preferred_element_type=F32) + b2_ref[...]))


def _filter_kernel(hid_ref, w3f_ref, w3b_ref, dl_ref, f1_ref, f2_ref, o_ref,
                   hf_ref, hb_ref, sf_ref, sb_ref, *, L):
    s = L // FFT_FAST
    nk, unroll = _half_spectrum(L)
    hid = hid_ref[...]
    row = lax.broadcasted_iota(jnp.int32, (L, 1), 0)
    t = row.astype(F32) * (1.0 / (L - 1))
    win = jnp.exp(-t * dl_ref[...])
    hf = jnp.dot(hid, w3f_ref[...], precision=HIGHEST, preferred_element_type=F32) * win
    hb = jnp.dot(hid, w3b_ref[...], precision=HIGHEST, preferred_element_type=F32) * win
    hb = jnp.where(row == 0, 0.0, hb)
    norm = (jnp.sum(jnp.abs(hf), axis=0, keepdims=True)
            + jnp.sum(jnp.abs(hb), axis=0, keepdims=True))
    cb = hf.shape[1]
    hf_ref[...] = hf.reshape(s, FFT_FAST, cb)
    hb_ref[...] = hb.reshape(s, FFT_FAST, cb)

    def stage1(g, c):
        f0 = pl.multiple_of(g * SUBLANES, SUBLANES)
        xg = jnp.concatenate(
            [hf_ref[:, pl.ds(f0, SUBLANES), :].reshape(s * SUBLANES, cb),
             hb_ref[:, pl.ds(f0, SUBLANES), :].reshape(s * SUBLANES, cb)], axis=1).astype(BF16)
        a = jnp.dot(f1_ref[...], xg, preferred_element_type=F32)
        sf_ref[:, pl.ds(f0, SUBLANES), :] = a[:, :cb].reshape(2 * nk, SUBLANES, cb)
        sb_ref[:, pl.ds(f0, SUBLANES), :] = a[:, cb:].reshape(2 * nk, SUBLANES, cb)
        return c

    lax.fori_loop(0, FFT_FAST // SUBLANES, stage1, 0, unroll=4)
    inv_norm = 1.0 / norm

    def stage2(k1, c):
        blk = jnp.concatenate(
            [sf_ref[pl.ds(2 * k1, 2)].reshape(2 * FFT_FAST, cb),
             sb_ref[pl.ds(2 * k1, 2)].reshape(2 * FFT_FAST, cb)], axis=1).astype(BF16)
        x = jnp.dot(f2_ref[k1], blk, preferred_element_type=F32)
        xf, xb = x[:, :cb], x[:, cb:]
        re = (xf[:FFT_FAST] + xb[:FFT_FAST]) * inv_norm
        im = (xf[FFT_FAST:] - xb[FFT_FAST:]) * inv_norm
        r0 = pl.multiple_of(k1 * 2 * FFT_FAST, 2 * FFT_FAST)
        o_ref[0, pl.ds(r0, 2 * FFT_FAST), :] = jnp.concatenate([re, im], axis=0).astype(o_ref.dtype)
        return c

    lax.fori_loop(0, nk, stage2, 0, unroll=unroll)


def _hyena_filter_spectra(L, width, w1, b1, w2, b2, w3, freq, consts):
    fwd1, fwd2, _, _ = consts
    n1 = 2 * L // FFT_FAST
    nk, _ = _half_spectrum(L)
    cb = LANES
    ncb = width // cb
    hidden = w1.shape[1]
    tt = np.linspace(0.0, 1.0, L)[:, None]
    ww = 2.0 * np.pi * np.arange(L)[:, None] / L
    nb = (FILTER_EMB_DIM - 1) // 2
    bands = np.linspace(1e-4, nb - 1, nb)[None, :]
    feats = np.concatenate([tt, np.cos(bands * ww), -np.sin(bands * ww)], axis=-1)
    feats = np.pad(feats, ((0, 0), (0, FILTER_EMB_PAD - FILTER_EMB_DIM))).astype(np.float32)
    w1p = jnp.pad(w1.astype(F32), ((0, FILTER_EMB_PAD - FILTER_EMB_DIM), (0, 0)))
    min_decay = math.log(DECAY_TARGET) / SLOW_DECAY_PCT
    max_decay = math.log(DECAY_TARGET) / FAST_DECAY_PCT
    deltas = np.abs(np.linspace(min_decay, max_decay, width)).astype(np.float32)[None, :]

    s = n1 // 2
    one = lambda shape: pl.BlockSpec(shape, lambda i: (0,) * len(shape))
    hid = pl.pallas_call(
        _filter_hidden_kernel,
        grid=(1,),
        in_specs=[one((L, FILTER_EMB_PAD)), one((FILTER_EMB_PAD, hidden)), one((1, hidden)),
                  one((hidden, hidden)), one((1, hidden)), one((1, hidden))],
        out_specs=one((L, hidden)),
        out_shape=jax.ShapeDtypeStruct((L, hidden), F32),
        compiler_params=_cparams(("arbitrary",)),
        name="hyena_filter_hidden",
    )(jnp.asarray(feats), w1p, b1.astype(F32)[None, :], w2.astype(F32), b2.astype(F32)[None, :],
      freq.astype(F32)[None, :])
    full = lambda shape: pl.BlockSpec(shape, lambda n, c: (0,) * len(shape))
    return pl.pallas_call(
        functools.partial(_filter_kernel, L=L),
        grid=(HYENA_ORDER, ncb),
        in_specs=[
            full((L, hidden)),
            pl.BlockSpec((hidden, cb), lambda n, c: (0, n * ncb + c)),
            pl.BlockSpec((hidden, cb), lambda n, c: (0, HYENA_ORDER * ncb + n * ncb + c)),
            pl.BlockSpec((1, cb), lambda n, c: (0, c)),
            full(fwd1.shape), full(fwd2.shape),
        ],
        out_specs=pl.BlockSpec((1, nk * 2 * FFT_FAST, cb), lambda n, c: (n, 0, c)),
        out_shape=jax.ShapeDtypeStruct((HYENA_ORDER, nk * 2 * FFT_FAST, width), BF16),
        scratch_shapes=[pltpu.VMEM((s, FFT_FAST, cb), F32), pltpu.VMEM((s, FFT_FAST, cb), F32),
                        pltpu.VMEM((2 * nk, FFT_FAST, cb), F32), pltpu.VMEM((2 * nk, FFT_FAST, cb), F32)],
        compiler_params=_cparams(("arbitrary", "arbitrary")),
        name="hyena_filter_spectra",
    )(hid, w3.astype(F32), w3.astype(F32), jnp.asarray(deltas), fwd1, fwd2)


def _short_conv(p, w, b, L):
    row = lax.broadcasted_iota(jnp.int32, p.shape, 0)
    prev = jnp.where(row == 0, 0.0, pltpu.roll(p, 1, 0))
    nxt = jnp.where(row == L - 1, 0.0, pltpu.roll(p, L - 1, 0))
    return b + prev * w[0:1] + p * w[1:2] + nxt * w[2:3]


def _hyena_kernel(x1_ref, x2_ref, v_ref, w1_ref, w2_ref, wv_ref, b1_ref, b2_ref, bv_ref,
                  skip_ref, ng_ref, kf_ref, f1_ref, f2_ref, f2i_ref, g1_ref, gm_ref,
                  o_ref, z_ref, ga_ref, gb_ref, s_ref, *, L):
    s = L // FFT_FAST
    nk, unroll = _half_spectrum(L)
    cb = o_ref.shape[-1]
    tiled = lambda a: a.reshape(s, FFT_FAST, cb)
    z_ref[...] = tiled(_short_conv(v_ref[0].astype(F32), wv_ref[...], bv_ref[...], L))
    ga_ref[...] = tiled(_short_conv(x1_ref[0].astype(F32), w1_ref[...], b1_ref[...], L))
    gb_ref[...] = tiled(_short_conv(x2_ref[0].astype(F32), w2_ref[...], b2_ref[...], L))
    groups = FFT_FAST // SUBLANES

    for n, gate_ref in enumerate((ga_ref, gb_ref)):
        skip = skip_ref[n:n + 1, :].reshape(1, 1, cb)

        def fwd1(gp, c):
            fa = pl.multiple_of(gp * 2 * SUBLANES, 2 * SUBLANES)
            fb = fa + SUBLANES
            xg = jnp.concatenate(
                [z_ref[:, pl.ds(fa, SUBLANES), :].reshape(s * SUBLANES, cb),
                 z_ref[:, pl.ds(fb, SUBLANES), :].reshape(s * SUBLANES, cb)], axis=1).astype(BF16)
            a = jnp.dot(f1_ref[...], xg, preferred_element_type=F32)
            s_ref[:, pl.ds(fa, SUBLANES), :] = a[:, :cb].reshape(2 * nk, SUBLANES, cb)
            s_ref[:, pl.ds(fb, SUBLANES), :] = a[:, cb:].reshape(2 * nk, SUBLANES, cb)
            return c

        lax.fori_loop(0, groups // 2, fwd1, 0, unroll=4)

        def mid(k1, c):
            blk = s_ref[pl.ds(2 * k1, 2)].reshape(2 * FFT_FAST, cb).astype(BF16)
            x = jnp.dot(f2_ref[k1], blk, preferred_element_type=F32)
            r0 = pl.multiple_of(k1 * 2 * FFT_FAST, 2 * FFT_FAST)
            kk = kf_ref[n, pl.ds(r0, 2 * FFT_FAST), :].astype(F32)
            xr, xi = x[:FFT_FAST], x[FFT_FAST:]
            kr, ki = kk[:FFT_FAST], kk[FFT_FAST:]
            y = jnp.concatenate([xr * kr - xi * ki, xr * ki + xi * kr], axis=0).astype(BF16)
            s_ref[pl.ds(2 * k1, 2)] = jnp.dot(f2i_ref[k1], y, preferred_element_type=F32).reshape(
                2, FFT_FAST, cb)
            return c

        lax.fori_loop(0, nk, mid, 0, unroll=unroll)

        def inv1(gp, c):
            fa = pl.multiple_of(gp * 2 * SUBLANES, 2 * SUBLANES)
            fb = fa + SUBLANES
            bg = jnp.concatenate(
                [s_ref[:, pl.ds(fa, SUBLANES), :].reshape(2 * nk * SUBLANES, cb),
                 s_ref[:, pl.ds(fb, SUBLANES), :].reshape(2 * nk * SUBLANES, cb)], axis=1).astype(BF16)
            y = jnp.dot(g1_ref[...], bg, preferred_element_type=F32)
            for f0, yh in ((fa, y[:, :cb]), (fb, y[:, cb:])):
                zf = z_ref[:, pl.ds(f0, SUBLANES), :]
                gf = gate_ref[:, pl.ds(f0, SUBLANES), :]
                z_ref[:, pl.ds(f0, SUBLANES), :] = gf * (yh.reshape(s, SUBLANES, cb) + skip * zf)
            return c

        lax.fori_loop(0, groups // 2, inv1, 0, unroll=4)

    z = z_ref[...].reshape(L, cb)
    zz = z * z
    hi = zz.astype(BF16)
    lo = (zz - hi.astype(F32)).astype(BF16)
    ms = (jnp.dot(hi, gm_ref[...], preferred_element_type=F32)
          + jnp.dot(lo, gm_ref[...], preferred_element_type=F32))
    o_ref[0] = (z * lax.rsqrt(ms + NORM_EPS) * ng_ref[...]).astype(o_ref.dtype)


def _hyena_mixer(proj_h, conv_w, conv_b, kf, skip, norm_g, consts):
    B, L, cols = proj_h.shape
    width = cols // (HYENA_ORDER + 1)
    cb = 2 * LANES
    ncb = width // cb
    gsz = width // HYENA_GROUPS
    n1 = 2 * L // FFT_FAST
    nk, _ = _half_spectrum(L)
    fwd1, fwd2, inv2, inv1 = consts
    gm = np.kron(np.eye(cb // gsz), np.full((gsz, gsz), 1.0 / gsz)).astype(np.float32)

    def stream(i):
        return pl.BlockSpec((1, L, cb), lambda c, b: (b, 0, i * ncb + c))

    def cvec(rows, i):
        return pl.BlockSpec((rows, cb), lambda c, b: (0, i * ncb + c))

    def const(shape):
        return pl.BlockSpec(shape, lambda c, b: (0,) * len(shape), pipeline_mode=pl.Buffered(1))

    cw = conv_w.astype(F32)
    cbias = conv_b.astype(F32)[None, :]
    return pl.pallas_call(
        functools.partial(_hyena_kernel, L=L),
        grid=(ncb, B),
        in_specs=[
            stream(0), stream(1), stream(2),
            cvec(3, 0), cvec(3, 1), cvec(3, 2),
            cvec(1, 0), cvec(1, 1), cvec(1, 2),
            pl.BlockSpec((HYENA_ORDER, cb), lambda c, b: (0, c)),
            pl.BlockSpec((1, cb), lambda c, b: (0, c)),
            pl.BlockSpec((HYENA_ORDER, nk * 2 * FFT_FAST, cb), lambda c, b: (0, 0, c),
                         pipeline_mode=pl.Buffered(1)),
            const(fwd1.shape), const(fwd2.shape), const(inv2.shape), const(inv1.shape),
            const(gm.shape),
        ],
        out_specs=pl.BlockSpec((1, L, cb), lambda c, b: (b, 0, c)),
        out_shape=jax.ShapeDtypeStruct((B, L, width), BF16),
        scratch_shapes=[pltpu.VMEM((n1 // 2, FFT_FAST, cb), F32), pltpu.VMEM((n1 // 2, FFT_FAST, cb), F32),
                        pltpu.VMEM((n1 // 2, FFT_FAST, cb), F32), pltpu.VMEM((2 * nk, FFT_FAST, cb), F32)],
        compiler_params=_cparams(("arbitrary", "arbitrary")),
        name="hyena_mixer",
    )(proj_h, proj_h, proj_h, cw, cw, cw, cbias, cbias, cbias,
      skip.astype(F32), norm_g.astype(F32)[None, :], kf,
      fwd1, fwd2, inv2, inv1, jnp.asarray(gm).astype(BF16))


def _rms(x, g):
    return x * lax.rsqrt(jnp.mean(x * x, axis=-1, keepdims=True) + NORM_EPS) * g


def _in_proj_kernel(x_ref, g_ref, wh_ref, wa_ref, oh_ref, oa_ref):
    hn = _rms(x_ref[...], g_ref[...]).astype(BF16)
    oh_ref[...] = jnp.dot(hn, wh_ref[...], preferred_element_type=F32).astype(oh_ref.dtype)
    oa_ref[...] = jnp.dot(hn, wa_ref[...], preferred_element_type=F32).astype(oa_ref.dtype)


def _in_proj(x2d, g, w_in, hyena_cols, tm):
    T, D = x2d.shape
    attn_cols = w_in.shape[1] - hyena_cols
    wh = w_in[:, :hyena_cols].astype(BF16)
    wa = w_in[:, hyena_cols:].astype(BF16)
    return pl.pallas_call(
        _in_proj_kernel,
        grid=(T // tm,),
        in_specs=[
            pl.BlockSpec((tm, D), lambda i: (i, 0)),
            pl.BlockSpec((1, D), lambda i: (0, 0)),
            pl.BlockSpec((D, hyena_cols), lambda i: (0, 0)),
            pl.BlockSpec((D, attn_cols), lambda i: (0, 0)),
        ],
        out_specs=[pl.BlockSpec((tm, hyena_cols), lambda i: (i, 0)),
                   pl.BlockSpec((tm, attn_cols), lambda i: (i, 0))],
        out_shape=[jax.ShapeDtypeStruct((T, hyena_cols), BF16),
                   jax.ShapeDtypeStruct((T, attn_cols), BF16)],
        compiler_params=_cparams(("arbitrary",)),
        name="in_proj",
    )(x2d, g.astype(F32)[None, :], wh, wa)


SCORE_BOUND = 60.0


def _attn_kernel(slope_ref, q_ref, k_ref, v_ref, lq1_ref, lk1_ref, lq2_ref, lk2_ref, sg_ref, o_ref,
                 kmax_ref, *, L, tq, tk, dh, lam_init):
    h = pl.program_id(1)
    qi = pl.program_id(2)
    slope = slope_ref[h]
    q = q_ref[0]
    lane = lax.broadcasted_iota(jnp.int32, q.shape, 1)
    qs = q * jnp.asarray(dh ** -0.5, q.dtype)
    q1 = jnp.where(lane < dh, qs, jnp.zeros_like(qs))
    q2 = jnp.where(lane >= dh, qs, jnp.zeros_like(qs))
    rel = (lax.broadcasted_iota(jnp.int32, (tq, tk), 0)
           - lax.broadcasted_iota(jnp.int32, (tq, tk), 1)).astype(F32)
    q0 = (qi * tq).astype(F32)

    def update(s, m, l, acc, vb):
        m_new = jnp.maximum(m, jnp.max(s, axis=-1, keepdims=True))
        alpha = jnp.exp(m - m_new)
        p = jnp.exp(s - m_new)
        l = alpha * l + jnp.sum(p, axis=-1, keepdims=True)
        acc = alpha * acc + jnp.dot(p.astype(BF16), vb, preferred_element_type=F32)
        return m_new, l, acc

    def body(j, carry):
        m1, l1, a1, m2, l2, a2 = carry
        k0 = pl.multiple_of(j * tk, tk)
        kb = k_ref[0, pl.ds(k0, tk), :]
        vb = v_ref[0, pl.ds(k0, tk), :]
        pen = slope * jnp.abs(rel + (q0 - k0.astype(F32)))
        nt = (((1,), (1,)), ((), ()))
        s1 = lax.dot_general(q1, kb, nt, preferred_element_type=F32) - pen
        s2 = lax.dot_general(q2, kb, nt, preferred_element_type=F32) - pen
        m1, l1, a1 = update(s1, m1, l1, a1, vb)
        m2, l2, a2 = update(s2, m2, l2, a2, vb)
        return m1, l1, a1, m2, l2, a2

    def online_softmax():
        neg = jnp.full((tq, 1), -1e30, F32)
        zero = jnp.zeros((tq, 1), F32)
        zacc = jnp.zeros((tq, 2 * dh), F32)
        m1, l1, a1, m2, l2, a2 = lax.fori_loop(0, L // tk, body, (neg, zero, zacc, neg, zero, zacc))
        return a1 / l1, a2 / l2

    @pl.when(qi == 0)
    def _():
        kmax_ref[...] = jnp.max(jnp.abs(k_ref[0].astype(F32)), axis=0, keepdims=True)

    w = jnp.abs(qs.astype(F32)) * kmax_ref[...]
    ub = jnp.maximum(jnp.sum(jnp.where(lane < dh, w, 0.0), axis=-1, keepdims=True),
                     jnp.sum(jnp.where(lane >= dh, w, 0.0), axis=-1, keepdims=True))
    bounded = jnp.max(ub) <= SCORE_BOUND

    def plain_softmax():
        ones = jnp.ones((tk, 2 * dh), BF16)

        nt = (((1,), (1,)), ((), ()))
        a1 = a2 = None
        for j in range(L // tk):
            kb = k_ref[0, j * tk:(j + 1) * tk, :]
            va = jnp.concatenate([v_ref[0, j * tk:(j + 1) * tk, :], ones], axis=1)
            pen = slope * jnp.abs(rel + (q0 - float(j * tk)))
            p1 = jnp.exp(lax.dot_general(q1, kb, nt, preferred_element_type=F32) - pen)
            p2 = jnp.exp(lax.dot_general(q2, kb, nt, preferred_element_type=F32) - pen)
            d1 = jnp.dot(p1.astype(BF16), va, preferred_element_type=F32)
            d2 = jnp.dot(p2.astype(BF16), va, preferred_element_type=F32)
            a1 = d1 if a1 is None else a1 + d1
            a2 = d2 if a2 is None else a2 + d2
        return a1[:, :2 * dh] / a1[:, 2 * dh:], a2[:, :2 * dh] / a2[:, 2 * dh:]

    o1, o2 = lax.cond(bounded, plain_softmax, online_softmax)
    lam = (jnp.exp(jnp.sum(lq1_ref[...] * lk1_ref[...], axis=-1, keepdims=True))
           - jnp.exp(jnp.sum(lq2_ref[...] * lk2_ref[...], axis=-1, keepdims=True)) + lam_init)
    o = o1 - lam * o2
    o = _rms(o, sg_ref[...]) * (1.0 - lam_init)
    o_ref[0] = o.astype(o_ref.dtype)


def _diff_attention(qkv, lq1, lk1, lq2, lk2, subln_g, lam_init, tq, tk):
    B, L, cols = qkv.shape
    W = cols // 3
    H = ATTN_HEADS
    hw = W // H
    dh = hw // 2
    slopes = jnp.asarray([2.0 ** (-8.0 * (i + 1) / H) for i in range(H)], F32)
    vec = lambda a: a.astype(F32)[None, :]
    small = lambda n: pl.BlockSpec((1, n), lambda b, h, i: (0, 0))
    return pl.pallas_call(
        functools.partial(_attn_kernel, L=L, tq=tq, tk=tk, dh=dh, lam_init=lam_init),
        grid=(B, H, L // tq),
        scratch_shapes=[pltpu.VMEM((1, hw), F32)],
        in_specs=[
            pl.BlockSpec(memory_space=pltpu.SMEM),
            pl.BlockSpec((1, tq, hw), lambda b, h, i: (b, i, h)),
            pl.BlockSpec((1, L, hw), lambda b, h, i: (b, 0, H + h)),
            pl.BlockSpec((1, L, hw), lambda b, h, i: (b, 0, 2 * H + h)),
            small(dh), small(dh), small(dh), small(dh), small(hw),
        ],
        out_specs=pl.BlockSpec((1, tq, hw), lambda b, h, i: (b, i, h)),
        out_shape=jax.ShapeDtypeStruct((B, L, W), BF16),
        compiler_params=_cparams(("arbitrary", "arbitrary", "arbitrary")),
        name="diff_attention",
    )(slopes, qkv, qkv, qkv, vec(lq1), vec(lk1), vec(lq2), vec(lk2), vec(subln_g))


def _out_proj_kernel(x_ref, hy_ref, at_ref, wh_ref, wa_ref, g_ref, wr_ref, x2_ref, h2_ref, aff_ref):
    x2 = (x_ref[...] + jnp.dot(hy_ref[...], wh_ref[...], preferred_element_type=F32)
          + jnp.dot(at_ref[...], wa_ref[...], preferred_element_type=F32))
    x2_ref[...] = x2
    h2 = _rms(x2, g_ref[...])
    h2_ref[...] = h2
    wr = wr_ref[...]
    h_hi, w_hi = h2.astype(BF16), wr.astype(BF16)
    h_lo = (h2 - h_hi.astype(F32)).astype(BF16)
    w_lo = (wr - w_hi.astype(F32)).astype(BF16)
    logits = (jnp.dot(h_hi, w_hi, preferred_element_type=F32)
              + (jnp.dot(h_hi, w_lo, preferred_element_type=F32)
                 + jnp.dot(h_lo, w_hi, preferred_element_type=F32)))
    logits = logits.T[:aff_ref.shape[1]]
    e = jnp.exp(logits - jnp.max(logits, axis=0, keepdims=True))
    aff_ref[0] = e / jnp.sum(e, axis=0, keepdims=True)


def _out_proj(x2d, hy2d, at2d, w_out, g, w_router, tm, L):
    T, D = x2d.shape
    wdt = hy2d.shape[1]
    E = w_router.shape[1]
    wh = w_out[:wdt].astype(BF16)
    wa = w_out[wdt:].astype(BF16)
    per = L // tm
    row = lambda n: pl.BlockSpec((tm, n), lambda i: (i, 0))
    return pl.pallas_call(
        _out_proj_kernel,
        grid=(T // tm,),
        in_specs=[row(D), row(wdt), row(at2d.shape[1]),
                  pl.BlockSpec(wh.shape, lambda i: (0, 0)), pl.BlockSpec(wa.shape, lambda i: (0, 0)),
                  pl.BlockSpec((1, D), lambda i: (0, 0)), pl.BlockSpec((D, LANES), lambda i: (0, 0))],
        out_specs=[row(D), row(D), pl.BlockSpec((1, E, tm), lambda i: (i // per, 0, i % per))],
        out_shape=[jax.ShapeDtypeStruct((T, D), F32), jax.ShapeDtypeStruct((T, D), F32),
                   jax.ShapeDtypeStruct((T // L, E, L), F32)],
        compiler_params=_cparams(("arbitrary",)),
        name="out_proj_router",
    )(x2d, hy2d, at2d, wh, wa, g.astype(F32)[None, :],
      jnp.pad(w_router.astype(F32), ((0, 0), (0, LANES - E))))


SEL_CHUNK = 256
SLOT_LO = 32
TOK_LO = 64


def _prefix_count(mask, tri_ref, L):
    parts = []
    carry = jnp.zeros((mask.shape[0], 1), F32)
    for blk in range(L // SEL_CHUNK):
        m = mask[:, blk * SEL_CHUNK:(blk + 1) * SEL_CHUNK]
        parts.append(jnp.dot(m.astype(BF16), tri_ref[...], preferred_element_type=F32) + carry)
        carry = carry + jnp.sum(m, axis=1, keepdims=True)
    return jnp.concatenate(parts, axis=1)


def _select_kernel(aff_ref, tri_ref, idx_ref, gate_ref, *, L, E, cap):
    aff = aff_ref[0]

    def step(i, prefix):
        cand = prefix | jnp.left_shift(jnp.int32(1), 30 - i)
        cnt = jnp.sum((aff >= pltpu.bitcast(cand, F32)).astype(jnp.int32), axis=1, keepdims=True)
        return jnp.where(cnt >= cap, cand, prefix)

    tau = pltpu.bitcast(lax.fori_loop(0, 31, step, jnp.zeros((E, 1), jnp.int32)), F32)
    gt = aff > tau
    eq = aff == tau
    need = cap - jnp.sum(gt.astype(jnp.int32), axis=1, keepdims=True)
    tie_rank = _prefix_count(eq.astype(F32), tri_ref, L) - eq.astype(F32)
    sel = gt | (eq & (tie_rank < need.astype(F32)))
    rank = _prefix_count(sel.astype(F32), tri_ref, L)
    slot = jnp.where(sel, rank - 1.0, -1.0)

    n_hi = cap // SLOT_LO
    s_hi = jnp.floor(slot * (1.0 / SLOT_LO))
    s_lo = slot - SLOT_LO * s_hi
    tok = lax.broadcasted_iota(jnp.int32, (1, L), 1).astype(F32)
    t_hi = jnp.floor(tok * (1.0 / TOK_LO))
    t_lo = tok - TOK_LO * t_hi
    g_hi = aff.astype(BF16)
    r1 = aff - g_hi.astype(F32)
    g_mid = r1.astype(BF16)
    g_lo = (r1 - g_mid.astype(F32)).astype(BF16)
    hi_ids = lax.broadcasted_iota(jnp.int32, (n_hi, 1), 0).astype(F32)
    lo_ids = lax.broadcasted_iota(jnp.int32, (SLOT_LO, 1), 0).astype(F32)
    nt = (((1,), (1,)), ((), ()))
    for e in range(E):
        in_hi = s_hi[e:e + 1, :] == hi_ids
        pieces = [t_hi, t_lo, g_hi[e:e + 1, :].astype(F32), g_mid[e:e + 1, :].astype(F32),
                  g_lo[e:e + 1, :].astype(F32)]
        lhs = jnp.concatenate([jnp.where(in_hi, p, 0.0) for p in pieces], axis=0).astype(BF16)
        in_lo = (s_lo[e:e + 1, :] == lo_ids).astype(BF16)
        out = lax.dot_general(lhs, in_lo, nt, preferred_element_type=F32)
        parts = [out[k * n_hi:(k + 1) * n_hi] for k in range(5)]
        idx_ref[0, e] = (parts[0] * TOK_LO + parts[1]).astype(jnp.int32)
        gate_ref[0, e] = parts[2] + (parts[3] + parts[4])


def _select(aff_t, cap):
    B, E, L = aff_t.shape
    n_hi = cap // SLOT_LO
    tri = np.triu(np.ones((SEL_CHUNK, SEL_CHUNK), np.float32))
    slots = pl.BlockSpec((1, E, n_hi, SLOT_LO), lambda b: (b, 0, 0, 0))
    idx, gate = pl.pallas_call(
        functools.partial(_select_kernel, L=L, E=E, cap=cap),
        grid=(B,),
        in_specs=[pl.BlockSpec((1, E, L), lambda b: (b, 0, 0)),
                  pl.BlockSpec(tri.shape, lambda b: (0, 0))],
        out_specs=[slots, slots],
        out_shape=[jax.ShapeDtypeStruct((B, E, n_hi, SLOT_LO), jnp.int32),
                   jax.ShapeDtypeStruct((B, E, n_hi, SLOT_LO), F32)],
        compiler_params=_cparams(("arbitrary",)),
        name="expert_choice_select",
    )(aff_t, jnp.asarray(tri).astype(BF16))
    return idx.reshape(B, E, cap), gate.reshape(B, E, cap)


ROW_UNROLL = 8


def _gather_kernel(idx_ref, h_ref, o_ref, buf_ref, *, E, cap):
    b = pl.program_id(0)
    e = pl.program_id(1)
    base = (b * E + e) * cap

    def body(i, c):
        for u in range(SUBLANES):
            t = idx_ref[base + i * SUBLANES + u]
            buf_ref[i, pl.ds(u, 1), :] = h_ref[0, pl.ds(t, 1), :]
        return c

    lax.fori_loop(0, cap // SUBLANES, body, 0, unroll=4)
    o_ref[0, 0] = buf_ref[...].reshape(cap, buf_ref.shape[-1]).astype(o_ref.dtype)


def _gather(idx_flat, h2, E, cap):
    B, L, D = h2.shape
    return pl.pallas_call(
        functools.partial(_gather_kernel, E=E, cap=cap),
        grid_spec=pltpu.PrefetchScalarGridSpec(
            num_scalar_prefetch=1,
            grid=(B, E),
            in_specs=[pl.BlockSpec((1, L, D), lambda b, e, idx: (b, 0, 0))],
            out_specs=pl.BlockSpec((1, 1, cap, D), lambda b, e, idx: (e, b, 0, 0)),
            scratch_shapes=[pltpu.VMEM((cap // SUBLANES, SUBLANES, D), F32)],
        ),
        out_shape=jax.ShapeDtypeStruct((E, B, cap, D), BF16),
        compiler_params=_cparams(("arbitrary", "arbitrary")),
        name="expert_gather",
    )(idx_flat, h2)


MXU_TILE = 256


def _ffn_kernel(x_ref, *refs, nchunk):
    wg, wu, wd = refs[:nchunk], refs[nchunk:2 * nchunk], refs[2 * nchunk:3 * nchunk]
    rest = refs[3 * nchunk:]
    x = x_ref[0]
    y = rest[0][0] if len(rest) == 2 else None
    for k in range(nchunk):
        a = jnp.dot(x, wg[k][0].astype(BF16), preferred_element_type=F32)
        u = jnp.dot(x, wu[k][0].astype(BF16), preferred_element_type=F32)
        hmid = (a * jax.nn.sigmoid(a) * u).astype(BF16)
        d = jnp.dot(hmid, wd[k][0].astype(BF16), preferred_element_type=F32)
        y = d if y is None else y + d
    rest[-1][0] = y


def _expert_ffn(xe, w_gate, w_up, w_down, tt, slabs):
    E, T, D = xe.shape
    nchunks = w_gate.shape[2] // MXU_TILE
    per = -(-nchunks // slabs)
    y = None
    for s in range(slabs):
        chunks = range(s * per, min((s + 1) * per, nchunks))
        tile = pl.BlockSpec((1, tt, D), lambda e, i: (e, i, 0))
        col = [pl.BlockSpec((1, D, MXU_TILE), lambda e, i, c=c: (e, 0, c)) for c in chunks]
        row = [pl.BlockSpec((1, MXU_TILE, D), lambda e, i, c=c: (e, c, 0)) for c in chunks]
        n = len(chunks)
        in_specs = [tile] + col + col + row
        args = [xe] + [w_gate] * n + [w_up] * n + [w_down] * n
        if y is not None:
            in_specs.append(tile)
            args.append(y)
        y = pl.pallas_call(
            functools.partial(_ffn_kernel, nchunk=n),
            grid=(E, T // tt),
            in_specs=in_specs,
            out_specs=tile,
            out_shape=jax.ShapeDtypeStruct((E, T, D), F32),
            compiler_params=_cparams(("arbitrary", "arbitrary")),
            name=f"expert_ffn_slab{s}",
        )(*args)
    return y


FIN_ROWS = 512


def _scatter_kernel(idx_ref, gate_ref, y_ref, x_ref, g_ref, o_ref, *, E, cap, L, final):
    b = pl.program_id(0)
    e = pl.program_id(1)
    base = (b * E + e) * cap

    @pl.when(e == 0)
    def _():
        o_ref[...] = jnp.zeros_like(o_ref)

    def body(i, c):
        j0 = i * ROW_UNROLL
        toks = [idx_ref[base + j0 + u] for u in range(ROW_UNROLL)]
        vals = [o_ref[0, pl.ds(toks[u], 1), :]
                + gate_ref[base + j0 + u] * y_ref[0, 0, i, pl.ds(u, 1), :] for u in range(ROW_UNROLL)]
        for u in range(ROW_UNROLL):
            o_ref[0, pl.ds(toks[u], 1), :] = vals[u]
        return c

    lax.fori_loop(0, cap // ROW_UNROLL, body, 0, unroll=4)

    @pl.when(e == E - 1)
    def _():
        def fin(r, c):
            rows = pl.ds(pl.multiple_of(r * FIN_ROWS, FIN_ROWS), FIN_ROWS)
            v = x_ref[0, rows, :] + o_ref[0, rows, :]
            o_ref[0, rows, :] = _rms(v, g_ref[...]) if final else v
            return c

        lax.fori_loop(0, L // FIN_ROWS, fin, 0)


def _scatter(idx_flat, gate_flat, ye, x2, norm_g, final):
    E, B, cap, D = ye.shape
    L = x2.shape[1]
    return pl.pallas_call(
        functools.partial(_scatter_kernel, E=E, cap=cap, L=L, final=final),
        grid_spec=pltpu.PrefetchScalarGridSpec(
            num_scalar_prefetch=2,
            grid=(B, E),
            in_specs=[pl.BlockSpec((1, 1, cap // ROW_UNROLL, ROW_UNROLL, D),
                                   lambda b, e, idx, gt: (e, b, 0, 0, 0)),
                      pl.BlockSpec((1, L, D), lambda b, e, idx, gt: (b, 0, 0),
                                   pipeline_mode=pl.Buffered(1)),
                      pl.BlockSpec((1, D), lambda b, e, idx, gt: (0, 0))],
            out_specs=pl.BlockSpec((1, L, D), lambda b, e, idx, gt: (b, 0, 0)),
        ),
        out_shape=jax.ShapeDtypeStruct((B, L, D), F32),
        compiler_params=_cparams(("arbitrary", "arbitrary")),
        name="expert_scatter_add",
    )(idx_flat, gate_flat, ye.reshape(E, B, cap // ROW_UNROLL, ROW_UNROLL, D), x2,
      norm_g.astype(F32)[None, :])


FFN_SLABS = 2


def _moe(x2, h2, aff_t, w_gate, w_up, w_down, norm_g, tt, final):
    B, L, D = h2.shape
    E = aff_t.shape[1]
    cap = EC_CAPACITY_FACTOR * L // E
    idx, gate = _select(aff_t, cap)
    idx_flat = idx.reshape(-1)
    xe = _gather(idx_flat, h2, E, cap)
    ye = _expert_ffn(xe.reshape(E, B * cap, D), w_gate, w_up, w_down, tt, FFN_SLABS)
    return _scatter(idx_flat, gate.reshape(-1), ye.reshape(E, B, cap, D), x2, norm_g, final)


def kernel(x, attn_norm_g, w_in, conv_w, conv_b, filt_w1, filt_b1, filt_w2, filt_b2, filt_w3, filt_freq, hyena_skip, hyena_norm_g, lambda_q1, lambda_k1, lambda_q2, lambda_k2, subln_g, w_out, ffn_norm_g, w_router, w_gate, w_up, w_down, final_norm_g):
    B, L, D = x.shape
    depth = w_in.shape[0]
    width = hyena_skip.shape[-1]
    hyena_cols = (HYENA_ORDER + 1) * width
    E = w_router.shape[-1]
    tm = min(512, B * L)
    tq = min(512, L)
    consts = _dft_constants(L)
    x2d = x.reshape(B * L, D)
    for l in range(depth):
        lam_init = 0.8 - 0.6 * math.exp(-0.3 * l)
        ph, qkv = _in_proj(x2d, attn_norm_g[l], w_in[l], hyena_cols, tm)
        kf = _hyena_filter_spectra(L, width, filt_w1[l], filt_b1[l], filt_w2[l], filt_b2[l],
                                   filt_w3[l], filt_freq[l], consts)
        hy = _hyena_mixer(ph.reshape(B, L, hyena_cols), conv_w[l], conv_b[l], kf, hyena_skip[l],
                          hyena_norm_g[l], consts)
        at = _diff_attention(qkv.reshape(B, L, -1), lambda_q1[l], lambda_k1[l], lambda_q2[l],
                             lambda_k2[l], subln_g[l], lam_init, min(2 * tq, L), tq)
        x2, h2, aff_t = _out_proj(x2d, hy.reshape(B * L, width), at.reshape(B * L, -1), w_out[l],
                                  ffn_norm_g[l], w_router[l], min(tm // 2, L), L)
        cap = EC_CAPACITY_FACTOR * L // E
        x2d = _moe(x2.reshape(B, L, D), h2.reshape(B, L, D), aff_t, w_gate[l], w_up[l],
                   w_down[l], final_norm_g, tt=min(512, B * cap), final=l + 1 == depth).reshape(B * L, D)
    return x2d.reshape(B, L, D)
```

```python
import functools
import math

import jax
import jax.numpy as jnp
import numpy as np
from jax import lax
from jax.experimental import pallas as pl
from jax.experimental.pallas import tpu as pltpu

F32 = jnp.float32
BF16 = jnp.bfloat16
HIGHEST = lax.Precision.HIGHEST

NORM_EPS = 1e-6
HYENA_ORDER = 2
HYENA_GROUPS = 8
ATTN_HEADS = 4
N_EXPERTS = 16
EC_CAPACITY_FACTOR = 2
FILTER_EMB_DIM = 33
FILTER_EMB_PAD = 40
DECAY_TARGET = 1e-2
FAST_DECAY_PCT = 0.3
SLOW_DECAY_PCT = 1.5

LANES = 128
FFT_FAST = 128
V7X_VMEM_LIMIT = 56 * 1024 * 1024


def _cparams(sem, vmem=V7X_VMEM_LIMIT):
    return pltpu.CompilerParams(dimension_semantics=sem, vmem_limit_bytes=vmem)


SUBLANES = 8


def _half_spectrum(L):
    nk = (2 * L // FFT_FAST) // 2 + 1
    return nk, max(u for u in range(1, 17) if nk % u == 0)


def _dft_constants(L):
    n2 = FFT_FAST
    n = 2 * L
    n1 = n // n2
    s = n1 // 2
    nk, _ = _half_spectrum(L)
    k1 = np.arange(nk)
    th1 = 2.0 * np.pi * np.outer(k1, np.arange(s)) / n1
    f1 = np.empty((2 * nk, s), np.float64)
    f1[0::2], f1[1::2] = np.cos(th1), -np.sin(th1)
    wgt = np.where((k1 == 0) | (k1 == n1 // 2), 1.0, 2.0)[None, :]
    g1 = np.empty((s, 2 * nk), np.float64)
    g1[:, 0::2], g1[:, 1::2] = wgt * np.cos(th1.T), -wgt * np.sin(th1.T)
    eye = np.eye(SUBLANES)
    fwd1 = np.kron(f1, eye)
    inv1 = np.kron(g1, eye)
    a = np.arange(n2)
    ph = 2.0 * np.pi * (np.outer(a, a)[None] / n2 + (k1[:, None, None] * a[None, None, :]) / n)
    tr, ti = np.cos(ph), -np.sin(ph)
    fwd2 = np.concatenate([np.concatenate([tr, -ti], 2), np.concatenate([ti, tr], 2)], 1)
    pht = np.transpose(ph, (0, 2, 1))
    ur, ui = np.cos(pht) / n, np.sin(pht) / n
    inv2 = np.concatenate([np.concatenate([ur, -ui], 2), np.concatenate([ui, ur], 2)], 1)
    return tuple(jnp.asarray(m.astype(np.float32)).astype(BF16) for m in (fwd1, fwd2, inv2, inv1))


def _filter_hidden_kernel(zf_ref, w1_ref, b1_ref, w2_ref, b2_ref, fq_ref, o_ref):
    fq = fq_ref[...]
    hid = jnp.sin(fq * (jnp.dot(zf_ref[...], w1_ref[...], precision=HIGHEST,
                                preferred_element_type=F32) + b1_ref[...]))
    o_ref[...] = jnp.sin(fq * (jnp.dot(hid, w2_ref[...], precision=HIGHEST,
                                       preferred_element_type=F32) + b2_ref[...]))


def _filter_kernel(hid_ref, w3f_ref, w3b_ref, dl_ref, f1_ref, f2_ref, o_ref,
                   hf_ref, hb_ref, sf_ref, sb_ref, *, L):
    s = L // FFT_FAST
    nk, unroll = _half_spectrum(L)
    hid = hid_ref[...]
    row = lax.broadcasted_iota(jnp.int32, (L, 1), 0)
    t = row.astype(F32) * (1.0 / (L - 1))
    win = jnp.exp(-t * dl_ref[...])
    hf = jnp.dot(hid, w3f_ref[...], precision=HIGHEST, preferred_element_type=F32) * win
    hb = jnp.dot(hid, w3b_ref[...], precision=HIGHEST, preferred_element_type=F32) * win
    hb = jnp.where(row == 0, 0.0, hb)
    norm = (jnp.sum(jnp.abs(hf), axis=0, keepdims=True)
            + jnp.sum(jnp.abs(hb), axis=0, keepdims=True))
    cb = hf.shape[1]
    hf_ref[...] = hf.reshape(s, FFT_FAST, cb)
    hb_ref[...] = hb.reshape(s, FFT_FAST, cb)

    def stage1(g, c):
        f0 = pl.multiple_of(g * SUBLANES, SUBLANES)
        xg = jnp.concatenate(
            [hf_ref[:, pl.ds(f0, SUBLANES), :].reshape(s * SUBLANES, cb),
             hb_ref[:, pl.ds(f0, SUBLANES), :].reshape(s * SUBLANES, cb)], axis=1).astype(BF16)
        a = jnp.dot(f1_ref[...], xg, preferred_element_type=F32)
        sf_ref[:, pl.ds(f0, SUBLANES), :] = a[:, :cb].reshape(2 * nk, SUBLANES, cb)
        sb_ref[:, pl.ds(f0, SUBLANES), :] = a[:, cb:].reshape(2 * nk, SUBLANES, cb)
        return c

    lax.fori_loop(0, FFT_FAST // SUBLANES, stage1, 0, unroll=4)
    inv_norm = 1.0 / norm

    def stage2(k1, c):
        blk = jnp.concatenate(
            [sf_ref[pl.ds(2 * k1, 2)].reshape(2 * FFT_FAST, cb),
             sb_ref[pl.ds(2 * k1, 2)].reshape(2 * FFT_FAST, cb)], axis=1).astype(BF16)
        x = jnp.dot(f2_ref[k1], blk, preferred_element_type=F32)
        xf, xb = x[:, :cb], x[:, cb:]
        re = (xf[:FFT_FAST] + xb[:FFT_FAST]) * inv_norm
        im = (xf[FFT_FAST:] - xb[FFT_FAST:]) * inv_norm
        r0 = pl.multiple_of(k1 * 2 * FFT_FAST, 2 * FFT_FAST)
        o_ref[0, pl.ds(r0, 2 * FFT_FAST), :] = jnp.concatenate([re, im], axis=0).astype(o_ref.dtype)
        return c

    lax.fori_loop(0, nk, stage2, 0, unroll=unroll)


def _hyena_filter_spectra(L, width, w1, b1, w2, b2, w3, freq, consts):
    fwd1, fwd2, _, _ = consts
    n1 = 2 * L // FFT_FAST
    nk, _ = _half_spectrum(L)
    cb = LANES
    ncb = width // cb
    hidden = w1.shape[1]
    tt = np.linspace(0.0, 1.0, L)[:, None]
    ww = 2.0 * np.pi * np.arange(L)[:, None] / L
    nb = (FILTER_EMB_DIM - 1) // 2
    bands = np.linspace(1e-4, nb - 1, nb)[None, :]
    feats = np.concatenate([tt, np.cos(bands * ww), -np.sin(bands * ww)], axis=-1)
    feats = np.pad(feats, ((0, 0), (0, FILTER_EMB_PAD - FILTER_EMB_DIM))).astype(np.float32)
    w1p = jnp.pad(w1.astype(F32), ((0, FILTER_EMB_PAD - FILTER_EMB_DIM), (0, 0)))
    min_decay = math.log(DECAY_TARGET) / SLOW_DECAY_PCT
    max_decay = math.log(DECAY_TARGET) / FAST_DECAY_PCT
    deltas = np.abs(np.linspace(min_decay, max_decay, width)).astype(np.float32)[None, :]

    s = n1 // 2
    one = lambda shape: pl.BlockSpec(shape, lambda i: (0,) * len(shape))
    hid = pl.pallas_call(
        _filter_hidden_kernel,
        grid=(1,),
        in_specs=[one((L, FILTER_EMB_PAD)), one((FILTER_EMB_PAD, hidden)), one((1, hidden)),
                  one((hidden, hidden)), one((1, hidden)), one((1, hidden))],
        out_specs=one((L, hidden)),
        out_shape=jax.ShapeDtypeStruct((L, hidden), F32),
        compiler_params=_cparams(("arbitrary",)),
        name="hyena_filter_hidden",
    )(jnp.asarray(feats), w1p, b1.astype(F32)[None, :], w2.astype(F32), b2.astype(F32)[None, :],
      freq.astype(F32)[None, :])
    full = lambda shape: pl.BlockSpec(shape, lambda n, c: (0,) * len(shape))
    return pl.pallas_call(
        functools.partial(_filter_kernel, L=L),
        grid=(HYENA_ORDER, ncb),
        in_specs=[
            full((L, hidden)),
            pl.BlockSpec((hidden, cb), lambda n, c: (0, n * ncb + c)),
            pl.BlockSpec((hidden, cb), lambda n, c: (0, HYENA_ORDER * ncb + n * ncb + c)),
            pl.BlockSpec((1, cb), lambda n, c: (0, c)),
            full(fwd1.shape), full(fwd2.shape),
        ],
        out_specs=pl.BlockSpec((1, nk * 2 * FFT_FAST, cb), lambda n, c: (n, 0, c)),
        out_shape=jax.ShapeDtypeStruct((HYENA_ORDER, nk * 2 * FFT_FAST, width), BF16),
        scratch_shapes=[pltpu.VMEM((s, FFT_FAST, cb), F32), pltpu.VMEM((s, FFT_FAST, cb), F32),
                        pltpu.VMEM((2 * nk, FFT_FAST, cb), F32), pltpu.VMEM((2 * nk, FFT_FAST, cb), F32)],
        compiler_params=_cparams(("arbitrary", "arbitrary")),
        name="hyena_filter_spectra",
    )(hid, w3.astype(F32), w3.astype(F32), jnp.asarray(deltas), fwd1, fwd2)


def _short_conv(p, w, b, L):
    row = lax.broadcasted_iota(jnp.int32, p.shape, 0)
    prev = jnp.where(row == 0, 0.0, pltpu.roll(p, 1, 0))
    nxt = jnp.where(row == L - 1, 0.0, pltpu.roll(p, L - 1, 0))
    return b + prev * w[0:1] + p * w[1:2] + nxt * w[2:3]


def _hyena_kernel(x1_ref, x2_ref, v_ref, w1_ref, w2_ref, wv_ref, b1_ref, b2_ref, bv_ref,
                  skip_ref, ng_ref, kf_ref, f1_ref, f2_ref, f2i_ref, g1_ref, gm_ref,
                  o_ref, z_ref, ga_ref, gb_ref, s_ref, *, L):
    s = L // FFT_FAST
    nk, unroll = _half_spectrum(L)
    cb = o_ref.shape[-1]
    tiled = lambda a: a.reshape(s, FFT_FAST, cb)
    z_ref[...] = tiled(_short_conv(v_ref[0].astype(F32), wv_ref[...], bv_ref[...], L))
    ga_ref[...] = tiled(_short_conv(x1_ref[0].astype(F32), w1_ref[...], b1_ref[...], L))
    gb_ref[...] = tiled(_short_conv(x2_ref[0].astype(F32), w2_ref[...], b2_ref[...], L))
    groups = FFT_FAST // SUBLANES

    for n, gate_ref in enumerate((ga_ref, gb_ref)):
        skip = skip_ref[n:n + 1, :].reshape(1, 1, cb)

        def fwd1(gp, c):
            fa = pl.multiple_of(gp * 2 * SUBLANES, 2 * SUBLANES)
            fb = fa + SUBLANES
            xg = jnp.concatenate(
                [z_ref[:, pl.ds(fa, SUBLANES), :].reshape(s * SUBLANES, cb),
                 z_ref[:, pl.ds(fb, SUBLANES), :].reshape(s * SUBLANES, cb)], axis=1).astype(BF16)
            a = jnp.dot(f1_ref[...], xg, preferred_element_type=F32)
            s_ref[:, pl.ds(fa, SUBLANES), :] = a[:, :cb].reshape(2 * nk, SUBLANES, cb)
            s_ref[:, pl.ds(fb, SUBLANES), :] = a[:, cb:].reshape(2 * nk, SUBLANES, cb)
            return c

        lax.fori_loop(0, groups // 2, fwd1, 0, unroll=4)

        def mid(k1, c):
            blk = s_ref[pl.ds(2 * k1, 2)].reshape(2 * FFT_FAST, cb).astype(BF16)
            x = jnp.dot(f2_ref[k1], blk, preferred_element_type=F32)
            r0 = pl.multiple_of(k1 * 2 * FFT_FAST, 2 * FFT_FAST)
            kk = kf_ref[n, pl.ds(r0, 2 * FFT_FAST), :].astype(F32)
            xr, xi = x[:FFT_FAST], x[FFT_FAST:]
            kr, ki = kk[:FFT_FAST], kk[FFT_FAST:]
            y = jnp.concatenate([xr * kr - xi * ki, xr * ki + xi * kr], axis=0).astype(BF16)
            s_ref[pl.ds(2 * k1, 2)] = jnp.dot(f2i_ref[k1], y, preferred_element_type=F32).reshape(
                2, FFT_FAST, cb)
            return c

        lax.fori_loop(0, nk, mid, 0, unroll=unroll)

        def inv1(gp, c):
            fa = pl.multiple_of(gp * 2 * SUBLANES, 2 * SUBLANES)
            fb = fa + SUBLANES
            bg = jnp.concatenate(
                [s_ref[:, pl.ds(fa, SUBLANES), :].reshape(2 * nk * SUBLANES, cb),
                 s_ref[:, pl.ds(fb, SUBLANES), :].reshape(2 * nk * SUBLANES, cb)], axis=1).astype(BF16)
            y = jnp.dot(g1_ref[...], bg, preferred_element_type=F32)
            for f0, yh in ((fa, y[:, :cb]), (fb, y[:, cb:])):
                zf = z_ref[:, pl.ds(f0, SUBLANES), :]
                gf = gate_ref[:, pl.ds(f0, SUBLANES), :]
                z_ref[:, pl.ds(f0, SUBLANES), :] = gf * (yh.reshape(s, SUBLANES, cb) + skip * zf)
            return c

        lax.fori_loop(0, groups // 2, inv1, 0, unroll=4)

    z = z_ref[...].reshape(L, cb)
    zz = z * z
    hi = zz.astype(BF16)
    lo = (zz - hi.astype(F32)).astype(BF16)
    ms = (jnp.dot(hi, gm_ref[...], preferred_element_type=F32)
          + jnp.dot(lo, gm_ref[...], preferred_element_type=F32))
    o_ref[0] = (z * lax.rsqrt(ms + NORM_EPS) * ng_ref[...]).astype(o_ref.dtype)


def _hyena_mixer(proj_h, conv_w, conv_b, kf, skip, norm_g, consts):
    B, L, cols = proj_h.shape
    width = cols // (HYENA_ORDER + 1)
    cb = 2 * LANES
    ncb = width // cb
    gsz = width // HYENA_GROUPS
    n1 = 2 * L // FFT_FAST
    nk, _ = _half_spectrum(L)
    fwd1, fwd2, inv2, inv1 = consts
    gm = np.kron(np.eye(cb // gsz), np.full((gsz, gsz), 1.0 / gsz)).astype(np.float32)

    def stream(i):
        return pl.BlockSpec((1, L, cb), lambda c, b: (b, 0, i * ncb + c))

    def cvec(rows, i):
        return pl.BlockSpec((rows, cb), lambda c, b: (0, i * ncb + c))

    def const(shape):
        return pl.BlockSpec(shape, lambda c, b: (0,) * len(shape), pipeline_mode=pl.Buffered(1))

    cw = conv_w.astype(F32)
    cbias = conv_b.astype(F32)[None, :]
    return pl.pallas_call(
        functools.partial(_hyena_kernel, L=L),
        grid=(ncb, B),
        in_specs=[
            stream(0), stream(1), stream(2),
            cvec(3, 0), cvec(3, 1), cvec(3, 2),
            cvec(1, 0), cvec(1, 1), cvec(1, 2),
            pl.BlockSpec((HYENA_ORDER, cb), lambda c, b: (0, c)),
            pl.BlockSpec((1, cb), lambda c, b: (0, c)),
            pl.BlockSpec((HYENA_ORDER, nk * 2 * FFT_FAST, cb), lambda c, b: (0, 0, c),
                         pipeline_mode=pl.Buffered(1)),
            const(fwd1.shape), const(fwd2.shape), const(inv2.shape), const(inv1.shape),
            const(gm.shape),
        ],
        out_specs=pl.BlockSpec((1, L, cb), lambda c, b: (b, 0, c)),
        out_shape=jax.ShapeDtypeStruct((B, L, width), BF16),
        scratch_shapes=[pltpu.VMEM((n1 // 2, FFT_FAST, cb), F32), pltpu.VMEM((n1 // 2, FFT_FAST, cb), F32),
                        pltpu.VMEM((n1 // 2, FFT_FAST, cb), F32), pltpu.VMEM((2 * nk, FFT_FAST, cb), F32)],
        compiler_params=_cparams(("arbitrary", "arbitrary")),
        name="hyena_mixer",
    )(proj_h, proj_h, proj_h, cw, cw, cw, cbias, cbias, cbias,
      skip.astype(F32), norm_g.astype(F32)[None, :], kf,
      fwd1, fwd2, inv2, inv1, jnp.asarray(gm).astype(BF16))


def _rms(x, g):
    return x * lax.rsqrt(jnp.mean(x * x, axis=-1, keepdims=True) + NORM_EPS) * g


def _in_proj_kernel(x_ref, g_ref, wh_ref, wa_ref, oh_ref, oa_ref):
    hn = _rms(x_ref[...], g_ref[...]).astype(BF16)
    oh_ref[...] = jnp.dot(hn, wh_ref[...], preferred_element_type=F32).astype(oh_ref.dtype)
    oa_ref[...] = jnp.dot(hn, wa_ref[...], preferred_element_type=F32).astype(oa_ref.dtype)


def _in_proj(x2d, g, w_in, hyena_cols, tm):
    T, D = x2d.shape
    attn_cols = w_in.shape[1] - hyena_cols
    wh = w_in[:, :hyena_cols].astype(BF16)
    wa = w_in[:, hyena_cols:].astype(BF16)
    return pl.pallas_call(
        _in_proj_kernel,
        grid=(T // tm,),
        in_specs=[
            pl.BlockSpec((tm, D), lambda i: (i, 0)),
            pl.BlockSpec((1, D), lambda i: (0, 0)),
            pl.BlockSpec((D, hyena_cols), lambda i: (0, 0)),
            pl.BlockSpec((D, attn_cols), lambda i: (0, 0)),
        ],
        out_specs=[pl.BlockSpec((tm, hyena_cols), lambda i: (i, 0)),
                   pl.BlockSpec((tm, attn_cols), lambda i: (i, 0))],
        out_shape=[jax.ShapeDtypeStruct((T, hyena_cols), BF16),
                   jax.ShapeDtypeStruct((T, attn_cols), BF16)],
        compiler_params=_cparams(("arbitrary",)),
        name="in_proj",
    )(x2d, g.astype(F32)[None, :], wh, wa)


SCORE_BOUND = 60.0


def _attn_kernel(slope_ref, q_ref, k_ref, v_ref, lq1_ref, lk1_ref, lq2_ref, lk2_ref, sg_ref, o_ref,
                 kmax_ref, *, L, tq, tk, dh, lam_init):
    h = pl.program_id(1)
    qi = pl.program_id(2)
    slope = slope_ref[h]
    q = q_ref[0]
    lane = lax.broadcasted_iota(jnp.int32, q.shape, 1)
    qs = q * jnp.asarray(dh ** -0.5, q.dtype)
    q1 = jnp.where(lane < dh, qs, jnp.zeros_like(qs))
    q2 = jnp.where(lane >= dh, qs, jnp.zeros_like(qs))
    rel = (lax.broadcasted_iota(jnp.int32, (tq, tk), 0)
           - lax.broadcasted_iota(jnp.int32, (tq, tk), 1)).astype(F32)
    q0 = (qi * tq).astype(F32)

    def update(s, m, l, acc, vb):
        m_new = jnp.maximum(m, jnp.max(s, axis=-1, keepdims=True))
        alpha = jnp.exp(m - m_new)
        p = jnp.exp(s - m_new)
        l = alpha * l + jnp.sum(p, axis=-1, keepdims=True)
        acc = alpha * acc + jnp.dot(p.astype(BF16), vb, preferred_element_type=F32)
        return m_new, l, acc

    def body(j, carry):
        m1, l1, a1, m2, l2, a2 = carry
        k0 = pl.multiple_of(j * tk, tk)
        kb = k_ref[0, pl.ds(k0, tk), :]
        vb = v_ref[0, pl.ds(k0, tk), :]
        pen = slope * jnp.abs(rel + (q0 - k0.astype(F32)))
        nt = (((1,), (1,)), ((), ()))
        s1 = lax.dot_general(q1, kb, nt, preferred_element_type=F32) - pen
        s2 = lax.dot_general(q2, kb, nt, preferred_element_type=F32) - pen
        m1, l1, a1 = update(s1, m1, l1, a1, vb)
        m2, l2, a2 = update(s2, m2, l2, a2, vb)
        return m1, l1, a1, m2, l2, a2

    def online_softmax():
        neg = jnp.full((tq, 1), -1e30, F32)
        zero = jnp.zeros((tq, 1), F32)
        zacc = jnp.zeros((tq, 2 * dh), F32)
        m1, l1, a1, m2, l2, a2 = lax.fori_loop(0, L // tk, body, (neg, zero, zacc, neg, zero, zacc))
        return a1 / l1, a2 / l2

    @pl.when(qi == 0)
    def _():
        kmax_ref[...] = jnp.max(jnp.abs(k_ref[0].astype(F32)), axis=0, keepdims=True)

    w = jnp.abs(qs.astype(F32)) * kmax_ref[...]
    ub = jnp.maximum(jnp.sum(jnp.where(lane < dh, w, 0.0), axis=-1, keepdims=True),
                     jnp.sum(jnp.where(lane >= dh, w, 0.0), axis=-1, keepdims=True))
    bounded = jnp.max(ub) <= SCORE_BOUND

    def plain_softmax():
        ones = jnp.ones((tk, 2 * dh), BF16)

        nt = (((1,), (1,)), ((), ()))
        a1 = a2 = None
        for j in range(L // tk):
            kb = k_ref[0, j * tk:(j + 1) * tk, :]
            va = jnp.concatenate([v_ref[0, j * tk:(j + 1) * tk, :], ones], axis=1)
            pen = slope * jnp.abs(rel + (q0 - float(j * tk)))
            p1 = jnp.exp(lax.dot_general(q1, kb, nt, preferred_element_type=F32) - pen)
            p2 = jnp.exp(lax.dot_general(q2, kb, nt, preferred_element_type=F32) - pen)
            d1 = jnp.dot(p1.astype(BF16), va, preferred_element_type=F32)
            d2 = jnp.dot(p2.astype(BF16), va, preferred_element_type=F32)
            a1 = d1 if a1 is None else a1 + d1
            a2 = d2 if a2 is None else a2 + d2
        return a1[:, :2 * dh] / a1[:, 2 * dh:], a2[:, :2 * dh] / a2[:, 2 * dh:]

    o1, o2 = lax.cond(bounded, plain_softmax, online_softmax)
    lam = (jnp.exp(jnp.sum(lq1_ref[...] * lk1_ref[...], axis=-1, keepdims=True))
           - jnp.exp(jnp.sum(lq2_ref[...] * lk2_ref[...], axis=-1, keepdims=True)) + lam_init)
    o = o1 - lam * o2
    o = _rms(o, sg_ref[...]) * (1.0 - lam_init)
    o_ref[0] = o.astype(o_ref.dtype)


def _diff_attention(qkv, lq1, lk1, lq2, lk2, subln_g, lam_init, tq, tk):
    B, L, cols = qkv.shape
    W = cols // 3
    H = ATTN_HEADS
    hw = W // H
    dh = hw // 2
    slopes = jnp.asarray([2.0 ** (-8.0 * (i + 1) / H) for i in range(H)], F32)
    vec = lambda a: a.astype(F32)[None, :]
    small = lambda n: pl.BlockSpec((1, n), lambda b, h, i: (0, 0))
    return pl.pallas_call(
        functools.partial(_attn_kernel, L=L, tq=tq, tk=tk, dh=dh, lam_init=lam_init),
        grid=(B, H, L // tq),
        scratch_shapes=[pltpu.VMEM((1, hw), F32)],
        in_specs=[
            pl.BlockSpec(memory_space=pltpu.SMEM),
            pl.BlockSpec((1, tq, hw), lambda b, h, i: (b, i, h)),
            pl.BlockSpec((1, L, hw), lambda b, h, i: (b, 0, H + h)),
            pl.BlockSpec((1, L, hw), lambda b, h, i: (b, 0, 2 * H + h)),
            small(dh), small(dh), small(dh), small(dh), small(hw),
        ],
        out_specs=pl.BlockSpec((1, tq, hw), lambda b, h, i: (b, i, h)),
        out_shape=jax.ShapeDtypeStruct((B, L, W), BF16),
        compiler_params=_cparams(("arbitrary", "arbitrary", "arbitrary")),
        name="diff_attention",
    )(slopes, qkv, qkv, qkv, vec(lq1), vec(lk1), vec(lq2), vec(lk2), vec(subln_g))


def _out_proj_kernel(x_ref, hy_ref, at_ref, wh_ref, wa_ref, g_ref, wr_ref, x2_ref, h2_ref, aff_ref):
    x2 = (x_ref[...] + jnp.dot(hy_ref[...], wh_ref[...], preferred_element_type=F32)
          + jnp.dot(at_ref[...], wa_ref[...], preferred_element_type=F32))
    x2_ref[...] = x2
    h2 = _rms(x2, g_ref[...])
    h2_ref[...] = h2
    wr = wr_ref[...]
    h_hi, w_hi = h2.astype(BF16), wr.astype(BF16)
    h_lo = (h2 - h_hi.astype(F32)).astype(BF16)
    w_lo = (wr - w_hi.astype(F32)).astype(BF16)
    logits = (jnp.dot(h_hi, w_hi, preferred_element_type=F32)
              + (jnp.dot(h_hi, w_lo, preferred_element_type=F32)
                 + jnp.dot(h_lo, w_hi, preferred_element_type=F32)))
    logits = logits.T[:aff_ref.shape[1]]
    e = jnp.exp(logits - jnp.max(logits, axis=0, keepdims=True))
    aff_ref[0] = e / jnp.sum(e, axis=0, keepdims=True)


def _out_proj(x2d, hy2d, at2d, w_out, g, w_router, tm, L):
    T, D = x2d.shape
    wdt = hy2d.shape[1]
    E = w_router.shape[1]
    wh = w_out[:wdt].astype(BF16)
    wa = w_out[wdt:].astype(BF16)
    per = L // tm
    row = lambda n: pl.BlockSpec((tm, n), lambda i: (i, 0))
    return pl.pallas_call(
        _out_proj_kernel,
        grid=(T // tm,),
        in_specs=[row(D), row(wdt), row(at2d.shape[1]),
                  pl.BlockSpec(wh.shape, lambda i: (0, 0)), pl.BlockSpec(wa.shape, lambda i: (0, 0)),
                  pl.BlockSpec((1, D), lambda i: (0, 0)), pl.BlockSpec((D, LANES), lambda i: (0, 0))],
        out_specs=[row(D), row(D), pl.BlockSpec((1, E, tm), lambda i: (i // per, 0, i % per))],
        out_shape=[jax.ShapeDtypeStruct((T, D), F32), jax.ShapeDtypeStruct((T, D), F32),
                   jax.ShapeDtypeStruct((T // L, E, L), F32)],
        compiler_params=_cparams(("arbitrary",)),
        name="out_proj_router",
    )(x2d, hy2d, at2d, wh, wa, g.astype(F32)[None, :],
      jnp.pad(w_router.astype(F32), ((0, 0), (0, LANES - E))))


SEL_CHUNK = 256
SLOT_LO = 32
TOK_LO = 64


def _prefix_count(mask, tri_ref, L):
    parts = []
    carry = jnp.zeros((mask.shape[0], 1), F32)
    for blk in range(L // SEL_CHUNK):
        m = mask[:, blk * SEL_CHUNK:(blk + 1) * SEL_CHUNK]
        parts.append(jnp.dot(m.astype(BF16), tri_ref[...], preferred_element_type=F32) + carry)
        carry = carry + jnp.sum(m, axis=1, keepdims=True)
    return jnp.concatenate(parts, axis=1)


def _select_kernel(aff_ref, tri_ref, idx_ref, gate_ref, *, L, E, cap):
    aff = aff_ref[0]

    def step(i, prefix):
        cand = prefix | jnp.left_shift(jnp.int32(1), 30 - i)
        cnt = jnp.sum((aff >= pltpu.bitcast(cand, F32)).astype(jnp.int32), axis=1, keepdims=True)
        return jnp.where(cnt >= cap, cand, prefix)

    tau = pltpu.bitcast(lax.fori_loop(0, 31, step, jnp.zeros((E, 1), jnp.int32)), F32)
    gt = aff > tau
    eq = aff == tau
    need = cap - jnp.sum(gt.astype(jnp.int32), axis=1, keepdims=True)
    tie_rank = _prefix_count(eq.astype(F32), tri_ref, L) - eq.astype(F32)
    sel = gt | (eq & (tie_rank < need.astype(F32)))
    rank = _prefix_count(sel.astype(F32), tri_ref, L)
    slot = jnp.where(sel, rank - 1.0, -1.0)

    n_hi = cap // SLOT_LO
    s_hi = jnp.floor(slot * (1.0 / SLOT_LO))
    s_lo = slot - SLOT_LO * s_hi
    tok = lax.broadcasted_iota(jnp.int32, (1, L), 1).astype(F32)
    t_hi = jnp.floor(tok * (1.0 / TOK_LO))
    t_lo = tok - TOK_LO * t_hi
    g_hi = aff.astype(BF16)
    r1 = aff - g_hi.astype(F32)
    g_mid = r1.astype(BF16)
    g_lo = (r1 - g_mid.astype(F32)).astype(BF16)
    hi_ids = lax.broadcasted_iota(jnp.int32, (n_hi, 1), 0).astype(F32)
    lo_ids = lax.broadcasted_iota(jnp.int32, (SLOT_LO, 1), 0).astype(F32)
    nt = (((1,), (1,)), ((), ()))
    for e in range(E):
        in_hi = s_hi[e:e + 1, :] == hi_ids
        pieces = [t_hi, t_lo, g_hi[e:e + 1, :].astype(F32), g_mid[e:e + 1, :].astype(F32),
                  g_lo[e:e + 1, :].astype(F32)]
        lhs = jnp.concatenate([jnp.where(in_hi, p, 0.0) for p in pieces], axis=0).astype(BF16)
        in_lo = (s_lo[e:e + 1, :] == lo_ids).astype(BF16)
        out = lax.dot_general(lhs, in_lo, nt, preferred_element_type=F32)
        parts = [out[k * n_hi:(k + 1) * n_hi] for k in range(5)]
        idx_ref[0, e] = (parts[0] * TOK_LO + parts[1]).astype(jnp.int32)
        gate_ref[0, e] = parts[2] + (parts[3] + parts[4])


def _select(aff_t, cap):
    B, E, L = aff_t.shape
    n_hi = cap // SLOT_LO
    tri = np.triu(np.ones((SEL_CHUNK, SEL_CHUNK), np.float32))
    slots = pl.BlockSpec((1, E, n_hi, SLOT_LO), lambda b: (b, 0, 0, 0))
    idx, gate = pl.pallas_call(
        functools.partial(_select_kernel, L=L, E=E, cap=cap),
        grid=(B,),
        in_specs=[pl.BlockSpec((1, E, L), lambda b: (b, 0, 0)),
                  pl.BlockSpec(tri.shape, lambda b: (0, 0))],
        out_specs=[slots, slots],
        out_shape=[jax.ShapeDtypeStruct((B, E, n_hi, SLOT_LO), jnp.int32),
                   jax.ShapeDtypeStruct((B, E, n_hi, SLOT_LO), F32)],
        compiler_params=_cparams(("arbitrary",)),
        name="expert_choice_select",
    )(aff_t, jnp.asarray(tri).astype(BF16))
    return idx.reshape(B, E, cap), gate.reshape(B, E, cap)


ROW_UNROLL = 8


def _gather_kernel(idx_ref, h_ref, o_ref, buf_ref, *, E, cap):
    b = pl.program_id(0)
    e = pl.program_id(1)
    base = (b * E + e) * cap

    def body(i, c):
        for u in range(SUBLANES):
            t = idx_ref[base + i * SUBLANES + u]
            buf_ref[i, pl.ds(u, 1), :] = h_ref[0, pl.ds(t, 1), :]
        return c

    lax.fori_loop(0, cap // SUBLANES, body, 0, unroll=4)
    o_ref[0, 0] = buf_ref[...].reshape(cap, buf_ref.shape[-1]).astype(o_ref.dtype)


def _gather(idx_flat, h2, E, cap):
    B, L, D = h2.shape
    return pl.pallas_call(
        functools.partial(_gather_kernel, E=E, cap=cap),
        grid_spec=pltpu.PrefetchScalarGridSpec(
            num_scalar_prefetch=1,
            grid=(B, E),
            in_specs=[pl.BlockSpec((1, L, D), lambda b, e, idx: (b, 0, 0))],
            out_specs=pl.BlockSpec((1, 1, cap, D), lambda b, e, idx: (e, b, 0, 0)),
            scratch_shapes=[pltpu.VMEM((cap // SUBLANES, SUBLANES, D), F32)],
        ),
        out_shape=jax.ShapeDtypeStruct((E, B, cap, D), BF16),
        compiler_params=_cparams(("arbitrary", "arbitrary")),
        name="expert_gather",
    )(idx_flat, h2)


MXU_TILE = 256


def _ffn_kernel(x_ref, *refs, nchunk):
    wg, wu, wd = refs[:nchunk], refs[nchunk:2 * nchunk], refs[2 * nchunk:3 * nchunk]
    rest = refs[3 * nchunk:]
    x = x_ref[0]
    y = rest[0][0] if len(rest) == 2 else None
    for k in range(nchunk):
        a = jnp.dot(x, wg[k][0].astype(BF16), preferred_element_type=F32)
        u = jnp.dot(x, wu[k][0].astype(BF16), preferred_element_type=F32)
        hmid = (a * jax.nn.sigmoid(a) * u).astype(BF16)
        d = jnp.dot(hmid, wd[k][0].astype(BF16), preferred_element_type=F32)
        y = d if y is None else y + d
    rest[-1][0] = y


def _expert_ffn(xe, w_gate, w_up, w_down, tt, slabs):
    E, T, D = xe.shape
    nchunks = w_gate.shape[2] // MXU_TILE
    per = -(-nchunks // slabs)
    y = None
    for s in range(slabs):
        chunks = range(s * per, min((s + 1) * per, nchunks))
        tile = pl.BlockSpec((1, tt, D), lambda e, i: (e, i, 0))
        col = [pl.BlockSpec((1, D, MXU_TILE), lambda e, i, c=c: (e, 0, c)) for c in chunks]
        row = [pl.BlockSpec((1, MXU_TILE, D), lambda e, i, c=c: (e, c, 0)) for c in chunks]
        n = len(chunks)
        in_specs = [tile] + col + col + row
        args = [xe] + [w_gate] * n + [w_up] * n + [w_down] * n
        if y is not None:
            in_specs.append(tile)
            args.append(y)
        y = pl.pallas_call(
            functools.partial(_ffn_kernel, nchunk=n),
            grid=(E, T // tt),
            in_specs=in_specs,
            out_specs=tile,
            out_shape=jax.ShapeDtypeStruct((E, T, D), F32),
            compiler_params=_cparams(("arbitrary", "arbitrary")),
            name=f"expert_ffn_slab{s}",
        )(*args)
    return y


FIN_ROWS = 512


def _scatter_kernel(idx_ref, gate_ref, y_ref, x_ref, g_ref, o_ref, *, E, cap, L, final):
    b = pl.program_id(0)
    e = pl.program_id(1)
    base = (b * E + e) * cap

    @pl.when(e == 0)
    def _():
        o_ref[...] = jnp.zeros_like(o_ref)

    def body(i, c):
        j0 = i * ROW_UNROLL
        toks = [idx_ref[base + j0 + u] for u in range(ROW_UNROLL)]
        vals = [o_ref[0, pl.ds(toks[u], 1), :]
                + gate_ref[base + j0 + u] * y_ref[0, 0, i, pl.ds(u, 1), :] for u in range(ROW_UNROLL)]
        for u in range(ROW_UNROLL):
            o_ref[0, pl.ds(toks[u], 1), :] = vals[u]
        return c

    lax.fori_loop(0, cap // ROW_UNROLL, body, 0, unroll=4)

    @pl.when(e == E - 1)
    def _():
        def fin(r, c):
            rows = pl.ds(pl.multiple_of(r * FIN_ROWS, FIN_ROWS), FIN_ROWS)
            v = x_ref[0, rows, :] + o_ref[0, rows, :]
            o_ref[0, rows, :] = _rms(v, g_ref[...]) if final else v
            return c

        lax.fori_loop(0, L // FIN_ROWS, fin, 0)


def _scatter(idx_flat, gate_flat, ye, x2, norm_g, final):
    E, B, cap, D = ye.shape
    L = x2.shape[1]
    return pl.pallas_call(
        functools.partial(_scatter_kernel, E=E, cap=cap, L=L, final=final),
        grid_spec=pltpu.PrefetchScalarGridSpec(
            num_scalar_prefetch=2,
            grid=(B, E),
            in_specs=[pl.BlockSpec((1, 1, cap // ROW_UNROLL, ROW_UNROLL, D),
                                   lambda b, e, idx, gt: (e, b, 0, 0, 0)),
                      pl.BlockSpec((1, L, D), lambda b, e, idx, gt: (b, 0, 0),
                                   pipeline_mode=pl.Buffered(1)),
                      pl.BlockSpec((1, D), lambda b, e, idx, gt: (0, 0))],
            out_specs=pl.BlockSpec((1, L, D), lambda b, e, idx, gt: (b, 0, 0)),
        ),
        out_shape=jax.ShapeDtypeStruct((B, L, D), F32),
        compiler_params=_cparams(("arbitrary", "arbitrary")),
        name="expert_scatter_add",
    )(idx_flat, gate_flat, ye.reshape(E, B, cap // ROW_UNROLL, ROW_UNROLL, D), x2,
      norm_g.astype(F32)[None, :])


FFN_SLABS = 3


def _moe(x2, h2, aff_t, w_gate, w_up, w_down, norm_g, tt, final):
    B, L, D = h2.shape
    E = aff_t.shape[1]
    cap = EC_CAPACITY_FACTOR * L // E
    idx, gate = _select(aff_t, cap)
    idx_flat = idx.reshape(-1)
    xe = _gather(idx_flat, h2, E, cap)
    ye = _expert_ffn(xe.reshape(E, B * cap, D), w_gate, w_up, w_down, tt, FFN_SLABS)
    return _scatter(idx_flat, gate.reshape(-1), ye.reshape(E, B, cap, D), x2, norm_g, final)


def kernel(x, attn_norm_g, w_in, conv_w, conv_b, filt_w1, filt_b1, filt_w2, filt_b2, filt_w3, filt_freq, hyena_skip, hyena_norm_g, lambda_q1, lambda_k1, lambda_q2, lambda_k2, subln_g, w_out, ffn_norm_g, w_router, w_gate, w_up, w_down, final_norm_g):
    B, L, D = x.shape
    depth = w_in.shape[0]
    width = hyena_skip.shape[-1]
    hyena_cols = (HYENA_ORDER + 1) * width
    E = w_router.shape[-1]
    tm = min(512, B * L)
    tq = min(512, L)
    consts = _dft_constants(L)
    x2d = x.reshape(B * L, D)
    for l in range(depth):
        lam_init = 0.8 - 0.6 * math.exp(-0.3 * l)
        ph, qkv = _in_proj(x2d, attn_norm_g[l], w_in[l], hyena_cols, tm)
        kf = _hyena_filter_spectra(L, width, filt_w1[l], filt_b1[l], filt_w2[l], filt_b2[l],
                                   filt_w3[l], filt_freq[l], consts)
        hy = _hyena_mixer(ph.reshape(B, L, hyena_cols), conv_w[l], conv_b[l], kf, hyena_skip[l],
                          hyena_norm_g[l], consts)
        at = _diff_attention(qkv.reshape(B, L, -1), lambda_q1[l], lambda_k1[l], lambda_q2[l],
                             lambda_k2[l], subln_g[l], lam_init, min(2 * tq, L), tq)
        x2, h2, aff_t = _out_proj(x2d, hy.reshape(B * L, width), at.reshape(B * L, -1), w_out[l],
                                  ffn_norm_g[l], w_router[l], min(tm // 2, L), L)
        cap = EC_CAPACITY_FACTOR * L // E
        x2d = _moe(x2.reshape(B, L, D), h2.reshape(B, L, D), aff_t, w_gate[l], w_up[l],
                   w_down[l], final_norm_g, tt=min(1024, B * cap), final=l + 1 == depth).reshape(B * L, D)
    return x2d.reshape(B, L, D)
```
